```python
import math
import jax, jax.numpy as jnp
from jax import lax
import numpy as np


D_MODEL = 1024
BATCH = 1
SEQ = 16384
DEPTH = 4
DEC_BATCH = 8
DEC_SEQ = 2048
PAST_LEN = 128

GRID_W = 64
N_HEADS = 8
N_KV_HEADS = 2
HEAD_DIM = 64
GROUP = N_HEADS // N_KV_HEADS
ROPE_THETA = 10000.0
ROPE_AXIS_DIM = HEAD_DIM // 2
Q_BLOCK = 128
ATT_W = N_HEADS * HEAD_DIM
KV_W = N_KV_HEADS * HEAD_DIM
HY_W = D_MODEL // 2
HY_EMB = 33
HY_BANDS = (HY_EMB - 1) // 2
HY_FILTER_W = 64
HY_TARGET = 1e-2
HY_FAST = 0.3
HY_SLOW = 1.5
GLA_HEADS = 4
GLA_DK = 64
GLA_DV = 128
GLA_K = GLA_HEADS * GLA_DK
GLA_V = GLA_HEADS * GLA_DV
GLA_RANK = 16
GLA_TAU = 16.0
GLA_CHUNK = 64
N_BRANCH = 3
MIX_W = 512
D_FF = 2816
EPS = 1e-6
IN_SIZES = (ATT_W, KV_W, KV_W, 3 * HY_W, GLA_K, GLA_K, GLA_V, GLA_V, 2 * GLA_RANK, N_BRANCH * D_MODEL)
IN_COLS = ATT_W + 2 * KV_W + 3 * HY_W + 2 * GLA_K + 2 * GLA_V + 2 * GLA_RANK + N_BRANCH * D_MODEL

kernel_name = 'hybrid_bidir_encoder'


def rms_norm(x, g):
    xf = x.astype(jnp.float32)
    y = xf * lax.rsqrt(jnp.mean(xf * xf, axis=-1, keepdims=True) + EPS)
    return (y * g.astype(jnp.float32)).astype(x.dtype)


def split_cols(z):
    out = []
    off = 0
    for s in IN_SIZES:
        out.append(z[..., off:off + s])
        off += s
    return out


def axial_rope(L):
    rows = L // GRID_W
    r = jnp.repeat(jnp.arange(rows, dtype=jnp.float32), GRID_W)
    c = jnp.tile(jnp.arange(GRID_W, dtype=jnp.float32), rows)
    inv = ROPE_THETA ** (-jnp.arange(0, ROPE_AXIS_DIM, 2, dtype=jnp.float32) / ROPE_AXIS_DIM)
    ang = jnp.concatenate([r[:, None] * inv, c[:, None] * inv], axis=-1)
    return jnp.cos(ang), jnp.sin(ang)


def apply_rope(x, cos, sin):
    half = HEAD_DIM // 2
    x1, x2 = x[..., :half], x[..., half:]
    return jnp.concatenate([x1 * cos - x2 * sin, x1 * sin + x2 * cos], axis=-1)


def gqa_attention(q, k, v, q_norm_g, k_norm_g, cos, sin):
    B, L, _ = q.shape
    q = rms_norm(q.astype(jnp.float32).reshape(B, L, N_KV_HEADS, GROUP, HEAD_DIM), q_norm_g)
    k = rms_norm(k.astype(jnp.float32).reshape(B, L, N_KV_HEADS, HEAD_DIM), k_norm_g)
    v = v.reshape(B, L, N_KV_HEADS, HEAD_DIM)
    q = apply_rope(q, cos[None, :, None, None, :], sin[None, :, None, None, :])
    k = apply_rope(k, cos[None, :, None, :], sin[None, :, None, :])
    nb = L // Q_BLOCK
    qb = jnp.moveaxis(q.reshape(B, nb, Q_BLOCK, N_KV_HEADS, GROUP, HEAD_DIM), 1, 0)
    scale = HEAD_DIM ** -0.5

    def block(qi):
        s = jnp.einsum('bqkgd,bskd->bkgqs', qi, k) * scale
        p = jax.nn.softmax(s, axis=-1)
        return jnp.einsum('bkgqs,bskd->bqkgd', p.astype(v.dtype), v)

    o = lax.map(block, qb)
    return jnp.moveaxis(o, 0, 1).reshape(B, L, ATT_W).astype(v.dtype)


def short_conv3(u, w, b):
    up = jnp.pad(u, ((0, 0), (1, 1), (0, 0)))
    return up[:, :-2] * w[0] + up[:, 1:-1] * w[1] + up[:, 2:] * w[2] + b


def hyena_filter(L, w1, b1, f1, w2, b2, f2, w3):
    f32 = jnp.float32
    t = jnp.linspace(0.0, 1.0, L, dtype=f32)[:, None]
    w = 2.0 * math.pi * jnp.arange(L, dtype=f32) / L
    fb = jnp.linspace(1e-4, HY_BANDS - 1, HY_BANDS, dtype=f32)
    ph = w[:, None] * fb
    feats = jnp.concatenate([t, jnp.cos(ph), -jnp.sin(ph)], axis=-1)
    h = jnp.sin(f1.astype(f32) * (feats @ w1.astype(f32) + b1.astype(f32)))
    h = jnp.sin(f2.astype(f32) * (h @ w2.astype(f32) + b2.astype(f32)))
    h = h @ w3.astype(f32)
    deltas = jnp.linspace(abs(math.log(HY_TARGET) / HY_SLOW), abs(math.log(HY_TARGET) / HY_FAST), HY_W, dtype=f32)
    decay = jnp.exp(-t * deltas)
    hf = h[:, :HY_W] * decay
    hb = h[:, HY_W:] * decay
    kf = jnp.concatenate([hf, jnp.zeros((1, HY_W), f32), jnp.flip(hb[1:], axis=0)], axis=0)
    return kf * lax.rsqrt(jnp.sum(kf * kf, axis=0, keepdims=True) + EPS)


def hyena_mixer(z, conv_w, conv_b, w1, b1, f1, w2, b2, f2, w3, skip):
    B, L, _ = z.shape
    zc = short_conv3(z, conv_w, conv_b)
    v, x1, x2 = jnp.split(zc, 3, axis=-1)
    kf = hyena_filter(L, w1, b1, f1, w2, b2, f2, w3)
    u = (v * x1).astype(jnp.float32)
    U = jnp.fft.rfft(u, n=2 * L, axis=1)
    K = jnp.fft.rfft(kf, n=2 * L, axis=0)
    y = jnp.fft.irfft(U * K[None], n=2 * L, axis=1)[:, :L] + u * skip.astype(jnp.float32)
    return (x2.astype(jnp.float32) * y).astype(z.dtype)


def gla_chunked(q, k, v, log_a, include_diag):
    B, L, H, DK = q.shape
    DV = v.shape[-1]
    C = GLA_CHUNK
    nc = L // C
    q = q.reshape(B, nc, C, H, DK)
    k = k.reshape(B, nc, C, H, DK)
    v = v.reshape(B, nc, C, H, DV)
    b = jnp.cumsum(log_a.reshape(B, nc, C, H, DK), axis=2)
    b_mid = b[:, :, C // 2 - 1:C // 2]
    A = jnp.einsum('bnihd,bnjhd->bnhij', q * jnp.exp(b - b_mid), k * jnp.exp(b_mid - b))
    mask = jnp.tril(jnp.ones((C, C), dtype=bool), k=0 if include_diag else -1)
    A = jnp.where(mask, A, 0.0)
    o_intra = jnp.einsum('bnhij,bnjhv->bnihv', A, v)
    b_last = b[:, :, C - 1:]
    U = jnp.einsum('bnjhd,bnjhv->nbhdv', k * jnp.exp(b_last - b), v)
    decay = jnp.moveaxis(jnp.exp(b_last[:, :, 0]), 1, 0)

    def step(S, inp):
        d, u = inp
        return d[..., None] * S + u, S

    _, S_prev = lax.scan(step, jnp.zeros((B, H, DK, DV), jnp.float32), (decay, U))
    o_inter = jnp.einsum('bnihd,nbhdv->bnihv', q * jnp.exp(b), S_prev)
    return (o_intra + o_inter).reshape(B, L, H, DV)


def gla_mixer(q, k, v, og, glow, gate_up, gate_b, norm_g):
    B, L, _ = q.shape
    f32 = jnp.float32
    q = q.astype(f32).reshape(B, L, GLA_HEADS, GLA_DK) * (GLA_DK ** -0.5)
    k = k.astype(f32).reshape(B, L, GLA_HEADS, GLA_DK)
    v = v.astype(f32).reshape(B, L, GLA_HEADS, GLA_DV)
    lo = glow.astype(f32).reshape(B, L, 2, GLA_RANK)
    logit = jnp.einsum('blnr,nrk->blnk', lo, gate_up.astype(f32)) + gate_b.astype(f32)
    log_a = jax.nn.log_sigmoid(logit) / GLA_TAU
    la_f = log_a[:, :, 0].reshape(B, L, GLA_HEADS, GLA_DK)
    la_b = log_a[:, :, 1].reshape(B, L, GLA_HEADS, GLA_DK)
    o_f = gla_chunked(q, k, v, la_f, True)
    o_b = jnp.flip(gla_chunked(jnp.flip(q, 1), jnp.flip(k, 1), jnp.flip(v, 1), jnp.flip(la_b, 1), False), axis=1)
    o = rms_norm(o_f + o_b, norm_g)
    o = o * jax.nn.silu(og.astype(f32)).reshape(B, L, GLA_HEADS, GLA_DV)
    return o.reshape(B, L, GLA_V).astype(og.dtype)


def encoder_layer(x, norm_mix_g, w_in, q_norm_g, k_norm_g, hy_conv_w, hy_conv_b, hy_w1, hy_b1, hy_f1, hy_w2, hy_b2, hy_f2, hy_w3, hy_skip, gla_gate_up, gla_gate_b, gla_norm_g, w_branch, w_out, norm_ffn_g, w_ffn_gate, w_ffn_up, w_ffn_down):
    B, L, _ = x.shape
    h = rms_norm(x, norm_mix_g)
    z = h @ w_in
    aq, ak, av, hz, gq, gk, gv, gog, glow, gates = split_cols(z)
    cos, sin = axial_rope(L)
    y_att = gqa_attention(aq, ak, av, q_norm_g, k_norm_g, cos, sin)
    y_hy = hyena_mixer(hz, hy_conv_w, hy_conv_b, hy_w1, hy_b1, hy_f1, hy_w2, hy_b2, hy_f2, hy_w3, hy_skip)
    y_gla = gla_mixer(gq, gk, gv, gog, glow, gla_gate_up, gla_gate_b, gla_norm_g)
    ys = jnp.stack([y_att, y_hy, y_gla], axis=2)
    proj = jnp.einsum('blnc,ncd->blnd', ys, w_branch)
    g = jax.nn.sigmoid(gates.reshape(B, L, N_BRANCH, D_MODEL))
    merged = jnp.sum(g * proj, axis=2)
    x = x + merged @ w_out
    h = rms_norm(x, norm_ffn_g)
    x = x + (jax.nn.silu(h @ w_ffn_gate) * (h @ w_ffn_up)) @ w_ffn_down
    return x


def setup_inputs(seed: int = 0) -> dict:
    key = jax.random.key(seed)
    ks = jax.random.split(key, 28)
    f32 = jnp.float32

    def nrm(k, shape, scale):
        return jax.random.normal(k, shape, f32) * scale

    return {
        'x_prompt': nrm(ks[0], (BATCH, SEQ, D_MODEL), 1.0),
        'x_sample': nrm(ks[1], (DEC_BATCH, DEC_SEQ, D_MODEL), 1.0),
        'norm_mix_g': 1.0 + nrm(ks[2], (DEPTH, D_MODEL), 0.01),
        'w_in': nrm(ks[3], (DEPTH, D_MODEL, IN_COLS), D_MODEL ** -0.5),
        'q_norm_g': 1.0 + nrm(ks[4], (DEPTH, HEAD_DIM), 0.01),
        'k_norm_g': 1.0 + nrm(ks[5], (DEPTH, HEAD_DIM), 0.01),
        'hy_conv_w': nrm(ks[6], (DEPTH, 3, 3 * HY_W), 3 ** -0.5),
        'hy_conv_b': nrm(ks[7], (DEPTH, 3 * HY_W), 0.01),
        'hy_w1': nrm(ks[8], (DEPTH, HY_EMB, HY_FILTER_W), HY_EMB ** -0.5),
        'hy_b1': nrm(ks[9], (DEPTH, HY_FILTER_W), 0.1),
        'hy_f1': 1.0 + nrm(ks[10], (DEPTH, HY_FILTER_W), 0.01),
        'hy_w2': nrm(ks[11], (DEPTH, HY_FILTER_W, HY_FILTER_W), HY_FILTER_W ** -0.5),
        'hy_b2': nrm(ks[12], (DEPTH, HY_FILTER_W), 0.1),
        'hy_f2': 1.0 + nrm(ks[13], (DEPTH, HY_FILTER_W), 0.01),
        'hy_w3': nrm(ks[14], (DEPTH, HY_FILTER_W, 2 * HY_W), HY_FILTER_W ** -0.5),
        'hy_skip': nrm(ks[15], (DEPTH, HY_W), 0.5),
        'gla_gate_up': nrm(ks[16], (DEPTH, 2, GLA_RANK, GLA_K), GLA_RANK ** -0.5),
        'gla_gate_b': nrm(ks[17], (DEPTH, 2, GLA_K), 0.1),
        'gla_norm_g': 1.0 + nrm(ks[18], (DEPTH, GLA_DV), 0.01),
        'w_branch': nrm(ks[19], (DEPTH, N_BRANCH, MIX_W, D_MODEL), MIX_W ** -0.5),
        'w_out': nrm(ks[20], (DEPTH, D_MODEL, D_MODEL), D_MODEL ** -0.5),
        'norm_ffn_g': 1.0 + nrm(ks[21], (DEPTH, D_MODEL), 0.01),
        'w_ffn_gate': nrm(ks[22], (DEPTH, D_MODEL, D_FF), D_MODEL ** -0.5),
        'w_ffn_up': nrm(ks[23], (DEPTH, D_MODEL, D_FF), D_MODEL ** -0.5),
        'w_ffn_down': nrm(ks[24], (DEPTH, D_FF, D_MODEL), D_FF ** -0.5),
        'final_norm_g': 1.0 + nrm(ks[25], (D_MODEL,), 0.01),
    }


def reference(x_prompt, x_sample, norm_mix_g, w_in, q_norm_g, k_norm_g, hy_conv_w, hy_conv_b, hy_w1, hy_b1, hy_f1, hy_w2, hy_b2, hy_f2, hy_w3, hy_skip, gla_gate_up, gla_gate_b, gla_norm_g, w_branch, w_out, norm_ffn_g, w_ffn_gate, w_ffn_up, w_ffn_down, final_norm_g):
    xp = x_prompt
    xs = x_sample
    for l in range(DEPTH):
        p = (norm_mix_g[l], w_in[l], q_norm_g[l], k_norm_g[l], hy_conv_w[l], hy_conv_b[l], hy_w1[l], hy_b1[l], hy_f1[l], hy_w2[l], hy_b2[l], hy_f2[l], hy_w3[l], hy_skip[l], gla_gate_up[l], gla_gate_b[l], gla_norm_g[l], w_branch[l], w_out[l], norm_ffn_g[l], w_ffn_gate[l], w_ffn_up[l], w_ffn_down[l])
        xp = encoder_layer(xp, *p)
        xs = encoder_layer(xs, *p)
    y_prompt = rms_norm(xp, final_norm_g)
    y_sample = rms_norm(xs, final_norm_g)
    return (y_prompt, y_sample)
```

```python
import functools
import math

import jax
import jax.numpy as jnp
from jax import lax
from jax.experimental import pallas as pl
from jax.experimental.pallas import tpu as pltpu

F32 = jnp.float32
BF16 = jnp.bfloat16

D_MODEL = 1024
DEPTH = 4
GRID_W = 64
N_HEADS = 8
N_KV_HEADS = 2
HEAD_DIM = 64
GROUP = N_HEADS // N_KV_HEADS
ROPE_THETA = 10000.0
ROPE_AXIS_DIM = HEAD_DIM // 2
ATT_W = N_HEADS * HEAD_DIM
KV_W = N_KV_HEADS * HEAD_DIM
HY_W = D_MODEL // 2
HY_EMB = 33
HY_BANDS = (HY_EMB - 1) // 2
HY_FILTER_W = 64
HY_TARGET = 1e-2
HY_FAST = 0.3
HY_SLOW = 1.5
GLA_HEADS = 4
GLA_DK = 64
GLA_DV = 128
GLA_K = GLA_HEADS * GLA_DK
GLA_V = GLA_HEADS * GLA_DV
GLA_RANK = 16
GLA_TAU = 16.0
GLA_CHUNK = 64
N_BRANCH = 3
MIX_W = 512
D_FF = 2816
EPS = 1e-6

LANES = 128
VMEM_LIMIT = 56 * 1024 * 1024

Z_GATES = 0
Z_AQ = 3072
Z_GV = 3584
Z_GOG = 4096
Z_HZ = 4608
Z_GQ = 6144
Z_GK = 6400
Z_AK = 6656
Z_AV = 6784
Z_GLOW = 6912
Z_COLS = 7168

FFT_N2 = 128
FFT_KB = 8
FFT_W = 4096
FFT_CT = 256


def _cparams(*sem):
    return pltpu.CompilerParams(dimension_semantics=sem, vmem_limit_bytes=VMEM_LIMIT)


def _split2(x):
    hi = x.astype(BF16)
    lo = (x - hi.astype(F32)).astype(BF16)
    return hi, lo


def _dot(a, b):
    return jnp.dot(a, b, preferred_element_type=F32)


def _dot3(a, b):
    ah, al = _split2(a)
    bh, bl = _split2(b)
    return _dot(ah, bh) + _dot(al, bh) + _dot(ah, bl)


def _in_proj_kernel(x_ref, g_ref, w_ref, o_ref, h_ref):
    @pl.when(pl.program_id(1) == 0)
    def _():
        x = x_ref[...]
        ms = jnp.mean(x * x, axis=-1, keepdims=True)
        h_ref[...] = (x * lax.rsqrt(ms + EPS) * g_ref[...]).astype(BF16)

    o_ref[...] = _dot(h_ref[...], w_ref[...])


def _in_proj(x, g, w, tm=512, tn=1024):
    n = x.shape[0]
    return pl.pallas_call(
        _in_proj_kernel,
        grid=(n // tm, Z_COLS // tn),
        in_specs=[
            pl.BlockSpec((tm, D_MODEL), lambda i, j: (i, 0)),
            pl.BlockSpec((1, D_MODEL), lambda i, j: (0, 0)),
            pl.BlockSpec((D_MODEL, tn), lambda i, j: (0, j)),
        ],
        out_specs=pl.BlockSpec((tm, tn), lambda i, j: (i, j)),
        out_shape=jax.ShapeDtypeStruct((n, Z_COLS), F32),
        scratch_shapes=[pltpu.VMEM((tm, D_MODEL), BF16)],
        compiler_params=_cparams("parallel", "arbitrary"),
        name="in_proj",
    )(x, g, w)


def _attn_prep_kernel(q_ref, k_ref, v_ref, cos_ref, sin_ref, gq_ref, gk_ref, qe_ref, kr_ref, vb_ref):
    tm = q_ref.shape[0]
    cos = cos_ref[...]
    sin = sin_ref[...]
    lane = lax.broadcasted_iota(jnp.int32, (tm, LANES), 1)
    first_half = (lane % HEAD_DIM) < (HEAD_DIM // 2)
    left = lane < HEAD_DIM
    r = lax.broadcasted_iota(jnp.int32, (LANES, LANES), 0) // HEAD_DIM
    c = lax.broadcasted_iota(jnp.int32, (LANES, LANES), 1) // HEAD_DIM
    head_ones = jnp.where(r == c, 1.0, 0.0).astype(BF16)

    def norm_rope(x, g):
        xsq = x * x
        hi, lo = _split2(xsq)
        lo2 = (xsq - hi.astype(F32) - lo.astype(F32)).astype(BF16)
        ssq = _dot(hi, head_ones) + _dot(lo, head_ones) + _dot(lo2, head_ones)
        y = x * lax.rsqrt(ssq * (1.0 / HEAD_DIM) + EPS) * g
        other = jnp.where(first_half, pltpu.roll(y, LANES - HEAD_DIM // 2, 1), pltpu.roll(y, HEAD_DIM // 2, 1))
        return y * cos + other * sin

    gq = gq_ref[...]
    for jb in range(ATT_W // LANES):
        y = norm_rope(q_ref[:, jb * LANES:(jb + 1) * LANES], gq) * (HEAD_DIM ** -0.5)
        ys = pltpu.roll(y, HEAD_DIM, 1)
        kv = (2 * jb) // GROUP
        if kv == 0:
            e0 = jnp.where(left, y, 0.0)
            e1 = jnp.where(left, ys, 0.0)
        else:
            e0 = jnp.where(left, 0.0, ys)
            e1 = jnp.where(left, 0.0, y)
        qe_ref[:, (2 * jb) * LANES:(2 * jb + 1) * LANES] = e0.astype(BF16)
        qe_ref[:, (2 * jb + 1) * LANES:(2 * jb + 2) * LANES] = e1.astype(BF16)
    kr_ref[...] = norm_rope(k_ref[...], gk_ref[...]).astype(BF16)
    vb_ref[...] = v_ref[...].astype(BF16)


def _attn_prep(z, cos_t, sin_t, gq, gk, L, tm=512):
    n = z.shape[0]
    lt = L // tm
    return pl.pallas_call(
        _attn_prep_kernel,
        grid=(n // tm,),
        in_specs=[
            pl.BlockSpec((tm, ATT_W), lambda i: (i, Z_AQ // ATT_W)),
            pl.BlockSpec((tm, KV_W), lambda i: (i, Z_AK // KV_W)),
            pl.BlockSpec((tm, KV_W), lambda i: (i, Z_AV // KV_W)),
            pl.BlockSpec((tm, LANES), lambda i: (i % lt, 0)),
            pl.BlockSpec((tm, LANES), lambda i: (i % lt, 0)),
            pl.BlockSpec((1, LANES), lambda i: (0, 0)),
            pl.BlockSpec((1, LANES), lambda i: (0, 0)),
        ],
        out_specs=[
            pl.BlockSpec((tm, N_HEADS * LANES), lambda i: (i, 0)),
            pl.BlockSpec((tm, KV_W), lambda i: (i, 0)),
            pl.BlockSpec((tm, KV_W), lambda i: (i, 0)),
        ],
        out_shape=[
            jax.ShapeDtypeStruct((n, N_HEADS * LANES), BF16),
            jax.ShapeDtypeStruct((n, KV_W), BF16),
            jax.ShapeDtypeStruct((n, KV_W), BF16),
        ],
        compiler_params=_cparams("parallel"),
        name="attn_prep",
    )(z, z, z, cos_t, sin_t, gq, gk)


def _flash_kernel(q_ref, k_ref, v_ref, o_ref, qs_ref, *, tk):
    tq = q_ref.shape[0]
    L = k_ref.shape[0]
    rows = N_HEADS * tq
    for h in range(N_HEADS):
        qs_ref[h * tq:(h + 1) * tq, :] = q_ref[:, h * LANES:(h + 1) * LANES]
    q = qs_ref[...]

    def body(j, carry):
        m, l, acc = carry
        start = pl.multiple_of(j * tk, tk)
        kj = k_ref[pl.ds(start, tk), :]
        vj = v_ref[pl.ds(start, tk), :]
        s = lax.dot_general(q, kj, (((1,), (1,)), ((), ())), preferred_element_type=F32)
        m_new = jnp.maximum(m, jnp.max(s, axis=-1, keepdims=True))
        alpha = jnp.exp(m - m_new)
        p = jnp.exp(s - m_new)
        l_new = alpha * l + jnp.sum(p, axis=-1, keepdims=True)
        acc_new = alpha * acc + _dot(p.astype(BF16), vj)
        return m_new, l_new, acc_new

    m0 = jnp.full((rows, 1), -jnp.inf, F32)
    l0 = jnp.zeros((rows, 1), F32)
    a0 = jnp.zeros((rows, LANES), F32)
    _, l, acc = lax.fori_loop(0, L // tk, body, (m0, l0, a0))
    o = acc / l
    lane = lax.broadcasted_iota(jnp.int32, (tq, LANES), 1)
    left = lane < HEAD_DIM
    for jb in range(ATT_W // LANES):
        a = o[(2 * jb) * tq:(2 * jb + 1) * tq]
        b = o[(2 * jb + 1) * tq:(2 * jb + 2) * tq]
        if (2 * jb) // GROUP == 0:
            blk = jnp.where(left, a, pltpu.roll(b, HEAD_DIM, 1))
        else:
            blk = jnp.where(left, pltpu.roll(a, HEAD_DIM, 1), b)
        o_ref[:, jb * LANES:(jb + 1) * LANES] = blk


def _flash(qe, kr, vb, B, L, tq=256, tk=512):
    n = qe.shape[0]
    lt = L // tq
    return pl.pallas_call(
        functools.partial(_flash_kernel, tk=tk),
        grid=(B, lt),
        in_specs=[
            pl.BlockSpec((tq, N_HEADS * LANES), lambda b, i: (b * lt + i, 0)),
            pl.BlockSpec((L, KV_W), lambda b, i: (b, 0)),
            pl.BlockSpec((L, KV_W), lambda b, i: (b, 0)),
        ],
        out_specs=pl.BlockSpec((tq, ATT_W), lambda b, i: (b * lt + i, 0)),
        out_shape=jax.ShapeDtypeStruct((n, ATT_W), F32),
        scratch_shapes=[pltpu.VMEM((N_HEADS * tq, LANES), BF16)],
        compiler_params=_cparams("parallel", "parallel"),
        name="flash",
    )(qe, kr, vb)


def _hy_pre_kernel(v_ref, vp_ref, vn_ref, a_ref, ap_ref, an_ref, b_ref, bp_ref, bn_ref, w_ref, cb_ref,
                   u_ref, x2_ref, *, lt):
    tm = v_ref.shape[0]
    i = pl.program_id(0)
    first = (i % lt) == 0
    last = (i % lt) == lt - 1
    rows = lax.broadcasted_iota(jnp.int32, v_ref.shape, 0)

    def conv3(m_ref, p_ref, n_ref, part):
        x = m_ref[...]
        prev = jnp.where(first, 0.0, p_ref[7:8, :])
        nxt = jnp.where(last, 0.0, n_ref[0:1, :])
        dn = jnp.where(rows == 0, prev, pltpu.roll(x, 1, 0))
        up = jnp.where(rows == tm - 1, nxt, pltpu.roll(x, tm - 1, 0))
        w = w_ref[part]
        return dn * w[0:1] + x * w[1:2] + up * w[2:3] + cb_ref[part:part + 1, :]

    u_ref[...] = conv3(v_ref, vp_ref, vn_ref, 0) * conv3(a_ref, ap_ref, an_ref, 1)
    x2_ref[...] = conv3(b_ref, bp_ref, bn_ref, 2)


def _hy_pre(z, conv_w, conv_b, L, tm=512, tc=256):
    n = z.shape[0]
    lt = L // tm
    r8 = tm // 8
    nb8 = n // 8
    specs = []
    for part in range(3):
        c0 = (Z_HZ + part * HY_W) // tc
        specs.append(pl.BlockSpec((tm, tc), lambda i, c, c0=c0: (i, c0 + c)))
        specs.append(pl.BlockSpec((8, tc), lambda i, c, c0=c0: (jnp.maximum(i * r8 - 1, 0), c0 + c)))
        specs.append(pl.BlockSpec((8, tc), lambda i, c, c0=c0: (jnp.minimum((i + 1) * r8, nb8 - 1), c0 + c)))
    specs.append(pl.BlockSpec((3, 3, tc), lambda i, c: (0, 0, c)))
    specs.append(pl.BlockSpec((3, tc), lambda i, c: (0, c)))
    return pl.pallas_call(
        functools.partial(_hy_pre_kernel, lt=lt),
        grid=(n // tm, HY_W // tc),
        in_specs=specs,
        out_specs=[pl.BlockSpec((tm, tc), lambda i, c: (i, c))] * 2,
        out_shape=[jax.ShapeDtypeStruct((n, HY_W), F32)] * 2,
        compiler_params=_cparams("parallel", "parallel"),
        name="hy_pre",
    )(*([z] * 9), conv_w, conv_b)


def _hy_filt_kernel(f_ref, w1_ref, b1_ref, f1_ref, w2_ref, b2_ref, f2_ref, w3_ref, dl_ref, kf_ref, ssq_ref):
    feats = f_ref[...]
    h = jnp.sin(f1_ref[...] * (_dot3(feats, w1_ref[...]) + b1_ref[...]))
    h = jnp.sin(f2_ref[...] * (_dot3(h, w2_ref[...]) + b2_ref[...]))
    h = _dot3(h, w3_ref[...])
    t = feats[:, 0:1]
    valid = feats[:, HY_EMB:HY_EMB + 1]
    kf = h * jnp.exp(-t * dl_ref[...]) * valid
    kf_ref[...] = kf

    @pl.when(pl.program_id(0) == 0)
    def _():
        ssq_ref[...] = jnp.zeros_like(ssq_ref)

    ssq_ref[...] += jnp.sum(kf * kf, axis=0, keepdims=True)


def _hy_filt(feats_ext, w1p, b1, f1, w2, b2, f2, w3s, deltas, L, tr=512):
    lt = L // tr
    full = lambda a: pl.BlockSpec(a.shape, lambda i: (0,) * a.ndim)
    return pl.pallas_call(
        _hy_filt_kernel,
        grid=(2 * lt,),
        in_specs=[
            pl.BlockSpec((tr, LANES), lambda i: (i, 0)),
            full(w1p), full(b1), full(f1), full(w2), full(b2), full(f2),
            pl.BlockSpec((None, HY_FILTER_W, HY_W), lambda i: (i // lt, 0, 0)),
            full(deltas),
        ],
        out_specs=[pl.BlockSpec((tr, HY_W), lambda i: (i, 0)), pl.BlockSpec((1, HY_W), lambda i: (0, 0))],
        out_shape=[jax.ShapeDtypeStruct((2 * L, HY_W), F32), jax.ShapeDtypeStruct((1, HY_W), F32)],
        compiler_params=_cparams("arbitrary"),
        name="hy_filt",
    )(feats_ext, w1p, b1, f1, w2, b2, f2, w3s, deltas)


def _fft1_kernel(x_ref, f_ref, o_ref):
    xh, xl = _split2(x_ref[...])
    o_ref[...] = _dot(f_ref[...], jnp.concatenate([xh, xl, xh], axis=0))


def _fft1(x2d, fcat):
    B, hin, wtot = x2d.shape
    rows = fcat.shape[0]
    return pl.pallas_call(
        _fft1_kernel,
        grid=(B, wtot // FFT_W),
        in_specs=[
            pl.BlockSpec((None, hin, FFT_W), lambda b, j: (b, 0, j)),
            pl.BlockSpec(fcat.shape, lambda b, j: (0, 0)),
        ],
        out_specs=pl.BlockSpec((None, rows, FFT_W), lambda b, j: (b, 0, j)),
        out_shape=jax.ShapeDtypeStruct((B, rows, wtot), F32),
        compiler_params=_cparams("parallel", "parallel"),
        name="fft1",
    )(x2d, fcat)


def _fft2_kernel(ar_ref, ai_ref, tr_ref, ti_ref, m_ref, o_ref):
    m = m_ref[...]
    for r in range(FFT_KB):
        ar, ai, tr, ti = ar_ref[r], ai_ref[r], tr_ref[r], ti_ref[r]
        b = jnp.concatenate([ar * tr - ai * ti, ar * ti + ai * tr], axis=0)
        bh, bl = _split2(b)
        o_ref[r] = _dot(m, jnp.concatenate([bh, bl, bh], axis=0))


def _fft2(a, tw_r, tw_i, mcat, kp):
    B = a.shape[0]
    C = a.shape[-1]
    nk = kp // FFT_KB
    return pl.pallas_call(
        _fft2_kernel,
        grid=(B, C // FFT_CT, nk),
        in_specs=[
            pl.BlockSpec((None, FFT_KB, FFT_N2, FFT_CT), lambda b, c, k: (b, k, 0, c)),
            pl.BlockSpec((None, FFT_KB, FFT_N2, FFT_CT), lambda b, c, k: (b, nk + k, 0, c)),
            pl.BlockSpec((FFT_KB, FFT_N2, FFT_CT), lambda b, c, k: (k, 0, 0)),
            pl.BlockSpec((FFT_KB, FFT_N2, FFT_CT), lambda b, c, k: (k, 0, 0)),
            pl.BlockSpec(mcat.shape, lambda b, c, k: (0, 0)),
        ],
        out_specs=pl.BlockSpec((None, FFT_KB, 2 * FFT_N2, FFT_CT), lambda b, c, k: (b, k, 0, c)),
        out_shape=jax.ShapeDtypeStruct((B, kp, 2 * FFT_N2, C), F32),
        compiler_params=_cparams("parallel", "parallel", "parallel"),
        name="fft2",
    )(a, a, tw_r, tw_i, mcat)


def _ifft2_kernel(x_ref, k_ref, ssq_ref, tr_ref, ti_ref, m_ref, o_ref):
    m = m_ref[...]
    scale = lax.rsqrt(ssq_ref[...] + EPS)
    n2 = FFT_N2
    for r in range(FFT_KB):
        x = x_ref[r]
        kf = k_ref[r]
        xr, xi = x[:n2], x[n2:]
        kr, ki = kf[:n2] * scale, kf[n2:] * scale
        p = jnp.concatenate([xr * kr - xi * ki, xr * ki + xi * kr], axis=0)
        ph, plo = _split2(p)
        c = _dot(m, jnp.concatenate([ph, plo, ph], axis=0))
        cr, ci = c[:n2], c[n2:]
        tr, ti = tr_ref[r], ti_ref[r]
        o_ref[0, r] = cr * tr + ci * ti
        o_ref[1, r] = ci * tr - cr * ti


def _ifft2(xs, ks, ssq, tw_r, tw_i, mcat_inv):
    B, kp, _, C = xs.shape
    nk = kp // FFT_KB
    return pl.pallas_call(
        _ifft2_kernel,
        grid=(B, C // FFT_CT, nk),
        in_specs=[
            pl.BlockSpec((None, FFT_KB, 2 * FFT_N2, FFT_CT), lambda b, c, k: (b, k, 0, c)),
            pl.BlockSpec((None, FFT_KB, 2 * FFT_N2, FFT_CT), lambda b, c, k: (0, k, 0, c)),
            pl.BlockSpec((1, FFT_CT), lambda b, c, k: (0, c)),
            pl.BlockSpec((FFT_KB, FFT_N2, FFT_CT), lambda b, c, k: (k, 0, 0)),
            pl.BlockSpec((FFT_KB, FFT_N2, FFT_CT), lambda b, c, k: (k, 0, 0)),
            pl.BlockSpec(mcat_inv.shape, lambda b, c, k: (0, 0)),
        ],
        out_specs=pl.BlockSpec((None, 2, FFT_KB, FFT_N2, FFT_CT), lambda b, c, k: (b, 0, k, 0, c)),
        out_shape=jax.ShapeDtypeStruct((B, 2, kp, FFT_N2, C), F32),
        compiler_params=_cparams("parallel", "parallel", "parallel"),
        name="ifft2",
    )(xs, ks, ssq, tw_r, tw_i, mcat_inv)


def _ifft1_kernel(d_ref, w_ref, u_ref, x2_ref, skip_ref, o_ref):
    dh, dl = _split2(d_ref[...])
    y = _dot(w_ref[...], jnp.concatenate([dh, dl, dh], axis=0))
    o_ref[...] = x2_ref[...] * (y + u_ref[...] * skip_ref[...])


def _ifft1(d2d, wcat, u2d, x22d, skip_t):
    B, rows, wtot = d2d.shape
    h = wcat.shape[0]
    return pl.pallas_call(
        _ifft1_kernel,
        grid=(B, wtot // FFT_W),
        in_specs=[
            pl.BlockSpec((None, rows, FFT_W), lambda b, j: (b, 0, j)),
            pl.BlockSpec(wcat.shape, lambda b, j: (0, 0)),
            pl.BlockSpec((None, h, FFT_W), lambda b, j: (b, 0, j)),
            pl.BlockSpec((None, h, FFT_W), lambda b, j: (b, 0, j)),
            pl.BlockSpec((1, FFT_W), lambda b, j: (0, 0)),
        ],
        out_specs=pl.BlockSpec((None, h, FFT_W), lambda b, j: (b, 0, j)),
        out_shape=jax.ShapeDtypeStruct((B, h, wtot), F32),
        compiler_params=_cparams("parallel", "parallel"),
        name="ifft1",
    )(d2d, wcat, u2d, x22d, skip_t)


def _hilo_cat(m):
    hi = m.astype(BF16)
    lo = (m - hi.astype(F32)).astype(BF16)
    return jnp.concatenate([hi, hi, lo], axis=1)


def _fft_tables(L):
    n = 2 * L
    n2 = FFT_N2
    n1 = n // n2
    h = n1 // 2
    kp = h + FFT_KB
    k1 = jnp.arange(kp, dtype=jnp.int32)

    def outer(hin):
        nn = jnp.arange(hin, dtype=jnp.int32)
        ang = (2.0 * math.pi / n1) * ((k1[:, None] * nn[None, :]) % n1).astype(F32)
        return _hilo_cat(jnp.concatenate([jnp.cos(ang), -jnp.sin(ang)], axis=0))

    f_data = outer(h)
    f_filt = outer(n1)
    j = jnp.arange(n2, dtype=jnp.int32)
    ang2 = (2.0 * math.pi / n2) * ((j[:, None] * j[None, :]) % n2).astype(F32)
    c2, s2 = jnp.cos(ang2), jnp.sin(ang2)
    m_fwd = _hilo_cat(jnp.block([[c2, s2], [-s2, c2]]))
    m_inv = _hilo_cat(jnp.block([[c2, -s2], [s2, c2]]))
    angt = (2.0 * math.pi / n) * (k1[:, None] * j[None, :]).astype(F32)
    tw_r = jnp.broadcast_to(jnp.cos(angt)[:, :, None], (kp, n2, FFT_CT))
    tw_i = jnp.broadcast_to(-jnp.sin(angt)[:, :, None], (kp, n2, FFT_CT))
    wgt = jnp.where((k1 == 0) | (k1 == h), 1.0, jnp.where(k1 < h, 2.0, 0.0)).astype(F32) / n
    nn = jnp.arange(h, dtype=jnp.int32)
    angi = (2.0 * math.pi / n1) * ((nn[:, None] * k1[None, :]) % n1).astype(F32)
    w_inv = _hilo_cat(jnp.concatenate([jnp.cos(angi) * wgt, -jnp.sin(angi) * wgt], axis=1))
    return dict(n1=n1, h=h, kp=kp, f_data=f_data, f_filt=f_filt, m_fwd=m_fwd, m_inv=m_inv,
                tw_r=tw_r, tw_i=tw_i, w_inv=w_inv)


def _filter_feats(L):
    t = jnp.linspace(0.0, 1.0, L, dtype=F32)[:, None]
    w = 2.0 * math.pi * jnp.arange(L, dtype=F32) / L
    fb = jnp.linspace(1e-4, HY_BANDS - 1, HY_BANDS, dtype=F32)
    ph = w[:, None] * fb
    feats = jnp.concatenate([t, jnp.cos(ph), -jnp.sin(ph), jnp.ones((L, 1), F32)], axis=-1)
    back = jnp.concatenate([jnp.zeros((1, HY_EMB + 1), F32), jnp.flip(feats[1:], axis=0)], axis=0)
    ext = jnp.concatenate([feats, back], axis=0)
    return jnp.pad(ext, ((0, 0), (0, LANES - HY_EMB - 1)))


def _hyena_spectrum(tabs, feats_ext, lp, L):
    kf, ssq = _hy_filt(feats_ext, lp["hy_w1p"], lp["hy_b1"], lp["hy_f1"], lp["hy_w2"], lp["hy_b2"], lp["hy_f2"],
                       lp["hy_w3s"], lp["hy_deltas"], L)
    a = _fft1(kf.reshape(1, tabs["n1"], FFT_N2 * HY_W), tabs["f_filt"])
    ks = _fft2(a.reshape(1, 2 * tabs["kp"], FFT_N2, HY_W), tabs["tw_r"], tabs["tw_i"], tabs["m_fwd"], tabs["kp"])
    return ks, ssq


def _hyena(z, tabs, ks, ssq, lp, B, L):
    u, x2c = _hy_pre(z, lp["hy_conv_w"], lp["hy_conv_b"], L)
    h, kp = tabs["h"], tabs["kp"]
    u2d = u.reshape(B, h, FFT_N2 * HY_W)
    a = _fft1(u2d, tabs["f_data"])
    xs = _fft2(a.reshape(B, 2 * kp, FFT_N2, HY_W), tabs["tw_r"], tabs["tw_i"], tabs["m_fwd"], kp)
    d = _ifft2(xs, ks, ssq, tabs["tw_r"], tabs["tw_i"], tabs["m_inv"])
    y = _ifft1(d.reshape(B, 2 * kp, FFT_N2 * HY_W), tabs["w_inv"], u2d, x2c.reshape(B, h, FFT_N2 * HY_W),
               lp["hy_skip_t"])
    return y.reshape(B * L, HY_W)


def _gla_kernel(qf_ref, kf_ref, vf_ref, lf_ref, qb_ref, kb_ref, vb_ref, lb_ref, gup_ref, gb_ref,
                of_ref, ob_ref, sf_ref, sb_ref, *, nchunk):
    C = GLA_CHUNK

    @pl.when(pl.program_id(1) == 0)
    def _():
        sf_ref[...] = jnp.zeros_like(sf_ref)
        sb_ref[...] = jnp.zeros_like(sb_ref)

    ri = lax.broadcasted_iota(jnp.int32, (C, C), 0)
    ci = lax.broadcasted_iota(jnp.int32, (C, C), 1)
    lane = lax.broadcasted_iota(jnp.int32, (C, GLA_K), 1) // GLA_DK
    ri4 = lax.broadcasted_iota(jnp.int32, (GLA_HEADS * C, C), 0) % C
    ci4 = lax.broadcasted_iota(jnp.int32, (GLA_HEADS * C, C), 1)

    def stack_heads(x):
        return jnp.concatenate([jnp.where(lane == h, x, 0.0) for h in range(GLA_HEADS)], axis=0).astype(BF16)

    def direction(q_ref, k_ref, v_ref, l_ref, o_ref, s_ref, d):
        glow = l_ref[...]
        logit = _dot3(glow, gup_ref[:, d * GLA_K:(d + 1) * GLA_K]) + gb_ref[:, d * GLA_K:(d + 1) * GLA_K]
        la_all = (jnp.minimum(logit, 0.0) - jnp.log(1.0 + jnp.exp(-jnp.abs(logit)))) * (1.0 / GLA_TAU)
        if d == 0:
            tri = jnp.where(ri >= ci, 1.0, 0.0).astype(BF16)
            keep = ri4 >= ci4
            mid, last = C // 2 - 1, C - 1
            order = range(nchunk)
        else:
            tri = jnp.where(ci >= ri, 1.0, 0.0).astype(BF16)
            keep = ci4 > ri4
            mid, last = C // 2, 0
            order = range(nchunk - 1, -1, -1)
        for c in order:
            rs = slice(c * C, (c + 1) * C)
            la = la_all[rs]
            hi, lo = _split2(la)
            lo2 = (la - hi.astype(F32) - lo.astype(F32)).astype(BF16)
            b = _dot(tri, hi) + _dot(tri, lo) + _dot(tri, lo2)
            b_mid = b[mid:mid + 1]
            b_last = b[last:last + 1]
            q = q_ref[rs, :] * (GLA_DK ** -0.5)
            k = k_ref[rs, :]
            v = v_ref[rs, :].astype(BF16)
            a = lax.dot_general(stack_heads(q * jnp.exp(b - b_mid)), (k * jnp.exp(b_mid - b)).astype(BF16),
                                (((1,), (1,)), ((), ())), preferred_element_type=F32)
            a = jnp.where(keep, a, 0.0).astype(BF16)
            s_prev = s_ref[...]
            o_inter = _dot(stack_heads(q * jnp.exp(b)), s_prev.astype(BF16))
            k_t = jnp.transpose(k * jnp.exp(b_last - b)).astype(BF16)
            outs, ups = [], []
            for h in range(GLA_HEADS):
                vh = v[:, h * GLA_DV:(h + 1) * GLA_DV]
                outs.append(_dot(a[h * C:(h + 1) * C], vh) + o_inter[h * C:(h + 1) * C])
                ups.append(_dot(k_t[h * GLA_DK:(h + 1) * GLA_DK], vh))
            decay = jnp.transpose(jnp.broadcast_to(jnp.exp(b_last), (GLA_DV, GLA_K)))
            s_ref[...] = decay * s_prev + jnp.concatenate(ups, axis=0)
            o_ref[rs, :] = jnp.concatenate(outs, axis=1)

    direction(qf_ref, kf_ref, vf_ref, lf_ref, of_ref, sf_ref, 0)
    direction(qb_ref, kb_ref, vb_ref, lb_ref, ob_ref, sb_ref, 1)


def _gla(z, gup, gb, B, L, nchunk=4):
    n = z.shape[0]
    R = nchunk * GLA_CHUNK
    T = L // R
    fwd = lambda b, t: b * T + t
    bwd = lambda b, t: b * T + (T - 1 - t)

    def specs(row):
        return [
            pl.BlockSpec((R, GLA_K), lambda b, t: (row(b, t), Z_GQ // GLA_K)),
            pl.BlockSpec((R, GLA_K), lambda b, t: (row(b, t), Z_GK // GLA_K)),
            pl.BlockSpec((R, GLA_V), lambda b, t: (row(b, t), Z_GV // GLA_V)),
            pl.BlockSpec((R, LANES), lambda b, t: (row(b, t), Z_GLOW // LANES)),
        ]

    return pl.pallas_call(
        functools.partial(_gla_kernel, nchunk=nchunk),
        grid=(B, T),
        in_specs=specs(fwd) + specs(bwd) + [
            pl.BlockSpec(gup.shape, lambda b, t: (0, 0)),
            pl.BlockSpec(gb.shape, lambda b, t: (0, 0)),
        ],
        out_specs=[
            pl.BlockSpec((R, GLA_V), lambda b, t: (fwd(b, t), 0)),
            pl.BlockSpec((R, GLA_V), lambda b, t: (bwd(b, t), 0)),
        ],
        out_shape=[jax.ShapeDtypeStruct((n, GLA_V), F32)] * 2,
        scratch_shapes=[pltpu.VMEM((GLA_K, GLA_DV), F32)] * 2,
        compiler_params=_cparams("parallel", "arbitrary"),
        name="gla",
    )(*([z] * 8), gup, gb)


def _mix_out_kernel(x_ref, ya_ref, yh_ref, of_ref, ob_ref, og_ref, g0_ref, g1_ref, g2_ref,
                    wb_ref, wo_ref, gn_ref, o_ref):
    o = of_ref[...] + ob_ref[...]
    og = og_ref[...]
    gn = gn_ref[...]
    parts = []
    for h in range(GLA_HEADS):
        blk = o[:, h * GLA_DV:(h + 1) * GLA_DV]
        ms = jnp.mean(blk * blk, axis=-1, keepdims=True)
        gate = og[:, h * GLA_DV:(h + 1) * GLA_DV]
        parts.append(blk * lax.rsqrt(ms + EPS) * gn * (gate * jax.nn.sigmoid(gate)))
    y_gla = jnp.concatenate(parts, axis=1)
    merged = jax.nn.sigmoid(g0_ref[...]) * _dot(ya_ref[...].astype(BF16), wb_ref[0])
    merged += jax.nn.sigmoid(g1_ref[...]) * _dot(yh_ref[...].astype(BF16), wb_ref[1])
    merged += jax.nn.sigmoid(g2_ref[...]) * _dot(y_gla.astype(BF16), wb_ref[2])
    o_ref[...] = x_ref[...] + _dot(merged.astype(BF16), wo_ref[...])


def _mix_out(x, y_att, y_hy, o_f, o_b, z, wb, wo, gn, tm=256):
    n = x.shape[0]
    row = lambda w: pl.BlockSpec((tm, w), lambda i: (i, 0))
    return pl.pallas_call(
        _mix_out_kernel,
        grid=(n // tm,),
        in_specs=[
            row(D_MODEL), row(MIX_W), row(MIX_W), row(MIX_W), row(MIX_W),
            pl.BlockSpec((tm, GLA_V), lambda i: (i, Z_GOG // GLA_V)),
            pl.BlockSpec((tm, D_MODEL), lambda i: (i, 0)),
            pl.BlockSpec((tm, D_MODEL), lambda i: (i, 1)),
            pl.BlockSpec((tm, D_MODEL), lambda i: (i, 2)),
            pl.BlockSpec(wb.shape, lambda i: (0, 0, 0)),
            pl.BlockSpec(wo.shape, lambda i: (0, 0)),
            pl.BlockSpec((1, GLA_DV), lambda i: (0, 0)),
        ],
        out_specs=row(D_MODEL),
        out_shape=jax.ShapeDtypeStruct((n, D_MODEL), F32),
        compiler_params=_cparams("parallel"),
        name="mix_out",
    )(x, y_att, y_hy, o_f, o_b, z, z, z, z, wb, wo, gn)


def _ffn_kernel(x_ref, g_ref, wg_ref, wu_ref, wd_ref, o_ref, h_ref, acc_ref):
    j = pl.program_id(1)

    @pl.when(j == 0)
    def _():
        x = x_ref[...]
        ms = jnp.mean(x * x, axis=-1, keepdims=True)
        h_ref[...] = (x * lax.rsqrt(ms + EPS) * g_ref[...]).astype(BF16)
        acc_ref[...] = jnp.zeros_like(acc_ref)

    h = h_ref[...]
    a = _dot(h, wg_ref[...])
    act = (a * jax.nn.sigmoid(a)) * _dot(h, wu_ref[...])
    acc_ref[...] += _dot(act.astype(BF16), wd_ref[...])

    @pl.when(j == pl.num_programs(1) - 1)
    def _():
        o_ref[...] = x_ref[...] + acc_ref[...]


def _ffn(x, g, wg, wu, wd, tm=512, tf=1408):
    n = x.shape[0]
    return pl.pallas_call(
        _ffn_kernel,
        grid=(n // tm, D_FF // tf),
        in_specs=[
            pl.BlockSpec((tm, D_MODEL), lambda i, j: (i, 0)),
            pl.BlockSpec((1, D_MODEL), lambda i, j: (0, 0)),
            pl.BlockSpec((D_MODEL, tf), lambda i, j: (0, j)),
            pl.BlockSpec((D_MODEL, tf), lambda i, j: (0, j)),
            pl.BlockSpec((tf, D_MODEL), lambda i, j: (j, 0)),
        ],
        out_specs=pl.BlockSpec((tm, D_MODEL), lambda i, j: (i, 0)),
        out_shape=jax.ShapeDtypeStruct((n, D_MODEL), F32),
        scratch_shapes=[pltpu.VMEM((tm, D_MODEL), BF16), pltpu.VMEM((tm, D_MODEL), F32)],
        compiler_params=_cparams("parallel", "arbitrary"),
        name="ffn",
    )(x, g, wg, wu, wd)


def _final_norm_kernel(x_ref, g_ref, o_ref):
    x = x_ref[...]
    ms = jnp.mean(x * x, axis=-1, keepdims=True)
    o_ref[...] = x * lax.rsqrt(ms + EPS) * g_ref[...]


def _final_norm(x, g, tm=512):
    n = x.shape[0]
    return pl.pallas_call(
        _final_norm_kernel,
        grid=(n // tm,),
        in_specs=[pl.BlockSpec((tm, D_MODEL), lambda i: (i, 0)), pl.BlockSpec((1, D_MODEL), lambda i: (0, 0))],
        out_specs=pl.BlockSpec((tm, D_MODEL), lambda i: (i, 0)),
        out_shape=jax.ShapeDtypeStruct((n, D_MODEL), F32),
        compiler_params=_cparams("parallel"),
        name="final_norm",
    )(x, g)


def _rope_tables(L):
    rows = L // GRID_W
    r = jnp.repeat(jnp.arange(rows, dtype=F32), GRID_W)
    c = jnp.tile(jnp.arange(GRID_W, dtype=F32), rows)
    inv = ROPE_THETA ** (-jnp.arange(0, ROPE_AXIS_DIM, 2, dtype=F32) / ROPE_AXIS_DIM)
    ang = jnp.concatenate([r[:, None] * inv, c[:, None] * inv], axis=-1)
    cos, sin = jnp.cos(ang), jnp.sin(ang)
    cos_t = jnp.tile(cos, (1, 2 * LANES // HEAD_DIM))
    sin_t = jnp.tile(jnp.concatenate([-sin, sin], axis=-1), (1, LANES // HEAD_DIM))
    return cos_t, sin_t


def _layer_params(l, norm_mix_g, w_in, q_norm_g, k_norm_g, hy_conv_w, hy_conv_b, hy_w1, hy_b1, hy_f1, hy_w2,
                  hy_b2, hy_f2, hy_w3, hy_skip, gla_gate_up, gla_gate_b, gla_norm_g, w_branch, w_out, norm_ffn_g,
                  w_ffn_gate, w_ffn_up, w_ffn_down):
    w = w_in[l]
    offs = {}
    off = 0
    for name, size in (("aq", ATT_W), ("ak", KV_W), ("av", KV_W), ("hz", 3 * HY_W), ("gq", GLA_K), ("gk", GLA_K),
                       ("gv", GLA_V), ("gog", GLA_V), ("glow", 2 * GLA_RANK), ("gates", N_BRANCH * D_MODEL)):
        offs[name] = w[:, off:off + size]
        off += size
    w_cat = jnp.concatenate(
        [offs["gates"], offs["aq"], offs["gv"], offs["gog"], offs["hz"], offs["gq"], offs["gk"], offs["ak"],
         offs["av"], offs["glow"], jnp.zeros((D_MODEL, Z_COLS - Z_GLOW - 2 * GLA_RANK), F32)], axis=1).astype(BF16)
    gup = jnp.zeros((LANES, 2 * GLA_K), F32)
    gup = gup.at[0:GLA_RANK, 0:GLA_K].set(gla_gate_up[l, 0])
    gup = gup.at[GLA_RANK:2 * GLA_RANK, GLA_K:2 * GLA_K].set(gla_gate_up[l, 1])
    deltas = jnp.linspace(abs(math.log(HY_TARGET) / HY_SLOW), abs(math.log(HY_TARGET) / HY_FAST), HY_W, dtype=F32)
    return dict(
        norm_mix_g=norm_mix_g[l][None, :],
        w_cat=w_cat,
        gq=jnp.tile(q_norm_g[l], LANES // HEAD_DIM)[None, :],
        gk=jnp.tile(k_norm_g[l], LANES // HEAD_DIM)[None, :],
        hy_conv_w=jnp.transpose(hy_conv_w[l].reshape(3, 3, HY_W), (1, 0, 2)),
        hy_conv_b=hy_conv_b[l].reshape(3, HY_W),
        hy_w1p=jnp.pad(hy_w1[l], ((0, LANES - HY_EMB), (0, 0))),
        hy_b1=hy_b1[l][None, :], hy_f1=hy_f1[l][None, :],
        hy_w2=hy_w2[l], hy_b2=hy_b2[l][None, :], hy_f2=hy_f2[l][None, :],
        hy_w3s=jnp.transpose(hy_w3[l].reshape(HY_FILTER_W, 2, HY_W), (1, 0, 2)),
        hy_deltas=deltas[None, :],
        hy_skip_t=jnp.tile(hy_skip[l], FFT_W // HY_W)[None, :],
        gup=gup, gb=gla_gate_b[l].reshape(1, 2 * GLA_K),
        gn=gla_norm_g[l][None, :],
        wb=w_branch[l].astype(BF16), wo=w_out[l].astype(BF16),
        norm_ffn_g=norm_ffn_g[l][None, :],
        wg=w_ffn_gate[l].astype(BF16), wu=w_ffn_up[l].astype(BF16), wd=w_ffn_down[l].astype(BF16),
    )


def _encoder_layer(x, lp, B, L, rope, tabs, ks, ssq):
    z = _in_proj(x, lp["norm_mix_g"], lp["w_cat"])
    qe, kr, vb = _attn_prep(z, rope[0], rope[1], lp["gq"], lp["gk"], L)
    y_att = _flash(qe, kr, vb, B, L)
    y_hy = _hyena(z, tabs, ks, ssq, lp, B, L)
    o_f, o_b = _gla(z, lp["gup"], lp["gb"], B, L)
    x = _mix_out(x, y_att, y_hy, o_f, o_b, z, lp["wb"], lp["wo"], lp["gn"])
    return _ffn(x, lp["norm_ffn_g"], lp["wg"], lp["wu"], lp["wd"])


def kernel(x_prompt, x_sample, norm_mix_g, w_in, q_norm_g, k_norm_g, hy_conv_w, hy_conv_b, hy_w1, hy_b1, hy_f1, hy_w2, hy_b2, hy_f2, hy_w3, hy_skip, gla_gate_up, gla_gate_b, gla_norm_g, w_branch, w_out, norm_ffn_g, w_ffn_gate, w_ffn_up, w_ffn_down, final_norm_g):
    streams = []
    for xin in (x_prompt, x_sample):
        B, L, _ = xin.shape
        streams.append(dict(x=xin.reshape(B * L, D_MODEL), B=B, L=L, rope=_rope_tables(L), tabs=_fft_tables(L),
                            feats=_filter_feats(L)))
    for l in range(DEPTH):
        lp = _layer_params(l, norm_mix_g, w_in, q_norm_g, k_norm_g, hy_conv_w, hy_conv_b, hy_w1, hy_b1, hy_f1,
                           hy_w2, hy_b2, hy_f2, hy_w3, hy_skip, gla_gate_up, gla_gate_b, gla_norm_g, w_branch,
                           w_out, norm_ffn_g, w_ffn_gate, w_ffn_up, w_ffn_down)
        for s in streams:
            ks, ssq = _hyena_spectrum(s["tabs"], s["feats"], lp, s["L"])
            s["x"] = _encoder_layer(s["x"], lp, s["B"], s["L"], s["rope"], s["tabs"], ks, ssq)
    outs = []
    for s, xin in zip(streams, (x_prompt, x_sample)):
        outs.append(_final_norm(s["x"], final_norm_g[None, :]).reshape(xin.shape))
    return tuple(outs)
```

```python
import functools
import math

import jax
import jax.numpy as jnp
from jax import lax
from jax.experimental import pallas as pl
from jax.experimental.pallas import tpu as pltpu

F32 = jnp.float32
BF16 = jnp.bfloat16

D_MODEL = 1024
DEPTH = 4
GRID_W = 64
N_HEADS = 8
N_KV_HEADS = 2
HEAD_DIM = 64
GROUP = N_HEADS // N_KV_HEADS
ROPE_THETA = 10000.0
ROPE_AXIS_DIM = HEAD_DIM // 2
ATT_W = N_HEADS * HEAD_DIM
KV_W = N_KV_HEADS * HEAD_DIM
HY_W = D_MODEL // 2
HY_EMB = 33
HY_BANDS = (HY_EMB - 1) // 2
HY_FILTER_W = 64
HY_TARGET = 1e-2
HY_FAST = 0.3
HY_SLOW = 1.5
GLA_HEADS = 4
GLA_DK = 64
GLA_DV = 128
GLA_K = GLA_HEADS * GLA_DK
GLA_V = GLA_HEADS * GLA_DV
GLA_RANK = 16
GLA_TAU = 16.0
GLA_CHUNK = 64
N_BRANCH = 3
MIX_W = 512
D_FF = 2816
EPS = 1e-6

LANES = 128
VMEM_LIMIT = 56 * 1024 * 1024

Z_GATES = 0
Z_AQ = 3072
Z_GV = 3584
Z_GOG = 4096
Z_HZ = 4608
Z_GQ = 6144
Z_GK = 6400
Z_AK = 6656
Z_AV = 6784
Z_GLOW = 6912
Z_COLS = 7168

FFT_N2 = 128
FFT_KB = 8
FFT_W = 4096
FFT_CT = 256


def _cparams(*sem):
    return pltpu.CompilerParams(dimension_semantics=sem, vmem_limit_bytes=VMEM_LIMIT)


def _split2(x):
    hi = x.astype(BF16)
    lo = (x - hi.astype(F32)).astype(BF16)
    return hi, lo


def _dot(a, b):
    return jnp.dot(a, b, preferred_element_type=F32)


def _dot3(a, b):
    ah, al = _split2(a)
    bh, bl = _split2(b)
    return _dot(ah, bh) + _dot(al, bh) + _dot(ah, bl)


def _in_proj_kernel(x_ref, g_ref, w_ref, o_ref, h_ref):
    @pl.when(pl.program_id(1) == 0)
    def _():
        x = x_ref[...]
        ms = jnp.mean(x * x, axis=-1, keepdims=True)
        h_ref[...] = (x * lax.rsqrt(ms + EPS) * g_ref[...]).astype(BF16)

    o_ref[...] = _dot(h_ref[...], w_ref[...])


def _in_proj(x, g, w, tm=512, tn=1024):
    n = x.shape[0]
    return pl.pallas_call(
        _in_proj_kernel,
        grid=(n // tm, Z_COLS // tn),
        in_specs=[
            pl.BlockSpec((tm, D_MODEL), lambda i, j: (i, 0)),
            pl.BlockSpec((1, D_MODEL), lambda i, j: (0, 0)),
            pl.BlockSpec((D_MODEL, tn), lambda i, j: (0, j)),
        ],
        out_specs=pl.BlockSpec((tm, tn), lambda i, j: (i, j)),
        out_shape=jax.ShapeDtypeStruct((n, Z_COLS), F32),
        scratch_shapes=[pltpu.VMEM((tm, D_MODEL), BF16)],
        compiler_params=_cparams("parallel", "arbitrary"),
        name="in_proj",
    )(x, g, w)


V_ROWS = HEAD_DIM + 16
Q_SCALE = (HEAD_DIM ** -0.5) * math.log2(math.e)


def _attn_prep_kernel(q_ref, k_ref, v_ref, cos_ref, sin_ref, gq_ref, gk_ref, qt_ref, kr_ref, vt_ref):
    tm = q_ref.shape[0]
    cos = cos_ref[...]
    sin = sin_ref[...]
    lane = lax.broadcasted_iota(jnp.int32, (tm, LANES), 1)
    first_half = (lane % HEAD_DIM) < (HEAD_DIM // 2)
    left = lane < HEAD_DIM
    r = lax.broadcasted_iota(jnp.int32, (LANES, LANES), 0) // HEAD_DIM
    c = lax.broadcasted_iota(jnp.int32, (LANES, LANES), 1) // HEAD_DIM
    head_ones = jnp.where(r == c, 1.0, 0.0).astype(BF16)

    def norm_rope(x, g):
        xsq = x * x
        hi, lo = _split2(xsq)
        lo2 = (xsq - hi.astype(F32) - lo.astype(F32)).astype(BF16)
        ssq = _dot(hi, head_ones) + _dot(lo, head_ones) + _dot(lo2, head_ones)
        y = x * lax.rsqrt(ssq * (1.0 / HEAD_DIM) + EPS) * g
        other = jnp.where(first_half, pltpu.roll(y, LANES - HEAD_DIM // 2, 1), pltpu.roll(y, HEAD_DIM // 2, 1))
        return y * cos + other * sin

    gq = gq_ref[...]
    for jb in range(ATT_W // LANES):
        y = norm_rope(q_ref[:, jb * LANES:(jb + 1) * LANES], gq) * Q_SCALE
        ys = pltpu.roll(y, HEAD_DIM, 1)
        if (2 * jb) // GROUP == 0:
            e0 = jnp.where(left, y, 0.0)
            e1 = jnp.where(left, ys, 0.0)
        else:
            e0 = jnp.where(left, 0.0, ys)
            e1 = jnp.where(left, 0.0, y)
        qt_ref[2 * jb] = jnp.transpose(e0).astype(BF16)
        qt_ref[2 * jb + 1] = jnp.transpose(e1).astype(BF16)
    kr_ref[...] = norm_rope(k_ref[...], gk_ref[...]).astype(BF16)
    vt = jnp.transpose(v_ref[...])
    ones = jnp.ones((V_ROWS - HEAD_DIM, tm), BF16)
    for kv in range(N_KV_HEADS):
        vt_ref[kv, 0:HEAD_DIM, :] = vt[kv * HEAD_DIM:(kv + 1) * HEAD_DIM].astype(BF16)
        vt_ref[kv, HEAD_DIM:V_ROWS, :] = ones


def _attn_prep(z, cos_t, sin_t, gq, gk, L, tm=512):
    n = z.shape[0]
    lt = L // tm
    return pl.pallas_call(
        _attn_prep_kernel,
        grid=(n // tm,),
        in_specs=[
            pl.BlockSpec((tm, ATT_W), lambda i: (i, Z_AQ // ATT_W)),
            pl.BlockSpec((tm, KV_W), lambda i: (i, Z_AK // KV_W)),
            pl.BlockSpec((tm, KV_W), lambda i: (i, Z_AV // KV_W)),
            pl.BlockSpec((tm, LANES), lambda i: (i % lt, 0)),
            pl.BlockSpec((tm, LANES), lambda i: (i % lt, 0)),
            pl.BlockSpec((1, LANES), lambda i: (0, 0)),
            pl.BlockSpec((1, LANES), lambda i: (0, 0)),
        ],
        out_specs=[
            pl.BlockSpec((N_HEADS, LANES, tm), lambda i: (0, 0, i)),
            pl.BlockSpec((tm, KV_W), lambda i: (i, 0)),
            pl.BlockSpec((N_KV_HEADS, V_ROWS, tm), lambda i: (0, 0, i)),
        ],
        out_shape=[
            jax.ShapeDtypeStruct((N_HEADS, LANES, n), BF16),
            jax.ShapeDtypeStruct((n, KV_W), BF16),
            jax.ShapeDtypeStruct((N_KV_HEADS, V_ROWS, n), BF16),
        ],
        compiler_params=_cparams("parallel"),
        name="attn_prep",
    )(z, z, z, cos_t, sin_t, gq, gk)


def _flash_kernel(qt_ref, k_ref, vt_ref, o_ref, s_ref, p_ref, *, tk):
    tq = qt_ref.shape[2]
    L = k_ref.shape[0]
    nt = L // tk
    heads = range(N_HEADS)

    def scores(j, slot):
        kj = k_ref[pl.ds(pl.multiple_of(j * tk, tk), tk), :]
        for h in heads:
            s_ref[slot, h] = _dot(kj, qt_ref[h])

    def softmax(slot, ms):
        new_ms, alphas = [], []
        for h in heads:
            s = s_ref[slot, h]
            m_new = jnp.maximum(ms[h], jnp.max(s, axis=0, keepdims=True))
            new_ms.append(m_new)
            alphas.append(jnp.exp2(ms[h] - m_new))
            p_ref[slot, h] = jnp.exp2(s - m_new).astype(BF16)
        return new_ms, alphas

    def values(j, slot, alphas, accs):
        start = pl.multiple_of(j * tk, tk)
        vjs = [vt_ref[kv, :, pl.ds(start, tk)] for kv in range(N_KV_HEADS)]
        return [alphas[h] * accs[h] + _dot(vjs[h // GROUP], p_ref[slot, h]) for h in heads]

    scores(0, 0)
    p_ref[1] = jnp.zeros(p_ref.shape[1:], BF16)

    def body(jj, carry):
        ms, alphas, accs = carry
        j = 2 * jj
        accs = values(jnp.maximum(j - 1, 0), 1, alphas, accs)
        scores(j + 1, 1)
        ms, alphas = softmax(0, ms)
        accs = values(j, 0, alphas, accs)
        scores(jnp.minimum(j + 2, nt - 1), 0)
        ms, alphas = softmax(1, ms)
        return ms, alphas, accs

    init = ([jnp.full((1, tq), -jnp.inf, F32)] * N_HEADS, [jnp.ones((1, tq), F32)] * N_HEADS,
            [jnp.zeros((V_ROWS, tq), F32)] * N_HEADS)
    _, alphas, accs = lax.fori_loop(0, nt // 2, body, init)
    accs = values(nt - 1, 1, alphas, accs)
    for h in heads:
        o_ref[h * HEAD_DIM:(h + 1) * HEAD_DIM, :] = accs[h][0:HEAD_DIM] / accs[h][HEAD_DIM:HEAD_DIM + 1]


def _flash(qt, kr, vt, B, L, tq=256, tk=512):
    n = kr.shape[0]
    lt = L // tq
    assert (L // tk) % 2 == 0
    return pl.pallas_call(
        functools.partial(_flash_kernel, tk=tk),
        grid=(B, lt),
        in_specs=[
            pl.BlockSpec((N_HEADS, LANES, tq), lambda b, i: (0, 0, b * lt + i)),
            pl.BlockSpec((L, KV_W), lambda b, i: (b, 0)),
            pl.BlockSpec((N_KV_HEADS, V_ROWS, L), lambda b, i: (0, 0, b)),
        ],
        out_specs=pl.BlockSpec((ATT_W, tq), lambda b, i: (0, b * lt + i)),
        out_shape=jax.ShapeDtypeStruct((ATT_W, n), F32),
        scratch_shapes=[pltpu.VMEM((2, N_HEADS, tk, tq), F32), pltpu.VMEM((2, N_HEADS, tk, tq), BF16)],
        compiler_params=_cparams("parallel", "parallel"),
        name="flash",
    )(qt, kr, vt)


def _hy_pre_kernel(v_ref, vp_ref, vn_ref, a_ref, ap_ref, an_ref, b_ref, bp_ref, bn_ref, w_ref, cb_ref,
                   u_ref, x2_ref, *, lt):
    tm = v_ref.shape[0]
    i = pl.program_id(0)
    first = (i % lt) == 0
    last = (i % lt) == lt - 1
    rows = lax.broadcasted_iota(jnp.int32, v_ref.shape, 0)

    def conv3(m_ref, p_ref, n_ref, part):
        x = m_ref[...]
        prev = jnp.where(first, 0.0, p_ref[7:8, :])
        nxt = jnp.where(last, 0.0, n_ref[0:1, :])
        dn = jnp.where(rows == 0, prev, pltpu.roll(x, 1, 0))
        up = jnp.where(rows == tm - 1, nxt, pltpu.roll(x, tm - 1, 0))
        w = w_ref[part]
        return dn * w[0:1] + x * w[1:2] + up * w[2:3] + cb_ref[part:part + 1, :]

    u_ref[...] = conv3(v_ref, vp_ref, vn_ref, 0) * conv3(a_ref, ap_ref, an_ref, 1)
    x2_ref[...] = conv3(b_ref, bp_ref, bn_ref, 2)


def _hy_pre(z, conv_w, conv_b, L, tm=512, tc=256):
    n = z.shape[0]
    lt = L // tm
    r8 = tm // 8
    nb8 = n // 8
    specs = []
    for part in range(3):
        c0 = (Z_HZ + part * HY_W) // tc
        specs.append(pl.BlockSpec((tm, tc), lambda i, c, c0=c0: (i, c0 + c)))
        specs.append(pl.BlockSpec((8, tc), lambda i, c, c0=c0: (jnp.maximum(i * r8 - 1, 0), c0 + c)))
        specs.append(pl.BlockSpec((8, tc), lambda i, c, c0=c0: (jnp.minimum((i + 1) * r8, nb8 - 1), c0 + c)))
    specs.append(pl.BlockSpec((3, 3, tc), lambda i, c: (0, 0, c)))
    specs.append(pl.BlockSpec((3, tc), lambda i, c: (0, c)))
    return pl.pallas_call(
        functools.partial(_hy_pre_kernel, lt=lt),
        grid=(n // tm, HY_W // tc),
        in_specs=specs,
        out_specs=[pl.BlockSpec((tm, tc), lambda i, c: (i, c))] * 2,
        out_shape=[jax.ShapeDtypeStruct((n, HY_W), F32)] * 2,
        compiler_params=_cparams("parallel", "parallel"),
        name="hy_pre",
    )(*([z] * 9), conv_w, conv_b)


def _hy_filt_kernel(f_ref, w1_ref, b1_ref, f1_ref, w2_ref, b2_ref, f2_ref, w3_ref, dl_ref, kf_ref, ssq_ref):
    feats = f_ref[...]
    h = jnp.sin(f1_ref[...] * (_dot3(feats, w1_ref[...]) + b1_ref[...]))
    h = jnp.sin(f2_ref[...] * (_dot3(h, w2_ref[...]) + b2_ref[...]))
    h = _dot3(h, w3_ref[...])
    t = feats[:, 0:1]
    valid = feats[:, HY_EMB:HY_EMB + 1]
    kf = h * jnp.exp(-t * dl_ref[...]) * valid
    kf_ref[...] = kf

    @pl.when(pl.program_id(0) == 0)
    def _():
        ssq_ref[...] = jnp.zeros_like(ssq_ref)

    ssq_ref[...] += jnp.sum(kf * kf, axis=0, keepdims=True)


def _hy_filt(feats_ext, w1p, b1, f1, w2, b2, f2, w3s, deltas, L, tr=512):
    lt = L // tr
    full = lambda a: pl.BlockSpec(a.shape, lambda i: (0,) * a.ndim)
    return pl.pallas_call(
        _hy_filt_kernel,
        grid=(2 * lt,),
        in_specs=[
            pl.BlockSpec((tr, LANES), lambda i: (i, 0)),
            full(w1p), full(b1), full(f1), full(w2), full(b2), full(f2),
            pl.BlockSpec((None, HY_FILTER_W, HY_W), lambda i: (i // lt, 0, 0)),
            full(deltas),
        ],
        out_specs=[pl.BlockSpec((tr, HY_W), lambda i: (i, 0)), pl.BlockSpec((1, HY_W), lambda i: (0, 0))],
        out_shape=[jax.ShapeDtypeStruct((2 * L, HY_W), F32), jax.ShapeDtypeStruct((1, HY_W), F32)],
        compiler_params=_cparams("arbitrary"),
        name="hy_filt",
    )(feats_ext, w1p, b1, f1, w2, b2, f2, w3s, deltas)


def _fft1_kernel(x_ref, f_ref, o_ref):
    xh, xl = _split2(x_ref[...])
    o_ref[...] = _dot(f_ref[...], jnp.concatenate([xh, xl, xh], axis=0))


def _fft1(x2d, fcat):
    B, hin, wtot = x2d.shape
    rows = fcat.shape[0]
    return pl.pallas_call(
        _fft1_kernel,
        grid=(B, wtot // FFT_W),
        in_specs=[
            pl.BlockSpec((None, hin, FFT_W), lambda b, j: (b, 0, j)),
            pl.BlockSpec(fcat.shape, lambda b, j: (0, 0)),
        ],
        out_specs=pl.BlockSpec((None, rows, FFT_W), lambda b, j: (b, 0, j)),
        out_shape=jax.ShapeDtypeStruct((B, rows, wtot), F32),
        compiler_params=_cparams("parallel", "parallel"),
        name="fft1",
    )(x2d, fcat)


def _fft2_kernel(ar_ref, ai_ref, tr_ref, ti_ref, m_ref, o_ref):
    m = m_ref[...]
    for r in range(FFT_KB):
        ar, ai, tr, ti = ar_ref[r], ai_ref[r], tr_ref[r], ti_ref[r]
        b = jnp.concatenate([ar * tr - ai * ti, ar * ti + ai * tr], axis=0)
        bh, bl = _split2(b)
        o_ref[r] = _dot(m, jnp.concatenate([bh, bl, bh], axis=0))


def _fft2(a, tw_r, tw_i, mcat, kp):
    B = a.shape[0]
    C = a.shape[-1]
    nk = kp // FFT_KB
    return pl.pallas_call(
        _fft2_kernel,
        grid=(B, C // FFT_CT, nk),
        in_specs=[
            pl.BlockSpec((None, FFT_KB, FFT_N2, FFT_CT), lambda b, c, k: (b, k, 0, c)),
            pl.BlockSpec((None, FFT_KB, FFT_N2, FFT_CT), lambda b, c, k: (b, nk + k, 0, c)),
            pl.BlockSpec((FFT_KB, FFT_N2, FFT_CT), lambda b, c, k: (k, 0, 0)),
            pl.BlockSpec((FFT_KB, FFT_N2, FFT_CT), lambda b, c, k: (k, 0, 0)),
            pl.BlockSpec(mcat.shape, lambda b, c, k: (0, 0)),
        ],
        out_specs=pl.BlockSpec((None, FFT_KB, 2 * FFT_N2, FFT_CT), lambda b, c, k: (b, k, 0, c)),
        out_shape=jax.ShapeDtypeStruct((B, kp, 2 * FFT_N2, C), F32),
        compiler_params=_cparams("parallel", "parallel", "parallel"),
        name="fft2",
    )(a, a, tw_r, tw_i, mcat)


def _ifft2_kernel(x_ref, k_ref, ssq_ref, tr_ref, ti_ref, m_ref, o_ref):
    m = m_ref[...]
    scale = lax.rsqrt(ssq_ref[...] + EPS)
    n2 = FFT_N2
    for r in range(FFT_KB):
        x = x_ref[r]
        kf = k_ref[r]
        xr, xi = x[:n2], x[n2:]
        kr, ki = kf[:n2] * scale, kf[n2:] * scale
        p = jnp.concatenate([xr * kr - xi * ki, xr * ki + xi * kr], axis=0)
        ph, plo = _split2(p)
        c = _dot(m, jnp.concatenate([ph, plo, ph], axis=0))
        cr, ci = c[:n2], c[n2:]
        tr, ti = tr_ref[r], ti_ref[r]
        o_ref[0, r] = cr * tr + ci * ti
        o_ref[1, r] = ci * tr - cr * ti


def _ifft2(xs, ks, ssq, tw_r, tw_i, mcat_inv):
    B, kp, _, C = xs.shape
    nk = kp // FFT_KB
    return pl.pallas_call(
        _ifft2_kernel,
        grid=(B, C // FFT_CT, nk),
        in_specs=[
            pl.BlockSpec((None, FFT_KB, 2 * FFT_N2, FFT_CT), lambda b, c, k: (b, k, 0, c)),
            pl.BlockSpec((None, FFT_KB, 2 * FFT_N2, FFT_CT), lambda b, c, k: (0, k, 0, c)),
            pl.BlockSpec((1, FFT_CT), lambda b, c, k: (0, c)),
            pl.BlockSpec((FFT_KB, FFT_N2, FFT_CT), lambda b, c, k: (k, 0, 0)),
            pl.BlockSpec((FFT_KB, FFT_N2, FFT_CT), lambda b, c, k: (k, 0, 0)),
            pl.BlockSpec(mcat_inv.shape, lambda b, c, k: (0, 0)),
        ],
        out_specs=pl.BlockSpec((None, 2, FFT_KB, FFT_N2, FFT_CT), lambda b, c, k: (b, 0, k, 0, c)),
        out_shape=jax.ShapeDtypeStruct((B, 2, kp, FFT_N2, C), F32),
        compiler_params=_cparams("parallel", "parallel", "parallel"),
        name="ifft2",
    )(xs, ks, ssq, tw_r, tw_i, mcat_inv)


def _ifft1_kernel(d_ref, w_ref, u_ref, x2_ref, skip_ref, o_ref):
    dh, dl = _split2(d_ref[...])
    y = _dot(w_ref[...], jnp.concatenate([dh, dl, dh], axis=0))
    o_ref[...] = x2_ref[...] * (y + u_ref[...] * skip_ref[...])


def _ifft1(d2d, wcat, u2d, x22d, skip_t):
    B, rows, wtot = d2d.shape
    h = wcat.shape[0]
    return pl.pallas_call(
        _ifft1_kernel,
        grid=(B, wtot // FFT_W),
        in_specs=[
            pl.BlockSpec((None, rows, FFT_W), lambda b, j: (b, 0, j)),
            pl.BlockSpec(wcat.shape, lambda b, j: (0, 0)),
            pl.BlockSpec((None, h, FFT_W), lambda b, j: (b, 0, j)),
            pl.BlockSpec((None, h, FFT_W), lambda b, j: (b, 0, j)),
            pl.BlockSpec((1, FFT_W), lambda b, j: (0, 0)),
        ],
        out_specs=pl.BlockSpec((None, h, FFT_W), lambda b, j: (b, 0, j)),
        out_shape=jax.ShapeDtypeStruct((B, h, wtot), F32),
        compiler_params=_cparams("parallel", "parallel"),
        name="ifft1",
    )(d2d, wcat, u2d, x22d, skip_t)


def _hilo_cat(m):
    hi = m.astype(BF16)
    lo = (m - hi.astype(F32)).astype(BF16)
    return jnp.concatenate([hi, hi, lo], axis=1)


def _fft_tables(L):
    n = 2 * L
    n2 = FFT_N2
    n1 = n // n2
    h = n1 // 2
    kp = h + FFT_KB
    k1 = jnp.arange(kp, dtype=jnp.int32)

    def outer(hin):
        nn = jnp.arange(hin, dtype=jnp.int32)
        ang = (2.0 * math.pi / n1) * ((k1[:, None] * nn[None, :]) % n1).astype(F32)
        return _hilo_cat(jnp.concatenate([jnp.cos(ang), -jnp.sin(ang)], axis=0))

    f_data = outer(h)
    f_filt = outer(n1)
    j = jnp.arange(n2, dtype=jnp.int32)
    ang2 = (2.0 * math.pi / n2) * ((j[:, None] * j[None, :]) % n2).astype(F32)
    c2, s2 = jnp.cos(ang2), jnp.sin(ang2)
    m_fwd = _hilo_cat(jnp.block([[c2, s2], [-s2, c2]]))
    m_inv = _hilo_cat(jnp.block([[c2, -s2], [s2, c2]]))
    angt = (2.0 * math.pi / n) * (k1[:, None] * j[None, :]).astype(F32)
    tw_r = jnp.broadcast_to(jnp.cos(angt)[:, :, None], (kp, n2, FFT_CT))
    tw_i = jnp.broadcast_to(-jnp.sin(angt)[:, :, None], (kp, n2, FFT_CT))
    wgt = jnp.where((k1 == 0) | (k1 == h), 1.0, jnp.where(k1 < h, 2.0, 0.0)).astype(F32) / n
    nn = jnp.arange(h, dtype=jnp.int32)
    angi = (2.0 * math.pi / n1) * ((nn[:, None] * k1[None, :]) % n1).astype(F32)
    w_inv = _hilo_cat(jnp.concatenate([jnp.cos(angi) * wgt, -jnp.sin(angi) * wgt], axis=1))
    return dict(n1=n1, h=h, kp=kp, f_data=f_data, f_filt=f_filt, m_fwd=m_fwd, m_inv=m_inv,
                tw_r=tw_r, tw_i=tw_i, w_inv=w_inv)


def _filter_feats(L):
    t = jnp.linspace(0.0, 1.0, L, dtype=F32)[:, None]
    w = 2.0 * math.pi * jnp.arange(L, dtype=F32) / L
    fb = jnp.linspace(1e-4, HY_BANDS - 1, HY_BANDS, dtype=F32)
    ph = w[:, None] * fb
    feats = jnp.concatenate([t, jnp.cos(ph), -jnp.sin(ph), jnp.ones((L, 1), F32)], axis=-1)
    back = jnp.concatenate([jnp.zeros((1, HY_EMB + 1), F32), jnp.flip(feats[1:], axis=0)], axis=0)
    ext = jnp.concatenate([feats, back], axis=0)
    return jnp.pad(ext, ((0, 0), (0, LANES - HY_EMB - 1)))


def _hyena_spectrum(tabs, feats_ext, lp, L):
    kf, ssq = _hy_filt(feats_ext, lp["hy_w1p"], lp["hy_b1"], lp["hy_f1"], lp["hy_w2"], lp["hy_b2"], lp["hy_f2"],
                       lp["hy_w3s"], lp["hy_deltas"], L)
    a = _fft1(kf.reshape(1, tabs["n1"], FFT_N2 * HY_W), tabs["f_filt"])
    ks = _fft2(a.reshape(1, 2 * tabs["kp"], FFT_N2, HY_W), tabs["tw_r"], tabs["tw_i"], tabs["m_fwd"], tabs["kp"])
    return ks, ssq


def _hyena(z, tabs, ks, ssq, lp, B, L):
    u, x2c = _hy_pre(z, lp["hy_conv_w"], lp["hy_conv_b"], L)
    h, kp = tabs["h"], tabs["kp"]
    u2d = u.reshape(B, h, FFT_N2 * HY_W)
    a = _fft1(u2d, tabs["f_data"])
    xs = _fft2(a.reshape(B, 2 * kp, FFT_N2, HY_W), tabs["tw_r"], tabs["tw_i"], tabs["m_fwd"], kp)
    d = _ifft2(xs, ks, ssq, tabs["tw_r"], tabs["tw_i"], tabs["m_inv"])
    y = _ifft1(d.reshape(B, 2 * kp, FFT_N2 * HY_W), tabs["w_inv"], u2d, x2c.reshape(B, h, FFT_N2 * HY_W),
               lp["hy_skip_t"])
    return y.reshape(B * L, HY_W)


def _gla_kernel(qf_ref, kf_ref, vf_ref, lf_ref, qb_ref, kb_ref, vb_ref, lb_ref, gup_ref, gb_ref,
                of_ref, ob_ref, sf_ref, sb_ref, *, nchunk):
    C = GLA_CHUNK

    @pl.when(pl.program_id(1) == 0)
    def _():
        sf_ref[...] = jnp.zeros_like(sf_ref)
        sb_ref[...] = jnp.zeros_like(sb_ref)

    ri = lax.broadcasted_iota(jnp.int32, (C, C), 0)
    ci = lax.broadcasted_iota(jnp.int32, (C, C), 1)
    lane = lax.broadcasted_iota(jnp.int32, (C, GLA_K), 1) // GLA_DK
    ri4 = lax.broadcasted_iota(jnp.int32, (GLA_HEADS * C, C), 0) % C
    ci4 = lax.broadcasted_iota(jnp.int32, (GLA_HEADS * C, C), 1)

    def stack_heads(x):
        return jnp.concatenate([jnp.where(lane == h, x, 0.0) for h in range(GLA_HEADS)], axis=0).astype(BF16)

    def direction(q_ref, k_ref, v_ref, l_ref, o_ref, s_ref, d):
        glow = l_ref[...]
        logit = _dot3(glow, gup_ref[:, d * GLA_K:(d + 1) * GLA_K]) + gb_ref[:, d * GLA_K:(d + 1) * GLA_K]
        la_all = (jnp.minimum(logit, 0.0) - jnp.log(1.0 + jnp.exp(-jnp.abs(logit)))) * (1.0 / GLA_TAU)
        if d == 0:
            tri = jnp.where(ri >= ci, 1.0, 0.0).astype(BF16)
            keep = ri4 >= ci4
            mid, last = C // 2 - 1, C - 1
            order = range(nchunk)
        else:
            tri = jnp.where(ci >= ri, 1.0, 0.0).astype(BF16)
            keep = ci4 > ri4
            mid, last = C // 2, 0
            order = range(nchunk - 1, -1, -1)
        for c in order:
            rs = slice(c * C, (c + 1) * C)
            la = la_all[rs]
            hi, lo = _split2(la)
            lo2 = (la - hi.astype(F32) - lo.astype(F32)).astype(BF16)
            b = _dot(tri, hi) + _dot(tri, lo) + _dot(tri, lo2)
            b_mid = b[mid:mid + 1]
            b_last = b[last:last + 1]
            q = q_ref[rs, :] * (GLA_DK ** -0.5)
            k = k_ref[rs, :]
            v = v_ref[rs, :].astype(BF16)
            a = lax.dot_general(stack_heads(q * jnp.exp(b - b_mid)), (k * jnp.exp(b_mid - b)).astype(BF16),
                                (((1,), (1,)), ((), ())), preferred_element_type=F32)
            a = jnp.where(keep, a, 0.0).astype(BF16)
            s_prev = s_ref[...]
            o_inter = _dot(stack_heads(q * jnp.exp(b)), s_prev.astype(BF16))
            k_t = jnp.transpose(k * jnp.exp(b_last - b)).astype(BF16)
            outs, ups = [], []
            for h in range(GLA_HEADS):
                vh = v[:, h * GLA_DV:(h + 1) * GLA_DV]
                outs.append(_dot(a[h * C:(h + 1) * C], vh) + o_inter[h * C:(h + 1) * C])
                ups.append(_dot(k_t[h * GLA_DK:(h + 1) * GLA_DK], vh))
            decay = jnp.transpose(jnp.broadcast_to(jnp.exp(b_last), (GLA_DV, GLA_K)))
            s_ref[...] = decay * s_prev + jnp.concatenate(ups, axis=0)
            o_ref[rs, :] = jnp.concatenate(outs, axis=1)

    direction(qf_ref, kf_ref, vf_ref, lf_ref, of_ref, sf_ref, 0)
    direction(qb_ref, kb_ref, vb_ref, lb_ref, ob_ref, sb_ref, 1)


def _gla(z, gup, gb, B, L, nchunk=4):
    n = z.shape[0]
    R = nchunk * GLA_CHUNK
    T = L // R
    fwd = lambda b, t: b * T + t
    bwd = lambda b, t: b * T + (T - 1 - t)

    def specs(row):
        return [
            pl.BlockSpec((R, GLA_K), lambda b, t: (row(b, t), Z_GQ // GLA_K)),
            pl.BlockSpec((R, GLA_K), lambda b, t: (row(b, t), Z_GK // GLA_K)),
            pl.BlockSpec((R, GLA_V), lambda b, t: (row(b, t), Z_GV // GLA_V)),
            pl.BlockSpec((R, LANES), lambda b, t: (row(b, t), Z_GLOW // LANES)),
        ]

    return pl.pallas_call(
        functools.partial(_gla_kernel, nchunk=nchunk),
        grid=(B, T),
        in_specs=specs(fwd) + specs(bwd) + [
            pl.BlockSpec(gup.shape, lambda b, t: (0, 0)),
            pl.BlockSpec(gb.shape, lambda b, t: (0, 0)),
        ],
        out_specs=[
            pl.BlockSpec((R, GLA_V), lambda b, t: (fwd(b, t), 0)),
            pl.BlockSpec((R, GLA_V), lambda b, t: (bwd(b, t), 0)),
        ],
        out_shape=[jax.ShapeDtypeStruct((n, GLA_V), F32)] * 2,
        scratch_shapes=[pltpu.VMEM((GLA_K, GLA_DV), F32)] * 2,
        compiler_params=_cparams("parallel", "arbitrary"),
        name="gla",
    )(*([z] * 8), gup, gb)


def _mix_out_kernel(x_ref, ya_ref, yh_ref, of_ref, ob_ref, og_ref, g0_ref, g1_ref, g2_ref,
                    wb_ref, wo_ref, gn_ref, o_ref):
    o = of_ref[...] + ob_ref[...]
    og = og_ref[...]
    gn = gn_ref[...]
    parts = []
    for h in range(GLA_HEADS):
        blk = o[:, h * GLA_DV:(h + 1) * GLA_DV]
        ms = jnp.mean(blk * blk, axis=-1, keepdims=True)
        gate = og[:, h * GLA_DV:(h + 1) * GLA_DV]
        parts.append(blk * lax.rsqrt(ms + EPS) * gn * (gate * jax.nn.sigmoid(gate)))
    y_gla = jnp.concatenate(parts, axis=1)
    proj_att = lax.dot_general(ya_ref[...].astype(BF16), wb_ref[0], (((0,), (0,)), ((), ())),
                               preferred_element_type=F32)
    merged = jax.nn.sigmoid(g0_ref[...]) * proj_att
    merged += jax.nn.sigmoid(g1_ref[...]) * _dot(yh_ref[...].astype(BF16), wb_ref[1])
    merged += jax.nn.sigmoid(g2_ref[...]) * _dot(y_gla.astype(BF16), wb_ref[2])
    o_ref[...] = x_ref[...] + _dot(merged.astype(BF16), wo_ref[...])


def _mix_out(x, y_att, y_hy, o_f, o_b, z, wb, wo, gn, tm=256):
    n = x.shape[0]
    row = lambda w: pl.BlockSpec((tm, w), lambda i: (i, 0))
    return pl.pallas_call(
        _mix_out_kernel,
        grid=(n // tm,),
        in_specs=[
            row(D_MODEL), pl.BlockSpec((MIX_W, tm), lambda i: (0, i)), row(MIX_W), row(MIX_W), row(MIX_W),
            pl.BlockSpec((tm, GLA_V), lambda i: (i, Z_GOG // GLA_V)),
            pl.BlockSpec((tm, D_MODEL), lambda i: (i, 0)),
            pl.BlockSpec((tm, D_MODEL), lambda i: (i, 1)),
            pl.BlockSpec((tm, D_MODEL), lambda i: (i, 2)),
            pl.BlockSpec(wb.shape, lambda i: (0, 0, 0)),
            pl.BlockSpec(wo.shape, lambda i: (0, 0)),
            pl.BlockSpec((1, GLA_DV), lambda i: (0, 0)),
        ],
        out_specs=row(D_MODEL),
        out_shape=jax.ShapeDtypeStruct((n, D_MODEL), F32),
        compiler_params=_cparams("parallel"),
        name="mix_out",
    )(x, y_att, y_hy, o_f, o_b, z, z, z, z, wb, wo, gn)


def _ffn_kernel(x_ref, g_ref, wg_ref, wu_ref, wd_ref, o_ref, h_ref, acc_ref):
    j = pl.program_id(1)

    @pl.when(j == 0)
    def _():
        x = x_ref[...]
        ms = jnp.mean(x * x, axis=-1, keepdims=True)
        h_ref[...] = (x * lax.rsqrt(ms + EPS) * g_ref[...]).astype(BF16)
        acc_ref[...] = jnp.zeros_like(acc_ref)

    h = h_ref[...]
    a = _dot(h, wg_ref[...])
    act = (a * jax.nn.sigmoid(a)) * _dot(h, wu_ref[...])
    acc_ref[...] += _dot(act.astype(BF16), wd_ref[...])

    @pl.when(j == pl.num_programs(1) - 1)
    def _():
        o_ref[...] = x_ref[...] + acc_ref[...]


def _ffn(x, g, wg, wu, wd, tm=512, tf=1408):
    n = x.shape[0]
    return pl.pallas_call(
        _ffn_kernel,
        grid=(n // tm, D_FF // tf),
        in_specs=[
            pl.BlockSpec((tm, D_MODEL), lambda i, j: (i, 0)),
            pl.BlockSpec((1, D_MODEL), lambda i, j: (0, 0)),
            pl.BlockSpec((D_MODEL, tf), lambda i, j: (0, j)),
            pl.BlockSpec((D_MODEL, tf), lambda i, j: (0, j)),
            pl.BlockSpec((tf, D_MODEL), lambda i, j: (j, 0)),
        ],
        out_specs=pl.BlockSpec((tm, D_MODEL), lambda i, j: (i, 0)),
        out_shape=jax.ShapeDtypeStruct((n, D_MODEL), F32),
        scratch_shapes=[pltpu.VMEM((tm, D_MODEL), BF16), pltpu.VMEM((tm, D_MODEL), F32)],
        compiler_params=_cparams("parallel", "arbitrary"),
        name="ffn",
    )(x, g, wg, wu, wd)


def _final_norm_kernel(x_ref, g_ref, o_ref):
    x = x_ref[...]
    ms = jnp.mean(x * x, axis=-1, keepdims=True)
    o_ref[...] = x * lax.rsqrt(ms + EPS) * g_ref[...]


def _final_norm(x, g, tm=512):
    n = x.shape[0]
    return pl.pallas_call(
        _final_norm_kernel,
        grid=(n // tm,),
        in_specs=[pl.BlockSpec((tm, D_MODEL), lambda i: (i, 0)), pl.BlockSpec((1, D_MODEL), lambda i: (0, 0))],
        out_specs=pl.BlockSpec((tm, D_MODEL), lambda i: (i, 0)),
        out_shape=jax.ShapeDtypeStruct((n, D_MODEL), F32),
        compiler_params=_cparams("parallel"),
        name="final_norm",
    )(x, g)


def _rope_tables(L):
    rows = L // GRID_W
    r = jnp.repeat(jnp.arange(rows, dtype=F32), GRID_W)
    c = jnp.tile(jnp.arange(GRID_W, dtype=F32), rows)
    inv = ROPE_THETA ** (-jnp.arange(0, ROPE_AXIS_DIM, 2, dtype=F32) / ROPE_AXIS_DIM)
    ang = jnp.concatenate([r[:, None] * inv, c[:, None] * inv], axis=-1)
    cos, sin = jnp.cos(ang), jnp.sin(ang)
    cos_t = jnp.tile(cos, (1, 2 * LANES // HEAD_DIM))
    sin_t = jnp.tile(jnp.concatenate([-sin, sin], axis=-1), (1, LANES // HEAD_DIM))
    return cos_t, sin_t


def _layer_params(l, norm_mix_g, w_in, q_norm_g, k_norm_g, hy_conv_w, hy_conv_b, hy_w1, hy_b1, hy_f1, hy_w2,
                  hy_b2, hy_f2, hy_w3, hy_skip, gla_gate_up, gla_gate_b, gla_norm_g, w_branch, w_out, norm_ffn_g,
                  w_ffn_gate, w_ffn_up, w_ffn_down):
    w = w_in[l]
    offs = {}
    off = 0
    for name, size in (("aq", ATT_W), ("ak", KV_W), ("av", KV_W), ("hz", 3 * HY_W), ("gq", GLA_K), ("gk", GLA_K),
                       ("gv", GLA_V), ("gog", GLA_V), ("glow", 2 * GLA_RANK), ("gates", N_BRANCH * D_MODEL)):
        offs[name] = w[:, off:off + size]
        off += size
    w_cat = jnp.concatenate(
        [offs["gates"], offs["aq"], offs["gv"], offs["gog"], offs["hz"], offs["gq"], offs["gk"], offs["ak"],
         offs["av"], offs["glow"], jnp.zeros((D_MODEL, Z_COLS - Z_GLOW - 2 * GLA_RANK), F32)], axis=1).astype(BF16)
    gup = jnp.zeros((LANES, 2 * GLA_K), F32)
    gup = gup.at[0:GLA_RANK, 0:GLA_K].set(gla_gate_up[l, 0])
    gup = gup.at[GLA_RANK:2 * GLA_RANK, GLA_K:2 * GLA_K].set(gla_gate_up[l, 1])
    deltas = jnp.linspace(abs(math.log(HY_TARGET) / HY_SLOW), abs(math.log(HY_TARGET) / HY_FAST), HY_W, dtype=F32)
    return dict(
        norm_mix_g=norm_mix_g[l][None, :],
        w_cat=w_cat,
        gq=jnp.tile(q_norm_g[l], LANES // HEAD_DIM)[None, :],
        gk=jnp.tile(k_norm_g[l], LANES // HEAD_DIM)[None, :],
        hy_conv_w=jnp.transpose(hy_conv_w[l].reshape(3, 3, HY_W), (1, 0, 2)),
        hy_conv_b=hy_conv_b[l].reshape(3, HY_W),
        hy_w1p=jnp.pad(hy_w1[l], ((0, LANES - HY_EMB), (0, 0))),
        hy_b1=hy_b1[l][None, :], hy_f1=hy_f1[l][None, :],
        hy_w2=hy_w2[l], hy_b2=hy_b2[l][None, :], hy_f2=hy_f2[l][None, :],
        hy_w3s=jnp.transpose(hy_w3[l].reshape(HY_FILTER_W, 2, HY_W), (1, 0, 2)),
        hy_deltas=deltas[None, :],
        hy_skip_t=jnp.tile(hy_skip[l], FFT_W // HY_W)[None, :],
        gup=gup, gb=gla_gate_b[l].reshape(1, 2 * GLA_K),
        gn=gla_norm_g[l][None, :],
        wb=w_branch[l].astype(BF16), wo=w_out[l].astype(BF16),
        norm_ffn_g=norm_ffn_g[l][None, :],
        wg=w_ffn_gate[l].astype(BF16), wu=w_ffn_up[l].astype(BF16), wd=w_ffn_down[l].astype(BF16),
    )


def _encoder_layer(x, lp, B, L, rope, tabs, ks, ssq):
    z = _in_proj(x, lp["norm_mix_g"], lp["w_cat"])
    qt, kr, vt = _attn_prep(z, rope[0], rope[1], lp["gq"], lp["gk"], L)
    y_att = _flash(qt, kr, vt, B, L)
    y_hy = _hyena(z, tabs, ks, ssq, lp, B, L)
    o_f, o_b = _gla(z, lp["gup"], lp["gb"], B, L)
    x = _mix_out(x, y_att, y_hy, o_f, o_b, z, lp["wb"], lp["wo"], lp["gn"])
    return _ffn(x, lp["norm_ffn_g"], lp["wg"], lp["wu"], lp["wd"])


def kernel(x_prompt, x_sample, norm_mix_g, w_in, q_norm_g, k_norm_g, hy_conv_w, hy_conv_b, hy_w1, hy_b1, hy_f1, hy_w2, hy_b2, hy_f2, hy_w3, hy_skip, gla_gate_up, gla_gate_b, gla_norm_g, w_branch, w_out, norm_ffn_g, w_ffn_gate, w_ffn_up, w_ffn_down, final_norm_g):
    streams = []
    for xin in (x_prompt, x_sample):
        B, L, _ = xin.shape
        streams.append(dict(x=xin.reshape(B * L, D_MODEL), B=B, L=L, rope=_rope_tables(L), tabs=_fft_tables(L),
                            feats=_filter_feats(L)))
    for l in range(DEPTH):
        lp = _layer_params(l, norm_mix_g, w_in, q_norm_g, k_norm_g, hy_conv_w, hy_conv_b, hy_w1, hy_b1, hy_f1,
                           hy_w2, hy_b2, hy_f2, hy_w3, hy_skip, gla_gate_up, gla_gate_b, gla_norm_g, w_branch,
                           w_out, norm_ffn_g, w_ffn_gate, w_ffn_up, w_ffn_down)
        for s in streams:
            ks, ssq = _hyena_spectrum(s["tabs"], s["feats"], lp, s["L"])
            s["x"] = _encoder_layer(s["x"], lp, s["B"], s["L"], s["rope"], s["tabs"], ks, ssq)
    outs = []
    for s, xin in zip(streams, (x_prompt, x_sample)):
        outs.append(_final_norm(s["x"], final_norm_g[None, :]).reshape(xin.shape))
    return tuple(outs)
```

```python
import functools
import math

import jax
import jax.numpy as jnp
from jax import lax
from jax.experimental import pallas as pl
from jax.experimental.pallas import tpu as pltpu

F32 = jnp.float32
BF16 = jnp.bfloat16

D_MODEL = 1024
DEPTH = 4
GRID_W = 64
N_HEADS = 8
N_KV_HEADS = 2
HEAD_DIM = 64
GROUP = N_HEADS // N_KV_HEADS
ROPE_THETA = 10000.0
ROPE_AXIS_DIM = HEAD_DIM // 2
ATT_W = N_HEADS * HEAD_DIM
KV_W = N_KV_HEADS * HEAD_DIM
HY_W = D_MODEL // 2
HY_EMB = 33
HY_BANDS = (HY_EMB - 1) // 2
HY_FILTER_W = 64
HY_TARGET = 1e-2
HY_FAST = 0.3
HY_SLOW = 1.5
GLA_HEADS = 4
GLA_DK = 64
GLA_DV = 128
GLA_K = GLA_HEADS * GLA_DK
GLA_V = GLA_HEADS * GLA_DV
GLA_RANK = 16
GLA_TAU = 16.0
GLA_CHUNK = 64
N_BRANCH = 3
MIX_W = 512
D_FF = 2816
EPS = 1e-6

LANES = 128
VMEM_LIMIT = 56 * 1024 * 1024

Z_GATES = 0
Z_AQ = 3072
Z_GV = 3584
Z_GOG = 4096
Z_HZ = 4608
Z_GQ = 6144
Z_GK = 6400
Z_AK = 6656
Z_AV = 6784
Z_GLOW = 6912
Z_COLS = 7168

FFT_N2 = 128
FFT_KB = 8
FFT_SUB = 8
FFT_JT = 4
FFT_CT = 256


def _cparams(*sem):
    return pltpu.CompilerParams(dimension_semantics=sem, vmem_limit_bytes=VMEM_LIMIT)


def _split2(x):
    hi = x.astype(BF16)
    lo = (x - hi.astype(F32)).astype(BF16)
    return hi, lo


def _dot(a, b):
    return jnp.dot(a, b, preferred_element_type=F32)


def _dot3(a, b):
    ah, al = _split2(a)
    bh, bl = _split2(b)
    return _dot(ah, bh) + _dot(al, bh) + _dot(ah, bl)


def _in_proj_kernel(x_ref, g_ref, w_ref, o_ref, h_ref):
    @pl.when(pl.program_id(1) == 0)
    def _():
        x = x_ref[...]
        ms = jnp.mean(x * x, axis=-1, keepdims=True)
        h_ref[...] = (x * lax.rsqrt(ms + EPS) * g_ref[...]).astype(BF16)

    o_ref[...] = _dot(h_ref[...], w_ref[...])


def _in_proj(x, g, w, tm=1024, tn=1024):
    n = x.shape[0]
    return pl.pallas_call(
        _in_proj_kernel,
        grid=(n // tm, Z_COLS // tn),
        in_specs=[
            pl.BlockSpec((tm, D_MODEL), lambda i, j: (i, 0)),
            pl.BlockSpec((1, D_MODEL), lambda i, j: (0, 0)),
            pl.BlockSpec((D_MODEL, tn), lambda i, j: (0, j)),
        ],
        out_specs=pl.BlockSpec((tm, tn), lambda i, j: (i, j)),
        out_shape=jax.ShapeDtypeStruct((n, Z_COLS), F32),
        scratch_shapes=[pltpu.VMEM((tm, D_MODEL), BF16)],
        compiler_params=_cparams("parallel", "arbitrary"),
        name="in_proj",
    )(x, g, w)


V_ROWS = HEAD_DIM + 16
Q_SCALE = (HEAD_DIM ** -0.5) * math.log2(math.e)


def _attn_prep_kernel(q_ref, k_ref, v_ref, cos_ref, sin_ref, gq_ref, gk_ref, qt_ref, kr_ref, vt_ref):
    tm = q_ref.shape[0]
    cos = cos_ref[...]
    sin = sin_ref[...]
    lane = lax.broadcasted_iota(jnp.int32, (tm, LANES), 1)
    first_half = (lane % HEAD_DIM) < (HEAD_DIM // 2)
    left = lane < HEAD_DIM
    r = lax.broadcasted_iota(jnp.int32, (LANES, LANES), 0) // HEAD_DIM
    c = lax.broadcasted_iota(jnp.int32, (LANES, LANES), 1) // HEAD_DIM
    head_ones = jnp.where(r == c, 1.0, 0.0).astype(BF16)

    def norm_rope(x, g):
        xsq = x * x
        hi, lo = _split2(xsq)
        lo2 = (xsq - hi.astype(F32) - lo.astype(F32)).astype(BF16)
        ssq = _dot(hi, head_ones) + _dot(lo, head_ones) + _dot(lo2, head_ones)
        y = x * lax.rsqrt(ssq * (1.0 / HEAD_DIM) + EPS) * g
        other = jnp.where(first_half, pltpu.roll(y, LANES - HEAD_DIM // 2, 1), pltpu.roll(y, HEAD_DIM // 2, 1))
        return y * cos + other * sin

    gq = gq_ref[...]
    for jb in range(ATT_W // LANES):
        y = norm_rope(q_ref[:, jb * LANES:(jb + 1) * LANES], gq) * Q_SCALE
        ys = pltpu.roll(y, HEAD_DIM, 1)
        if (2 * jb) // GROUP == 0:
            e0 = jnp.where(left, y, 0.0)
            e1 = jnp.where(left, ys, 0.0)
        else:
            e0 = jnp.where(left, 0.0, ys)
            e1 = jnp.where(left, 0.0, y)
        qt_ref[2 * jb] = jnp.transpose(e0).astype(BF16)
        qt_ref[2 * jb + 1] = jnp.transpose(e1).astype(BF16)
    kr_ref[...] = norm_rope(k_ref[...], gk_ref[...]).astype(BF16)
    vt = jnp.transpose(v_ref[...])
    ones = jnp.ones((V_ROWS - HEAD_DIM, tm), BF16)
    for kv in range(N_KV_HEADS):
        vt_ref[kv, 0:HEAD_DIM, :] = vt[kv * HEAD_DIM:(kv + 1) * HEAD_DIM].astype(BF16)
        vt_ref[kv, HEAD_DIM:V_ROWS, :] = ones


def _attn_prep(z, cos_t, sin_t, gq, gk, L, tm=512):
    n = z.shape[0]
    lt = L // tm
    return pl.pallas_call(
        _attn_prep_kernel,
        grid=(n // tm,),
        in_specs=[
            pl.BlockSpec((tm, ATT_W), lambda i: (i, Z_AQ // ATT_W)),
            pl.BlockSpec((tm, KV_W), lambda i: (i, Z_AK // KV_W)),
            pl.BlockSpec((tm, KV_W), lambda i: (i, Z_AV // KV_W)),
            pl.BlockSpec((tm, LANES), lambda i: (i % lt, 0)),
            pl.BlockSpec((tm, LANES), lambda i: (i % lt, 0)),
            pl.BlockSpec((1, LANES), lambda i: (0, 0)),
            pl.BlockSpec((1, LANES), lambda i: (0, 0)),
        ],
        out_specs=[
            pl.BlockSpec((N_HEADS, LANES, tm), lambda i: (0, 0, i)),
            pl.BlockSpec((tm, KV_W), lambda i: (i, 0)),
            pl.BlockSpec((N_KV_HEADS, V_ROWS, tm), lambda i: (0, 0, i)),
        ],
        out_shape=[
            jax.ShapeDtypeStruct((N_HEADS, LANES, n), BF16),
            jax.ShapeDtypeStruct((n, KV_W), BF16),
            jax.ShapeDtypeStruct((N_KV_HEADS, V_ROWS, n), BF16),
        ],
        compiler_params=_cparams("parallel"),
        name="attn_prep",
    )(z, z, z, cos_t, sin_t, gq, gk)


def _flash_kernel(qt_ref, k_ref, vt_ref, o_ref, s_ref, p_ref, *, tk):
    tq = qt_ref.shape[2]
    L = k_ref.shape[0]
    nt = L // tk
    heads = range(N_HEADS)

    def scores(j, slot):
        kj = k_ref[pl.ds(pl.multiple_of(j * tk, tk), tk), :]
        for h in heads:
            s_ref[slot, h] = _dot(kj, qt_ref[h])

    def softmax(slot, ms):
        new_ms, alphas = [], []
        for h in heads:
            s = s_ref[slot, h]
            m_new = jnp.maximum(ms[h], jnp.max(s, axis=0, keepdims=True))
            new_ms.append(m_new)
            alphas.append(jnp.exp2(ms[h] - m_new))
            p_ref[slot, h] = jnp.exp2(s - m_new).astype(BF16)
        return new_ms, alphas

    def values(j, slot, alphas, accs):
        start = pl.multiple_of(j * tk, tk)
        vjs = [vt_ref[kv, :, pl.ds(start, tk)] for kv in range(N_KV_HEADS)]
        return [alphas[h] * accs[h] + _dot(vjs[h // GROUP], p_ref[slot, h]) for h in heads]

    scores(0, 0)
    p_ref[1] = jnp.zeros(p_ref.shape[1:], BF16)

    def body(jj, carry):
        ms, alphas, accs = carry
        j = 2 * jj
        accs = values(jnp.maximum(j - 1, 0), 1, alphas, accs)
        scores(j + 1, 1)
        ms, alphas = softmax(0, ms)
        accs = values(j, 0, alphas, accs)
        scores(jnp.minimum(j + 2, nt - 1), 0)
        ms, alphas = softmax(1, ms)
        return ms, alphas, accs

    init = ([jnp.full((1, tq), -jnp.inf, F32)] * N_HEADS, [jnp.ones((1, tq), F32)] * N_HEADS,
            [jnp.zeros((V_ROWS, tq), F32)] * N_HEADS)
    _, alphas, accs = lax.fori_loop(0, nt // 2, body, init)
    accs = values(nt - 1, 1, alphas, accs)
    for h in heads:
        o_ref[h * HEAD_DIM:(h + 1) * HEAD_DIM, :] = accs[h][0:HEAD_DIM] / accs[h][HEAD_DIM:HEAD_DIM + 1]


def _flash(qt, kr, vt, B, L, tq=256, tk=512):
    n = kr.shape[0]
    lt = L // tq
    assert (L // tk) % 2 == 0
    return pl.pallas_call(
        functools.partial(_flash_kernel, tk=tk),
        grid=(B, lt),
        in_specs=[
            pl.BlockSpec((N_HEADS, LANES, tq), lambda b, i: (0, 0, b * lt + i)),
            pl.BlockSpec((L, KV_W), lambda b, i: (b, 0)),
            pl.BlockSpec((N_KV_HEADS, V_ROWS, L), lambda b, i: (0, 0, b)),
        ],
        out_specs=pl.BlockSpec((ATT_W, tq), lambda b, i: (0, b * lt + i)),
        out_shape=jax.ShapeDtypeStruct((ATT_W, n), F32),
        scratch_shapes=[pltpu.VMEM((2, N_HEADS, tk, tq), F32), pltpu.VMEM((2, N_HEADS, tk, tq), BF16)],
        compiler_params=_cparams("parallel", "parallel"),
        name="flash",
    )(qt, kr, vt)


def _hy_pre_kernel(v_ref, vp_ref, vn_ref, a_ref, ap_ref, an_ref, b_ref, bp_ref, bn_ref, w_ref, cb_ref,
                   u_ref, x2_ref, *, lt):
    tm = v_ref.shape[0]
    i = pl.program_id(0)
    first = (i % lt) == 0
    last = (i % lt) == lt - 1
    rows = lax.broadcasted_iota(jnp.int32, v_ref.shape, 0)

    def conv3(m_ref, p_ref, n_ref, part):
        x = m_ref[...]
        prev = jnp.where(first, 0.0, p_ref[7:8, :])
        nxt = jnp.where(last, 0.0, n_ref[0:1, :])
        dn = jnp.where(rows == 0, prev, pltpu.roll(x, 1, 0))
        up = jnp.where(rows == tm - 1, nxt, pltpu.roll(x, tm - 1, 0))
        w = w_ref[part]
        return dn * w[0:1] + x * w[1:2] + up * w[2:3] + cb_ref[part:part + 1, :]

    u_ref[...] = conv3(v_ref, vp_ref, vn_ref, 0) * conv3(a_ref, ap_ref, an_ref, 1)
    x2_ref[...] = conv3(b_ref, bp_ref, bn_ref, 2)


def _hy_pre(z, conv_w, conv_b, L, tm=512, tc=256):
    n = z.shape[0]
    lt = L // tm
    r8 = tm // 8
    nb8 = n // 8
    specs = []
    for part in range(3):
        c0 = (Z_HZ + part * HY_W) // tc
        specs.append(pl.BlockSpec((tm, tc), lambda i, c, c0=c0: (i, c0 + c)))
        specs.append(pl.BlockSpec((8, tc), lambda i, c, c0=c0: (jnp.maximum(i * r8 - 1, 0), c0 + c)))
        specs.append(pl.BlockSpec((8, tc), lambda i, c, c0=c0: (jnp.minimum((i + 1) * r8, nb8 - 1), c0 + c)))
    specs.append(pl.BlockSpec((3, 3, tc), lambda i, c: (0, 0, c)))
    specs.append(pl.BlockSpec((3, tc), lambda i, c: (0, c)))
    return pl.pallas_call(
        functools.partial(_hy_pre_kernel, lt=lt),
        grid=(n // tm, HY_W // tc),
        in_specs=specs,
        out_specs=[pl.BlockSpec((tm, tc), lambda i, c: (i, c))] * 2,
        out_shape=[jax.ShapeDtypeStruct((n, HY_W), F32)] * 2,
        compiler_params=_cparams("parallel", "parallel"),
        name="hy_pre",
    )(*([z] * 9), conv_w, conv_b)


def _hy_filt_kernel(f_ref, w1_ref, b1_ref, f1_ref, w2_ref, b2_ref, f2_ref, w3_ref, dl_ref, kf_ref, ssq_ref):
    feats = f_ref[...]
    h = jnp.sin(f1_ref[...] * (_dot3(feats, w1_ref[...]) + b1_ref[...]))
    h = jnp.sin(f2_ref[...] * (_dot3(h, w2_ref[...]) + b2_ref[...]))
    h = _dot3(h, w3_ref[...])
    t = feats[:, 0:1]
    valid = feats[:, HY_EMB:HY_EMB + 1]
    kf = h * jnp.exp(-t * dl_ref[...]) * valid
    kf_ref[...] = kf

    @pl.when(pl.program_id(0) == 0)
    def _():
        ssq_ref[...] = jnp.zeros_like(ssq_ref)

    ssq_ref[...] += jnp.sum(kf * kf, axis=0, keepdims=True)


def _hy_filt(feats_ext, w1p, b1, f1, w2, b2, f2, w3s, deltas, L, tr=512):
    lt = L // tr
    full = lambda a: pl.BlockSpec(a.shape, lambda i: (0,) * a.ndim)
    return pl.pallas_call(
        _hy_filt_kernel,
        grid=(2 * lt,),
        in_specs=[
            pl.BlockSpec((tr, LANES), lambda i: (i, 0)),
            full(w1p), full(b1), full(f1), full(w2), full(b2), full(f2),
            pl.BlockSpec((None, HY_FILTER_W, HY_W), lambda i: (i // lt, 0, 0)),
            full(deltas),
        ],
        out_specs=[pl.BlockSpec((tr, HY_W), lambda i: (i, 0)), pl.BlockSpec((1, HY_W), lambda i: (0, 0))],
        out_shape=[jax.ShapeDtypeStruct((2 * L, HY_W), F32), jax.ShapeDtypeStruct((1, HY_W), F32)],
        compiler_params=_cparams("arbitrary"),
        name="hy_filt",
    )(feats_ext, w1p, b1, f1, w2, b2, f2, w3s, deltas)


def _fft1_kernel(x_ref, f_ref, o_ref):
    hin, rows = x_ref.shape[0], o_ref.shape[0]
    npos = x_ref.shape[1] * x_ref.shape[2]
    x2 = x_ref.reshape(hin * npos, LANES)
    o2 = o_ref.reshape(rows * npos, LANES)
    f = f_ref[...]
    for p in range(0, npos, 2):
        x = jnp.concatenate([x2[pl.ds(p, hin, stride=npos), :], x2[pl.ds(p + 1, hin, stride=npos), :]], axis=1)
        xh, xl = _split2(x)
        y = _dot(f, jnp.concatenate([xh, xl, xh], axis=0))
        o2[pl.ds(p, rows, stride=npos), :] = y[:, :LANES]
        o2[pl.ds(p + 1, rows, stride=npos), :] = y[:, LANES:]


def _fft1(x, fcat, B, hin):
    C = x.shape[-1]
    rows = fcat.shape[0]
    nj = FFT_N2 // FFT_SUB
    out = pl.pallas_call(
        _fft1_kernel,
        grid=(B, nj // FFT_JT, C // LANES),
        in_specs=[
            pl.BlockSpec((None, hin, FFT_JT, FFT_SUB, LANES), lambda b, j, c: (b, 0, j, 0, c)),
            pl.BlockSpec(fcat.shape, lambda b, j, c: (0, 0)),
        ],
        out_specs=pl.BlockSpec((None, rows, FFT_JT, FFT_SUB, LANES), lambda b, j, c: (b, 0, j, 0, c)),
        out_shape=jax.ShapeDtypeStruct((B, rows, nj, FFT_SUB, C), F32),
        compiler_params=_cparams("parallel", "parallel", "parallel"),
        name="fft1",
    )(x.reshape(B, hin, nj, FFT_SUB, C), fcat)
    return out.reshape(B, rows, FFT_N2, C)


def _fft2_kernel(ar_ref, ai_ref, tr_ref, ti_ref, m_ref, o_ref):
    m = m_ref[...]
    for r in range(FFT_KB):
        ar, ai, tr, ti = ar_ref[r], ai_ref[r], tr_ref[r], ti_ref[r]
        b = jnp.concatenate([ar * tr - ai * ti, ar * ti + ai * tr], axis=0)
        bh, bl = _split2(b)
        o_ref[r] = _dot(m, jnp.concatenate([bh, bl, bh], axis=0))


def _fft2(a, tw_r, tw_i, mcat, kp):
    B = a.shape[0]
    C = a.shape[-1]
    nk = kp // FFT_KB
    return pl.pallas_call(
        _fft2_kernel,
        grid=(B, C // FFT_CT, nk),
        in_specs=[
            pl.BlockSpec((None, FFT_KB, FFT_N2, FFT_CT), lambda b, c, k: (b, k, 0, c)),
            pl.BlockSpec((None, FFT_KB, FFT_N2, FFT_CT), lambda b, c, k: (b, nk + k, 0, c)),
            pl.BlockSpec((FFT_KB, FFT_N2, FFT_CT), lambda b, c, k: (k, 0, 0)),
            pl.BlockSpec((FFT_KB, FFT_N2, FFT_CT), lambda b, c, k: (k, 0, 0)),
            pl.BlockSpec(mcat.shape, lambda b, c, k: (0, 0)),
        ],
        out_specs=pl.BlockSpec((None, FFT_KB, 2 * FFT_N2, FFT_CT), lambda b, c, k: (b, k, 0, c)),
        out_shape=jax.ShapeDtypeStruct((B, kp, 2 * FFT_N2, C), F32),
        compiler_params=_cparams("parallel", "parallel", "parallel"),
        name="fft2",
    )(a, a, tw_r, tw_i, mcat)


def _ifft2_kernel(x_ref, k_ref, ssq_ref, tr_ref, ti_ref, m_ref, o_ref):
    m = m_ref[...]
    scale = lax.rsqrt(ssq_ref[...] + EPS)
    n2 = FFT_N2
    for r in range(FFT_KB):
        x = x_ref[r]
        kf = k_ref[r]
        xr, xi = x[:n2], x[n2:]
        kr, ki = kf[:n2] * scale, kf[n2:] * scale
        p = jnp.concatenate([xr * kr - xi * ki, xr * ki + xi * kr], axis=0)
        ph, plo = _split2(p)
        c = _dot(m, jnp.concatenate([ph, plo, ph], axis=0))
        cr, ci = c[:n2], c[n2:]
        tr, ti = tr_ref[r], ti_ref[r]
        o_ref[0, r] = cr * tr + ci * ti
        o_ref[1, r] = ci * tr - cr * ti


def _ifft2(xs, ks, ssq, tw_r, tw_i, mcat_inv):
    B, kp, _, C = xs.shape
    nk = kp // FFT_KB
    return pl.pallas_call(
        _ifft2_kernel,
        grid=(B, C // FFT_CT, nk),
        in_specs=[
            pl.BlockSpec((None, FFT_KB, 2 * FFT_N2, FFT_CT), lambda b, c, k: (b, k, 0, c)),
            pl.BlockSpec((None, FFT_KB, 2 * FFT_N2, FFT_CT), lambda b, c, k: (0, k, 0, c)),
            pl.BlockSpec((1, FFT_CT), lambda b, c, k: (0, c)),
            pl.BlockSpec((FFT_KB, FFT_N2, FFT_CT), lambda b, c, k: (k, 0, 0)),
            pl.BlockSpec((FFT_KB, FFT_N2, FFT_CT), lambda b, c, k: (k, 0, 0)),
            pl.BlockSpec(mcat_inv.shape, lambda b, c, k: (0, 0)),
        ],
        out_specs=pl.BlockSpec((None, 2, FFT_KB, FFT_N2, FFT_CT), lambda b, c, k: (b, 0, k, 0, c)),
        out_shape=jax.ShapeDtypeStruct((B, 2, kp, FFT_N2, C), F32),
        compiler_params=_cparams("parallel", "parallel", "parallel"),
        name="ifft2",
    )(xs, ks, ssq, tw_r, tw_i, mcat_inv)


def _ifft1_kernel(d_ref, w_ref, u_ref, x2_ref, skip_ref, o_ref):
    rows, h = d_ref.shape[0], o_ref.shape[0]
    npos = d_ref.shape[1] * d_ref.shape[2]
    d2 = d_ref.reshape(rows * npos, LANES)
    u2, x22, o2 = (r.reshape(h * npos, LANES) for r in (u_ref, x2_ref, o_ref))
    w = w_ref[...]
    skip = skip_ref[...]
    for p in range(0, npos, 2):
        d = jnp.concatenate([d2[pl.ds(p, rows, stride=npos), :], d2[pl.ds(p + 1, rows, stride=npos), :]], axis=1)
        dh, dl = _split2(d)
        y = _dot(w, jnp.concatenate([dh, dl, dh], axis=0))
        for i in range(2):
            sl = pl.ds(p + i, h, stride=npos)
            o2[sl, :] = x22[sl, :] * (y[:, i * LANES:(i + 1) * LANES] + u2[sl, :] * skip)


def _ifft1(d, wcat, u, x2c, skip, B):
    rows, C = d.shape[1], d.shape[-1]
    h = wcat.shape[0]
    nj = FFT_N2 // FFT_SUB
    seq = pl.BlockSpec((None, h, FFT_JT, FFT_SUB, LANES), lambda b, j, c: (b, 0, j, 0, c))
    out = pl.pallas_call(
        _ifft1_kernel,
        grid=(B, nj // FFT_JT, C // LANES),
        in_specs=[
            pl.BlockSpec((None, rows, FFT_JT, FFT_SUB, LANES), lambda b, j, c: (b, 0, j, 0, c)),
            pl.BlockSpec(wcat.shape, lambda b, j, c: (0, 0)),
            seq, seq,
            pl.BlockSpec((1, LANES), lambda b, j, c: (0, c)),
        ],
        out_specs=seq,
        out_shape=jax.ShapeDtypeStruct((B, h, nj, FFT_SUB, C), F32),
        compiler_params=_cparams("parallel", "parallel", "parallel"),
        name="ifft1",
    )(d.reshape(B, rows, nj, FFT_SUB, C), wcat, u.reshape(B, h, nj, FFT_SUB, C), x2c.reshape(B, h, nj, FFT_SUB, C),
      skip)
    return out.reshape(B * h * FFT_N2, C)


def _hilo_cat(m):
    hi = m.astype(BF16)
    lo = (m - hi.astype(F32)).astype(BF16)
    return jnp.concatenate([hi, hi, lo], axis=1)


def _fft_tables(L):
    n = 2 * L
    n2 = FFT_N2
    n1 = n // n2
    h = n1 // 2
    kp = h + FFT_KB
    k1 = jnp.arange(kp, dtype=jnp.int32)

    def outer(hin):
        nn = jnp.arange(hin, dtype=jnp.int32)
        ang = (2.0 * math.pi / n1) * ((k1[:, None] * nn[None, :]) % n1).astype(F32)
        return _hilo_cat(jnp.concatenate([jnp.cos(ang), -jnp.sin(ang)], axis=0))

    f_data = outer(h)
    f_filt = outer(n1)
    j = jnp.arange(n2, dtype=jnp.int32)
    ang2 = (2.0 * math.pi / n2) * ((j[:, None] * j[None, :]) % n2).astype(F32)
    c2, s2 = jnp.cos(ang2), jnp.sin(ang2)
    m_fwd = _hilo_cat(jnp.block([[c2, s2], [-s2, c2]]))
    m_inv = _hilo_cat(jnp.block([[c2, -s2], [s2, c2]]))
    angt = (2.0 * math.pi / n) * (k1[:, None] * j[None, :]).astype(F32)
    tw_r = jnp.broadcast_to(jnp.cos(angt)[:, :, None], (kp, n2, FFT_CT))
    tw_i = jnp.broadcast_to(-jnp.sin(angt)[:, :, None], (kp, n2, FFT_CT))
    wgt = jnp.where((k1 == 0) | (k1 == h), 1.0, jnp.where(k1 < h, 2.0, 0.0)).astype(F32) / n
    nn = jnp.arange(h, dtype=jnp.int32)
    angi = (2.0 * math.pi / n1) * ((nn[:, None] * k1[None, :]) % n1).astype(F32)
    w_inv = _hilo_cat(jnp.concatenate([jnp.cos(angi) * wgt, -jnp.sin(angi) * wgt], axis=1))
    return dict(n1=n1, h=h, kp=kp, f_data=f_data, f_filt=f_filt, m_fwd=m_fwd, m_inv=m_inv,
                tw_r=tw_r, tw_i=tw_i, w_inv=w_inv)


def _filter_feats(L):
    t = jnp.linspace(0.0, 1.0, L, dtype=F32)[:, None]
    w = 2.0 * math.pi * jnp.arange(L, dtype=F32) / L
    fb = jnp.linspace(1e-4, HY_BANDS - 1, HY_BANDS, dtype=F32)
    ph = w[:, None] * fb
    feats = jnp.concatenate([t, jnp.cos(ph), -jnp.sin(ph), jnp.ones((L, 1), F32)], axis=-1)
    back = jnp.concatenate([jnp.zeros((1, HY_EMB + 1), F32), jnp.flip(feats[1:], axis=0)], axis=0)
    ext = jnp.concatenate([feats, back], axis=0)
    return jnp.pad(ext, ((0, 0), (0, LANES - HY_EMB - 1)))


def _hyena_spectrum(tabs, feats_ext, lp, L):
    kf, ssq = _hy_filt(feats_ext, lp["hy_w1p"], lp["hy_b1"], lp["hy_f1"], lp["hy_w2"], lp["hy_b2"], lp["hy_f2"],
                       lp["hy_w3s"], lp["hy_deltas"], L)
    a = _fft1(kf, tabs["f_filt"], 1, tabs["n1"])
    ks = _fft2(a, tabs["tw_r"], tabs["tw_i"], tabs["m_fwd"], tabs["kp"])
    return ks, ssq


def _hyena(z, tabs, ks, ssq, lp, B, L):
    u, x2c = _hy_pre(z, lp["hy_conv_w"], lp["hy_conv_b"], L)
    kp = tabs["kp"]
    a = _fft1(u, tabs["f_data"], B, tabs["h"])
    xs = _fft2(a, tabs["tw_r"], tabs["tw_i"], tabs["m_fwd"], kp)
    d = _ifft2(xs, ks, ssq, tabs["tw_r"], tabs["tw_i"], tabs["m_inv"])
    return _ifft1(d.reshape(B, 2 * kp, FFT_N2, HY_W), tabs["w_inv"], u, x2c, lp["hy_skip"], B)


def _gla_kernel(qf_ref, kf_ref, vf_ref, lf_ref, qb_ref, kb_ref, vb_ref, lb_ref, gup_ref, gb_ref,
                of_ref, ob_ref, sf_ref, sb_ref, *, nchunk):
    C = GLA_CHUNK
    R = nchunk * C

    @pl.when(pl.program_id(1) == 0)
    def _():
        sf_ref[...] = jnp.zeros_like(sf_ref)
        sb_ref[...] = jnp.zeros_like(sb_ref)

    rr = lax.broadcasted_iota(jnp.int32, (R, R), 0)
    cc = lax.broadcasted_iota(jnp.int32, (R, R), 1)
    same_chunk = (rr // C) == (cc // C)
    lane = lax.broadcasted_iota(jnp.int32, (C, GLA_K), 1) // GLA_DK
    ri4 = lax.broadcasted_iota(jnp.int32, (GLA_HEADS * C, C), 0) % C
    ci4 = lax.broadcasted_iota(jnp.int32, (GLA_HEADS * C, C), 1)

    def stack_heads(x):
        return jnp.concatenate([jnp.where(lane == h, x, 0.0) for h in range(GLA_HEADS)], axis=0).astype(BF16)

    def per_chunk_rows(b, row):
        return jnp.concatenate([jnp.broadcast_to(b[c * C + row:c * C + row + 1], (C, GLA_K)) for c in range(nchunk)],
                               axis=0)

    dirs = []
    for d, (q_ref, k_ref, v_ref, l_ref) in enumerate(((qf_ref, kf_ref, vf_ref, lf_ref),
                                                       (qb_ref, kb_ref, vb_ref, lb_ref))):
        logit = _dot3(l_ref[...], gup_ref[:, d * GLA_K:(d + 1) * GLA_K]) + gb_ref[:, d * GLA_K:(d + 1) * GLA_K]
        la = (jnp.minimum(logit, 0.0) - jnp.log(1.0 + jnp.exp(-jnp.abs(logit)))) * (1.0 / GLA_TAU)
        if d == 0:
            tri = jnp.where(same_chunk & (rr >= cc), 1.0, 0.0).astype(BF16)
            keep = ri4 >= ci4
            mid, last = C // 2 - 1, C - 1
            order = list(range(nchunk))
        else:
            tri = jnp.where(same_chunk & (cc >= rr), 1.0, 0.0).astype(BF16)
            keep = ci4 > ri4
            mid, last = C // 2, 0
            order = list(range(nchunk - 1, -1, -1))
        hi, lo = _split2(la)
        lo2 = (la - hi.astype(F32) - lo.astype(F32)).astype(BF16)
        b = _dot(tri, hi) + _dot(tri, lo) + _dot(tri, lo2)
        b_mid = per_chunk_rows(b, mid)
        b_last = per_chunk_rows(b, last)
        q = q_ref[...] * (GLA_DK ** -0.5)
        k = k_ref[...]
        dirs.append(dict(
            keep=keep, order=order, v=v_ref[...].astype(BF16),
            qt=q * jnp.exp(b - b_mid), kt=(k * jnp.exp(b_mid - b)).astype(BF16),
            kh=k * jnp.exp(b_last - b), qh=q * jnp.exp(b), dec=jnp.exp(b_last)))

    for dd in dirs:
        dd["a"] = []
        for c in range(nchunk):
            rs = slice(c * C, (c + 1) * C)
            a = lax.dot_general(stack_heads(dd["qt"][rs]), dd["kt"][rs], (((1,), (1,)), ((), ())),
                                preferred_element_type=F32)
            dd["a"].append(jnp.where(dd["keep"], a, 0.0).astype(BF16))

    for dd in dirs:
        dd["o"], dd["u"] = [], []
        for c in range(nchunk):
            rs = slice(c * C, (c + 1) * C)
            k_t = jnp.transpose(dd["kh"][rs]).astype(BF16)
            outs, ups = [], []
            for h in range(GLA_HEADS):
                vh = dd["v"][rs, h * GLA_DV:(h + 1) * GLA_DV]
                outs.append(_dot(dd["a"][c][h * C:(h + 1) * C], vh))
                ups.append(_dot(k_t[h * GLA_DK:(h + 1) * GLA_DK], vh))
            dd["o"].append(outs)
            dd["u"].append(jnp.concatenate(ups, axis=0))

    for dd, s_ref in zip(dirs, (sf_ref, sb_ref)):
        s = s_ref[...]
        dd["s_prev"] = {}
        for c in dd["order"]:
            dd["s_prev"][c] = s.astype(BF16)
            decay = jnp.transpose(jnp.broadcast_to(dd["dec"][c * C:c * C + 1], (GLA_DV, GLA_K)))
            s = decay * s + dd["u"][c]
        s_ref[...] = s

    for dd, o_ref in zip(dirs, (of_ref, ob_ref)):
        for c in range(nchunk):
            rs = slice(c * C, (c + 1) * C)
            o_inter = _dot(stack_heads(dd["qh"][rs]), dd["s_prev"][c])
            o_ref[rs, :] = jnp.concatenate(
                [dd["o"][c][h] + o_inter[h * C:(h + 1) * C] for h in range(GLA_HEADS)], axis=1)


def _gla(z, gup, gb, B, L, nchunk=4):
    n = z.shape[0]
    R = nchunk * GLA_CHUNK
    T = L // R
    fwd = lambda b, t: b * T + t
    bwd = lambda b, t: b * T + (T - 1 - t)

    def specs(row):
        return [
            pl.BlockSpec((R, GLA_K), lambda b, t: (row(b, t), Z_GQ // GLA_K)),
            pl.BlockSpec((R, GLA_K), lambda b, t: (row(b, t), Z_GK // GLA_K)),
            pl.BlockSpec((R, GLA_V), lambda b, t: (row(b, t), Z_GV // GLA_V)),
            pl.BlockSpec((R, LANES), lambda b, t: (row(b, t), Z_GLOW // LANES)),
        ]

    return pl.pallas_call(
        functools.partial(_gla_kernel, nchunk=nchunk),
        grid=(B, T),
        in_specs=specs(fwd) + specs(bwd) + [
            pl.BlockSpec(gup.shape, lambda b, t: (0, 0)),
            pl.BlockSpec(gb.shape, lambda b, t: (0, 0)),
        ],
        out_specs=[
            pl.BlockSpec((R, GLA_V), lambda b, t: (fwd(b, t), 0)),
            pl.BlockSpec((R, GLA_V), lambda b, t: (bwd(b, t), 0)),
        ],
        out_shape=[jax.ShapeDtypeStruct((n, GLA_V), F32)] * 2,
        scratch_shapes=[pltpu.VMEM((GLA_K, GLA_DV), F32)] * 2,
        compiler_params=_cparams("parallel", "arbitrary"),
        name="gla",
    )(*([z] * 8), gup, gb)


def _mix_out_kernel(x_ref, ya_ref, yh_ref, of_ref, ob_ref, og_ref, g0_ref, g1_ref, g2_ref,
                    wb_ref, wo_ref, gn_ref, o_ref):
    o = of_ref[...] + ob_ref[...]
    og = og_ref[...]
    gn = gn_ref[...]
    parts = []
    for h in range(GLA_HEADS):
        blk = o[:, h * GLA_DV:(h + 1) * GLA_DV]
        ms = jnp.mean(blk * blk, axis=-1, keepdims=True)
        gate = og[:, h * GLA_DV:(h + 1) * GLA_DV]
        parts.append(blk * lax.rsqrt(ms + EPS) * gn * (gate * jax.nn.sigmoid(gate)))
    y_gla = jnp.concatenate(parts, axis=1)
    proj_att = lax.dot_general(ya_ref[...].astype(BF16), wb_ref[0], (((0,), (0,)), ((), ())),
                               preferred_element_type=F32)
    merged = jax.nn.sigmoid(g0_ref[...]) * proj_att
    merged += jax.nn.sigmoid(g1_ref[...]) * _dot(yh_ref[...].astype(BF16), wb_ref[1])
    merged += jax.nn.sigmoid(g2_ref[...]) * _dot(y_gla.astype(BF16), wb_ref[2])
    o_ref[...] = x_ref[...] + _dot(merged.astype(BF16), wo_ref[...])


def _mix_out(x, y_att, y_hy, o_f, o_b, z, wb, wo, gn, tm=256):
    n = x.shape[0]
    row = lambda w: pl.BlockSpec((tm, w), lambda i: (i, 0))
    return pl.pallas_call(
        _mix_out_kernel,
        grid=(n // tm,),
        in_specs=[
            row(D_MODEL), pl.BlockSpec((MIX_W, tm), lambda i: (0, i)), row(MIX_W), row(MIX_W), row(MIX_W),
            pl.BlockSpec((tm, GLA_V), lambda i: (i, Z_GOG // GLA_V)),
            pl.BlockSpec((tm, D_MODEL), lambda i: (i, 0)),
            pl.BlockSpec((tm, D_MODEL), lambda i: (i, 1)),
            pl.BlockSpec((tm, D_MODEL), lambda i: (i, 2)),
            pl.BlockSpec(wb.shape, lambda i: (0, 0, 0)),
            pl.BlockSpec(wo.shape, lambda i: (0, 0)),
            pl.BlockSpec((1, GLA_DV), lambda i: (0, 0)),
        ],
        out_specs=row(D_MODEL),
        out_shape=jax.ShapeDtypeStruct((n, D_MODEL), F32),
        compiler_params=_cparams("parallel"),
        name="mix_out",
    )(x, y_att, y_hy, o_f, o_b, z, z, z, z, wb, wo, gn)


def _ffn_kernel(x_ref, g_ref, wg_ref, wu_ref, wd_ref, o_ref, h_ref, acc_ref):
    j = pl.program_id(1)

    @pl.when(j == 0)
    def _():
        x = x_ref[...]
        ms = jnp.mean(x * x, axis=-1, keepdims=True)
        h_ref[...] = (x * lax.rsqrt(ms + EPS) * g_ref[...]).astype(BF16)
        acc_ref[...] = jnp.zeros_like(acc_ref)

    h = h_ref[...]
    a = _dot(h, wg_ref[...])
    act = (a * jax.nn.sigmoid(a)) * _dot(h, wu_ref[...])
    acc_ref[...] += _dot(act.astype(BF16), wd_ref[...])

    @pl.when(j == pl.num_programs(1) - 1)
    def _():
        o_ref[...] = x_ref[...] + acc_ref[...]


def _ffn(x, g, wg, wu, wd, tm=512, tf=1408):
    n = x.shape[0]
    return pl.pallas_call(
        _ffn_kernel,
        grid=(n // tm, D_FF // tf),
        in_specs=[
            pl.BlockSpec((tm, D_MODEL), lambda i, j: (i, 0)),
            pl.BlockSpec((1, D_MODEL), lambda i, j: (0, 0)),
            pl.BlockSpec((D_MODEL, tf), lambda i, j: (0, j)),
            pl.BlockSpec((D_MODEL, tf), lambda i, j: (0, j)),
            pl.BlockSpec((tf, D_MODEL), lambda i, j: (j, 0)),
        ],
        out_specs=pl.BlockSpec((tm, D_MODEL), lambda i, j: (i, 0)),
        out_shape=jax.ShapeDtypeStruct((n, D_MODEL), F32),
        scratch_shapes=[pltpu.VMEM((tm, D_MODEL), BF16), pltpu.VMEM((tm, D_MODEL), F32)],
        compiler_params=_cparams("parallel", "arbitrary"),
        name="ffn",
    )(x, g, wg, wu, wd)


def _final_norm_kernel(x_ref, g_ref, o_ref):
    x = x_ref[...]
    ms = jnp.mean(x * x, axis=-1, keepdims=True)
    o_ref[...] = x * lax.rsqrt(ms + EPS) * g_ref[...]


def _final_norm(x, g, tm=512):
    n = x.shape[0]
    return pl.pallas_call(
        _final_norm_kernel,
        grid=(n // tm,),
        in_specs=[pl.BlockSpec((tm, D_MODEL), lambda i: (i, 0)), pl.BlockSpec((1, D_MODEL), lambda i: (0, 0))],
        out_specs=pl.BlockSpec((tm, D_MODEL), lambda i: (i, 0)),
        out_shape=jax.ShapeDtypeStruct((n, D_MODEL), F32),
        compiler_params=_cparams("parallel"),
        name="final_norm",
    )(x, g)


def _rope_tables(L):
    rows = L // GRID_W
    r = jnp.repeat(jnp.arange(rows, dtype=F32), GRID_W)
    c = jnp.tile(jnp.arange(GRID_W, dtype=F32), rows)
    inv = ROPE_THETA ** (-jnp.arange(0, ROPE_AXIS_DIM, 2, dtype=F32) / ROPE_AXIS_DIM)
    ang = jnp.concatenate([r[:, None] * inv, c[:, None] * inv], axis=-1)
    cos, sin = jnp.cos(ang), jnp.sin(ang)
    cos_t = jnp.tile(cos, (1, 2 * LANES // HEAD_DIM))
    sin_t = jnp.tile(jnp.concatenate([-sin, sin], axis=-1), (1, LANES // HEAD_DIM))
    return cos_t, sin_t


def _layer_params(l, norm_mix_g, w_in, q_norm_g, k_norm_g, hy_conv_w, hy_conv_b, hy_w1, hy_b1, hy_f1, hy_w2,
                  hy_b2, hy_f2, hy_w3, hy_skip, gla_gate_up, gla_gate_b, gla_norm_g, w_branch, w_out, norm_ffn_g,
                  w_ffn_gate, w_ffn_up, w_ffn_down):
    w = w_in[l]
    offs = {}
    off = 0
    for name, size in (("aq", ATT_W), ("ak", KV_W), ("av", KV_W), ("hz", 3 * HY_W), ("gq", GLA_K), ("gk", GLA_K),
                       ("gv", GLA_V), ("gog", GLA_V), ("glow", 2 * GLA_RANK), ("gates", N_BRANCH * D_MODEL)):
        offs[name] = w[:, off:off + size]
        off += size
    w_cat = jnp.concatenate(
        [offs["gates"], offs["aq"], offs["gv"], offs["gog"], offs["hz"], offs["gq"], offs["gk"], offs["ak"],
         offs["av"], offs["glow"], jnp.zeros((D_MODEL, Z_COLS - Z_GLOW - 2 * GLA_RANK), F32)], axis=1).astype(BF16)
    gup = jnp.zeros((LANES, 2 * GLA_K), F32)
    gup = gup.at[0:GLA_RANK, 0:GLA_K].set(gla_gate_up[l, 0])
    gup = gup.at[GLA_RANK:2 * GLA_RANK, GLA_K:2 * GLA_K].set(gla_gate_up[l, 1])
    deltas = jnp.linspace(abs(math.log(HY_TARGET) / HY_SLOW), abs(math.log(HY_TARGET) / HY_FAST), HY_W, dtype=F32)
    return dict(
        norm_mix_g=norm_mix_g[l][None, :],
        w_cat=w_cat,
        gq=jnp.tile(q_norm_g[l], LANES // HEAD_DIM)[None, :],
        gk=jnp.tile(k_norm_g[l], LANES // HEAD_DIM)[None, :],
        hy_conv_w=jnp.transpose(hy_conv_w[l].reshape(3, 3, HY_W), (1, 0, 2)),
        hy_conv_b=hy_conv_b[l].reshape(3, HY_W),
        hy_w1p=jnp.pad(hy_w1[l], ((0, LANES - HY_EMB), (0, 0))),
        hy_b1=hy_b1[l][None, :], hy_f1=hy_f1[l][None, :],
        hy_w2=hy_w2[l], hy_b2=hy_b2[l][None, :], hy_f2=hy_f2[l][None, :],
        hy_w3s=jnp.transpose(hy_w3[l].reshape(HY_FILTER_W, 2, HY_W), (1, 0, 2)),
        hy_deltas=deltas[None, :],
        hy_skip=hy_skip[l][None, :],
        gup=gup, gb=gla_gate_b[l].reshape(1, 2 * GLA_K),
        gn=gla_norm_g[l][None, :],
        wb=w_branch[l].astype(BF16), wo=w_out[l].astype(BF16),
        norm_ffn_g=norm_ffn_g[l][None, :],
        wg=w_ffn_gate[l].astype(BF16), wu=w_ffn_up[l].astype(BF16), wd=w_ffn_down[l].astype(BF16),
    )


def _encoder_layer(x, lp, B, L, rope, tabs, ks, ssq):
    z = _in_proj(x, lp["norm_mix_g"], lp["w_cat"])
    qt, kr, vt = _attn_prep(z, rope[0], rope[1], lp["gq"], lp["gk"], L)
    y_att = _flash(qt, kr, vt, B, L)
    y_hy = _hyena(z, tabs, ks, ssq, lp, B, L)
    o_f, o_b = _gla(z, lp["gup"], lp["gb"], B, L)
    x = _mix_out(x, y_att, y_hy, o_f, o_b, z, lp["wb"], lp["wo"], lp["gn"])
    return _ffn(x, lp["norm_ffn_g"], lp["wg"], lp["wu"], lp["wd"])


def kernel(x_prompt, x_sample, norm_mix_g, w_in, q_norm_g, k_norm_g, hy_conv_w, hy_conv_b, hy_w1, hy_b1, hy_f1, hy_w2, hy_b2, hy_f2, hy_w3, hy_skip, gla_gate_up, gla_gate_b, gla_norm_g, w_branch, w_out, norm_ffn_g, w_ffn_gate, w_ffn_up, w_ffn_down, final_norm_g):
    streams = []
    for xin in (x_prompt, x_sample):
        B, L, _ = xin.shape
        streams.append(dict(x=xin.reshape(B * L, D_MODEL), B=B, L=L, rope=_rope_tables(L), tabs=_fft_tables(L),
                            feats=_filter_feats(L)))
    for l in range(DEPTH):
        lp = _layer_params(l, norm_mix_g, w_in, q_norm_g, k_norm_g, hy_conv_w, hy_conv_b, hy_w1, hy_b1, hy_f1,
                           hy_w2, hy_b2, hy_f2, hy_w3, hy_skip, gla_gate_up, gla_gate_b, gla_norm_g, w_branch,
                           w_out, norm_ffn_g, w_ffn_gate, w_ffn_up, w_ffn_down)
        for s in streams:
            ks, ssq = _hyena_spectrum(s["tabs"], s["feats"], lp, s["L"])
            s["x"] = _encoder_layer(s["x"], lp, s["B"], s["L"], s["rope"], s["tabs"], ks, ssq)
    outs = []
    for s, xin in zip(streams, (x_prompt, x_sample)):
        outs.append(_final_norm(s["x"], final_norm_g[None, :]).reshape(xin.shape))
    return tuple(outs)
```

```python
import functools
import math

import jax
import jax.numpy as jnp
from jax import lax
from jax.experimental import pallas as pl
from jax.experimental.pallas import tpu as pltpu

F32 = jnp.float32
BF16 = jnp.bfloat16

D_MODEL = 1024
DEPTH = 4
GRID_W = 64
N_HEADS = 8
N_KV_HEADS = 2
HEAD_DIM = 64
GROUP = N_HEADS // N_KV_HEADS
ROPE_THETA = 10000.0
ROPE_AXIS_DIM = HEAD_DIM // 2
ATT_W = N_HEADS * HEAD_DIM
KV_W = N_KV_HEADS * HEAD_DIM
HY_W = D_MODEL // 2
HY_EMB = 33
HY_BANDS = (HY_EMB - 1) // 2
HY_FILTER_W = 64
HY_TARGET = 1e-2
HY_FAST = 0.3
HY_SLOW = 1.5
GLA_HEADS = 4
GLA_DK = 64
GLA_DV = 128
GLA_K = GLA_HEADS * GLA_DK
GLA_V = GLA_HEADS * GLA_DV
GLA_RANK = 16
GLA_TAU = 16.0
GLA_CHUNK = 64
N_BRANCH = 3
MIX_W = 512
D_FF = 2816
EPS = 1e-6

LANES = 128
VMEM_LIMIT = 56 * 1024 * 1024

Z_GATES = 0
Z_AQ = 3072
Z_GV = 3584
Z_GOG = 4096
Z_HZ = 4608
Z_GQ = 6144
Z_GK = 6400
Z_AK = 6656
Z_AV = 6784
Z_GLOW = 6912
Z_COLS = 7168

FFT_N2 = 128
FFT_KB = 8
FFT_SUB = 8
FFT_JT = 4
FFT_CT = 512


def _cparams(*sem):
    return pltpu.CompilerParams(dimension_semantics=sem, vmem_limit_bytes=VMEM_LIMIT)


def _split2(x):
    hi = x.astype(BF16)
    lo = (x - hi.astype(F32)).astype(BF16)
    return hi, lo


def _dot(a, b):
    return jnp.dot(a, b, preferred_element_type=F32)


def _dot3(a, b):
    ah, al = _split2(a)
    bh, bl = _split2(b)
    return _dot(ah, bh) + _dot(al, bh) + _dot(ah, bl)


def _in_proj_kernel(x_ref, g_ref, w_ref, o_ref, h_ref):
    @pl.when(pl.program_id(1) == 0)
    def _():
        x = x_ref[...]
        ms = jnp.mean(x * x, axis=-1, keepdims=True)
        h_ref[...] = (x * lax.rsqrt(ms + EPS) * g_ref[...]).astype(BF16)

    o_ref[...] = _dot(h_ref[...], w_ref[...])


def _in_proj(x, g, w, tm=1024, tn=1024):
    n = x.shape[0]
    return pl.pallas_call(
        _in_proj_kernel,
        grid=(n // tm, Z_COLS // tn),
        in_specs=[
            pl.BlockSpec((tm, D_MODEL), lambda i, j: (i, 0)),
            pl.BlockSpec((1, D_MODEL), lambda i, j: (0, 0)),
            pl.BlockSpec((D_MODEL, tn), lambda i, j: (0, j)),
        ],
        out_specs=pl.BlockSpec((tm, tn), lambda i, j: (i, j)),
        out_shape=jax.ShapeDtypeStruct((n, Z_COLS), F32),
        scratch_shapes=[pltpu.VMEM((tm, D_MODEL), BF16)],
        compiler_params=_cparams("parallel", "arbitrary"),
        name="in_proj",
    )(x, g, w)


V_ROWS = HEAD_DIM + 16
Q_SCALE = (HEAD_DIM ** -0.5) * math.log2(math.e)


def _attn_prep_kernel(q_ref, k_ref, v_ref, cos_ref, sin_ref, gq_ref, gk_ref, qt_ref, kr_ref, vt_ref):
    tm = q_ref.shape[0]
    cos = cos_ref[...]
    sin = sin_ref[...]
    lane = lax.broadcasted_iota(jnp.int32, (tm, LANES), 1)
    first_half = (lane % HEAD_DIM) < (HEAD_DIM // 2)
    left = lane < HEAD_DIM
    r = lax.broadcasted_iota(jnp.int32, (LANES, LANES), 0) // HEAD_DIM
    c = lax.broadcasted_iota(jnp.int32, (LANES, LANES), 1) // HEAD_DIM
    head_ones = jnp.where(r == c, 1.0, 0.0).astype(BF16)

    def norm_rope(x, g):
        xsq = x * x
        hi, lo = _split2(xsq)
        lo2 = (xsq - hi.astype(F32) - lo.astype(F32)).astype(BF16)
        ssq = _dot(hi, head_ones) + _dot(lo, head_ones) + _dot(lo2, head_ones)
        y = x * lax.rsqrt(ssq * (1.0 / HEAD_DIM) + EPS) * g
        other = jnp.where(first_half, pltpu.roll(y, LANES - HEAD_DIM // 2, 1), pltpu.roll(y, HEAD_DIM // 2, 1))
        return y * cos + other * sin

    gq = gq_ref[...]
    for jb in range(ATT_W // LANES):
        y = norm_rope(q_ref[:, jb * LANES:(jb + 1) * LANES], gq) * Q_SCALE
        ys = pltpu.roll(y, HEAD_DIM, 1)
        if (2 * jb) // GROUP == 0:
            e0 = jnp.where(left, y, 0.0)
            e1 = jnp.where(left, ys, 0.0)
        else:
            e0 = jnp.where(left, 0.0, ys)
            e1 = jnp.where(left, 0.0, y)
        qt_ref[2 * jb] = jnp.transpose(e0).astype(BF16)
        qt_ref[2 * jb + 1] = jnp.transpose(e1).astype(BF16)
    kr_ref[...] = norm_rope(k_ref[...], gk_ref[...]).astype(BF16)
    vt = jnp.transpose(v_ref[...])
    ones = jnp.ones((V_ROWS - HEAD_DIM, tm), BF16)
    for kv in range(N_KV_HEADS):
        vt_ref[kv, 0:HEAD_DIM, :] = vt[kv * HEAD_DIM:(kv + 1) * HEAD_DIM].astype(BF16)
        vt_ref[kv, HEAD_DIM:V_ROWS, :] = ones


def _attn_prep(z, cos_t, sin_t, gq, gk, L, tm=512):
    n = z.shape[0]
    lt = L // tm
    return pl.pallas_call(
        _attn_prep_kernel,
        grid=(n // tm,),
        in_specs=[
            pl.BlockSpec((tm, ATT_W), lambda i: (i, Z_AQ // ATT_W)),
            pl.BlockSpec((tm, KV_W), lambda i: (i, Z_AK // KV_W)),
            pl.BlockSpec((tm, KV_W), lambda i: (i, Z_AV // KV_W)),
            pl.BlockSpec((tm, LANES), lambda i: (i % lt, 0)),
            pl.BlockSpec((tm, LANES), lambda i: (i % lt, 0)),
            pl.BlockSpec((1, LANES), lambda i: (0, 0)),
            pl.BlockSpec((1, LANES), lambda i: (0, 0)),
        ],
        out_specs=[
            pl.BlockSpec((N_HEADS, LANES, tm), lambda i: (0, 0, i)),
            pl.BlockSpec((tm, KV_W), lambda i: (i, 0)),
            pl.BlockSpec((N_KV_HEADS, V_ROWS, tm), lambda i: (0, 0, i)),
        ],
        out_shape=[
            jax.ShapeDtypeStruct((N_HEADS, LANES, n), BF16),
            jax.ShapeDtypeStruct((n, KV_W), BF16),
            jax.ShapeDtypeStruct((N_KV_HEADS, V_ROWS, n), BF16),
        ],
        compiler_params=_cparams("parallel"),
        name="attn_prep",
    )(z, z, z, cos_t, sin_t, gq, gk)


def _flash_kernel(qt_ref, k_ref, vt_ref, o_ref, s_ref, p_ref, *, tk):
    tq = qt_ref.shape[2]
    L = k_ref.shape[0]
    nt = L // tk
    heads = range(N_HEADS)

    def scores(j, slot):
        kj = k_ref[pl.ds(pl.multiple_of(j * tk, tk), tk), :]
        for h in heads:
            s_ref[slot, h] = _dot(kj, qt_ref[h])

    def values(j, slot, alphas, accs):
        start = pl.multiple_of(j * tk, tk)
        vjs = [vt_ref[kv, :, pl.ds(start, tk)] for kv in range(N_KV_HEADS)]
        return [alphas[h] * accs[h] + _dot(vjs[h // GROUP], p_ref[slot, h]) for h in heads]

    scores(0, 0)
    p_ref[1] = jnp.zeros(p_ref.shape[1:], BF16)

    def half_step(jv, js, other, cur, ms, alphas, accs):
        kj = k_ref[pl.ds(pl.multiple_of(js * tk, tk), tk), :]
        start = pl.multiple_of(jv * tk, tk)
        vjs = [vt_ref[kv, :, pl.ds(start, tk)] for kv in range(N_KV_HEADS)]
        ms, alphas, accs = list(ms), list(alphas), list(accs)
        for h in heads:
            accs[h] = alphas[h] * accs[h] + _dot(vjs[h // GROUP], p_ref[other, h])
            s = s_ref[cur, h]
            m_new = jnp.maximum(ms[h], jnp.max(s, axis=0, keepdims=True))
            alphas[h] = jnp.exp2(ms[h] - m_new)
            ms[h] = m_new
            p_ref[cur, h] = jnp.exp2(s - m_new).astype(BF16)
            s_ref[other, h] = _dot(kj, qt_ref[h])
        return ms, alphas, accs

    def body(jj, carry):
        ms, alphas, accs = carry
        j = 2 * jj
        ms, alphas, accs = half_step(jnp.maximum(j - 1, 0), j + 1, 1, 0, ms, alphas, accs)
        return half_step(j, jnp.minimum(j + 2, nt - 1), 0, 1, ms, alphas, accs)

    init = ([jnp.full((1, tq), -jnp.inf, F32)] * N_HEADS, [jnp.ones((1, tq), F32)] * N_HEADS,
            [jnp.zeros((V_ROWS, tq), F32)] * N_HEADS)
    _, alphas, accs = lax.fori_loop(0, nt // 2, body, init)
    accs = values(nt - 1, 1, alphas, accs)
    for h in heads:
        o_ref[h * HEAD_DIM:(h + 1) * HEAD_DIM, :] = accs[h][0:HEAD_DIM] / accs[h][HEAD_DIM:HEAD_DIM + 1]


def _flash(qt, kr, vt, B, L, tq=256, tk=512):
    n = kr.shape[0]
    lt = L // tq
    assert (L // tk) % 2 == 0
    return pl.pallas_call(
        functools.partial(_flash_kernel, tk=tk),
        grid=(B, lt),
        in_specs=[
            pl.BlockSpec((N_HEADS, LANES, tq), lambda b, i: (0, 0, b * lt + i)),
            pl.BlockSpec((L, KV_W), lambda b, i: (b, 0)),
            pl.BlockSpec((N_KV_HEADS, V_ROWS, L), lambda b, i: (0, 0, b)),
        ],
        out_specs=pl.BlockSpec((ATT_W, tq), lambda b, i: (0, b * lt + i)),
        out_shape=jax.ShapeDtypeStruct((ATT_W, n), F32),
        scratch_shapes=[pltpu.VMEM((2, N_HEADS, tk, tq), F32), pltpu.VMEM((2, N_HEADS, tk, tq), BF16)],
        compiler_params=_cparams("parallel", "parallel"),
        name="flash",
    )(qt, kr, vt)


def _hy_pre_kernel(v_ref, vp_ref, vn_ref, a_ref, ap_ref, an_ref, b_ref, bp_ref, bn_ref, w_ref, cb_ref,
                   u_ref, x2_ref, *, lt):
    tm = v_ref.shape[0]
    i = pl.program_id(0)
    first = (i % lt) == 0
    last = (i % lt) == lt - 1
    rows = lax.broadcasted_iota(jnp.int32, v_ref.shape, 0)

    def conv3(m_ref, p_ref, n_ref, part):
        x = m_ref[...]
        prev = jnp.where(first, 0.0, p_ref[7:8, :])
        nxt = jnp.where(last, 0.0, n_ref[0:1, :])
        dn = jnp.where(rows == 0, prev, pltpu.roll(x, 1, 0))
        up = jnp.where(rows == tm - 1, nxt, pltpu.roll(x, tm - 1, 0))
        w = w_ref[part]
        return dn * w[0:1] + x * w[1:2] + up * w[2:3] + cb_ref[part:part + 1, :]

    u_ref[...] = conv3(v_ref, vp_ref, vn_ref, 0) * conv3(a_ref, ap_ref, an_ref, 1)
    x2_ref[...] = conv3(b_ref, bp_ref, bn_ref, 2)


def _hy_pre(z, conv_w, conv_b, L, tm=512, tc=256):
    n = z.shape[0]
    lt = L // tm
    r8 = tm // 8
    nb8 = n // 8
    specs = []
    for part in range(3):
        c0 = (Z_HZ + part * HY_W) // tc
        specs.append(pl.BlockSpec((tm, tc), lambda i, c, c0=c0: (i, c0 + c)))
        specs.append(pl.BlockSpec((8, tc), lambda i, c, c0=c0: (jnp.maximum(i * r8 - 1, 0), c0 + c)))
        specs.append(pl.BlockSpec((8, tc), lambda i, c, c0=c0: (jnp.minimum((i + 1) * r8, nb8 - 1), c0 + c)))
    specs.append(pl.BlockSpec((3, 3, tc), lambda i, c: (0, 0, c)))
    specs.append(pl.BlockSpec((3, tc), lambda i, c: (0, c)))
    return pl.pallas_call(
        functools.partial(_hy_pre_kernel, lt=lt),
        grid=(n // tm, HY_W // tc),
        in_specs=specs,
        out_specs=[pl.BlockSpec((tm, tc), lambda i, c: (i, c))] * 2,
        out_shape=[jax.ShapeDtypeStruct((n, HY_W), F32)] * 2,
        compiler_params=_cparams("parallel", "parallel"),
        name="hy_pre",
    )(*([z] * 9), conv_w, conv_b)


def _hy_filt_kernel(f_ref, w1_ref, b1_ref, f1_ref, w2_ref, b2_ref, f2_ref, w3_ref, dl_ref, kf_ref, ssq_ref):
    feats = f_ref[...]
    h = jnp.sin(f1_ref[...] * (_dot3(feats, w1_ref[...]) + b1_ref[...]))
    h = jnp.sin(f2_ref[...] * (_dot3(h, w2_ref[...]) + b2_ref[...]))
    h = _dot3(h, w3_ref[...])
    t = feats[:, 0:1]
    valid = feats[:, HY_EMB:HY_EMB + 1]
    kf = h * jnp.exp(-t * dl_ref[...]) * valid
    kf_ref[...] = kf

    @pl.when(pl.program_id(0) == 0)
    def _():
        ssq_ref[...] = jnp.zeros_like(ssq_ref)

    ssq_ref[...] += jnp.sum(kf * kf, axis=0, keepdims=True)


def _hy_filt(feats_ext, w1p, b1, f1, w2, b2, f2, w3s, deltas, L, tr=512):
    lt = L // tr
    full = lambda a: pl.BlockSpec(a.shape, lambda i: (0,) * a.ndim)
    return pl.pallas_call(
        _hy_filt_kernel,
        grid=(2 * lt,),
        in_specs=[
            pl.BlockSpec((tr, LANES), lambda i: (i, 0)),
            full(w1p), full(b1), full(f1), full(w2), full(b2), full(f2),
            pl.BlockSpec((None, HY_FILTER_W, HY_W), lambda i: (i // lt, 0, 0)),
            full(deltas),
        ],
        out_specs=[pl.BlockSpec((tr, HY_W), lambda i: (i, 0)), pl.BlockSpec((1, HY_W), lambda i: (0, 0))],
        out_shape=[jax.ShapeDtypeStruct((2 * L, HY_W), F32), jax.ShapeDtypeStruct((1, HY_W), F32)],
        compiler_params=_cparams("arbitrary"),
        name="hy_filt",
    )(feats_ext, w1p, b1, f1, w2, b2, f2, w3s, deltas)


def _fft1_kernel(x_ref, f_ref, o_ref):
    hin, rows = x_ref.shape[0], o_ref.shape[0]
    npos = x_ref.shape[1] * x_ref.shape[2]
    x2 = x_ref.reshape(hin * npos, LANES)
    o2 = o_ref.reshape(rows * npos, LANES)
    f = f_ref[...]
    for p in range(0, npos, 2):
        x = jnp.concatenate([x2[pl.ds(p, hin, stride=npos), :], x2[pl.ds(p + 1, hin, stride=npos), :]], axis=1)
        xh, xl = _split2(x)
        y = _dot(f, jnp.concatenate([xh, xl, xh], axis=0))
        o2[pl.ds(p, rows, stride=npos), :] = y[:, :LANES]
        o2[pl.ds(p + 1, rows, stride=npos), :] = y[:, LANES:]


def _fft1(x, fcat, B, hin):
    C = x.shape[-1]
    rows = fcat.shape[0]
    nj = FFT_N2 // FFT_SUB
    out = pl.pallas_call(
        _fft1_kernel,
        grid=(B, nj // FFT_JT, C // LANES),
        in_specs=[
            pl.BlockSpec((None, hin, FFT_JT, FFT_SUB, LANES), lambda b, j, c: (b, 0, j, 0, c)),
            pl.BlockSpec(fcat.shape, lambda b, j, c: (0, 0)),
        ],
        out_specs=pl.BlockSpec((None, rows, FFT_JT, FFT_SUB, LANES), lambda b, j, c: (b, 0, j, 0, c)),
        out_shape=jax.ShapeDtypeStruct((B, rows, nj, FFT_SUB, C), F32),
        compiler_params=_cparams("parallel", "parallel", "parallel"),
        name="fft1",
    )(x.reshape(B, hin, nj, FFT_SUB, C), fcat)
    return out.reshape(B, rows, FFT_N2, C)


def _fft2_kernel(ar_ref, ai_ref, tr_ref, ti_ref, m_ref, o_ref):
    m = m_ref[...]
    for r in range(FFT_KB):
        ar, ai = ar_ref[r], ai_ref[r]
        tr, ti = (jnp.tile(t[r], (1, FFT_CT // LANES)) for t in (tr_ref, ti_ref))
        b = jnp.concatenate([ar * tr - ai * ti, ar * ti + ai * tr], axis=0)
        bh, bl = _split2(b)
        o_ref[r] = _dot(m, jnp.concatenate([bh, bl, bh], axis=0))


def _fft2(a, tw_r, tw_i, mcat, kp):
    B = a.shape[0]
    C = a.shape[-1]
    nk = kp // FFT_KB
    return pl.pallas_call(
        _fft2_kernel,
        grid=(B, C // FFT_CT, nk),
        in_specs=[
            pl.BlockSpec((None, FFT_KB, FFT_N2, FFT_CT), lambda b, c, k: (b, k, 0, c)),
            pl.BlockSpec((None, FFT_KB, FFT_N2, FFT_CT), lambda b, c, k: (b, nk + k, 0, c)),
            pl.BlockSpec((FFT_KB, FFT_N2, LANES), lambda b, c, k: (k, 0, 0)),
            pl.BlockSpec((FFT_KB, FFT_N2, LANES), lambda b, c, k: (k, 0, 0)),
            pl.BlockSpec(mcat.shape, lambda b, c, k: (0, 0)),
        ],
        out_specs=pl.BlockSpec((None, FFT_KB, 2 * FFT_N2, FFT_CT), lambda b, c, k: (b, k, 0, c)),
        out_shape=jax.ShapeDtypeStruct((B, kp, 2 * FFT_N2, C), F32),
        compiler_params=_cparams("parallel", "parallel", "parallel"),
        name="fft2",
    )(a, a, tw_r, tw_i, mcat)


def _ifft2_kernel(x_ref, k_ref, ssq_ref, tr_ref, ti_ref, m_ref, o_ref):
    m = m_ref[...]
    scale = lax.rsqrt(ssq_ref[...] + EPS)
    n2 = FFT_N2
    for r in range(FFT_KB):
        x = x_ref[r]
        kf = k_ref[r]
        xr, xi = x[:n2], x[n2:]
        kr, ki = kf[:n2] * scale, kf[n2:] * scale
        p = jnp.concatenate([xr * kr - xi * ki, xr * ki + xi * kr], axis=0)
        ph, plo = _split2(p)
        c = _dot(m, jnp.concatenate([ph, plo, ph], axis=0))
        cr, ci = c[:n2], c[n2:]
        tr, ti = (jnp.tile(t[r], (1, FFT_CT // LANES)) for t in (tr_ref, ti_ref))
        o_ref[0, r] = cr * tr + ci * ti
        o_ref[1, r] = ci * tr - cr * ti


def _ifft2(xs, ks, ssq, tw_r, tw_i, mcat_inv):
    B, kp, _, C = xs.shape
    nk = kp // FFT_KB
    return pl.pallas_call(
        _ifft2_kernel,
        grid=(B, C // FFT_CT, nk),
        in_specs=[
            pl.BlockSpec((None, FFT_KB, 2 * FFT_N2, FFT_CT), lambda b, c, k: (b, k, 0, c)),
            pl.BlockSpec((None, FFT_KB, 2 * FFT_N2, FFT_CT), lambda b, c, k: (0, k, 0, c)),
            pl.BlockSpec((1, FFT_CT), lambda b, c, k: (0, c)),
            pl.BlockSpec((FFT_KB, FFT_N2, LANES), lambda b, c, k: (k, 0, 0)),
            pl.BlockSpec((FFT_KB, FFT_N2, LANES), lambda b, c, k: (k, 0, 0)),
            pl.BlockSpec(mcat_inv.shape, lambda b, c, k: (0, 0)),
        ],
        out_specs=pl.BlockSpec((None, 2, FFT_KB, FFT_N2, FFT_CT), lambda b, c, k: (b, 0, k, 0, c)),
        out_shape=jax.ShapeDtypeStruct((B, 2, kp, FFT_N2, C), F32),
        compiler_params=_cparams("parallel", "parallel", "parallel"),
        name="ifft2",
    )(xs, ks, ssq, tw_r, tw_i, mcat_inv)


def _ifft1_kernel(d_ref, w_ref, u_ref, x2_ref, skip_ref, o_ref):
    rows, h = d_ref.shape[0], o_ref.shape[0]
    npos = d_ref.shape[1] * d_ref.shape[2]
    d2 = d_ref.reshape(rows * npos, LANES)
    u2, x22, o2 = (r.reshape(h * npos, LANES) for r in (u_ref, x2_ref, o_ref))
    w = w_ref[...]
    skip = skip_ref[...]
    for p in range(0, npos, 2):
        d = jnp.concatenate([d2[pl.ds(p, rows, stride=npos), :], d2[pl.ds(p + 1, rows, stride=npos), :]], axis=1)
        dh, dl = _split2(d)
        y = _dot(w, jnp.concatenate([dh, dl, dh], axis=0))
        for i in range(2):
            sl = pl.ds(p + i, h, stride=npos)
            o2[sl, :] = x22[sl, :] * (y[:, i * LANES:(i + 1) * LANES] + u2[sl, :] * skip)


def _ifft1(d, wcat, u, x2c, skip, B):
    rows, C = d.shape[1], d.shape[-1]
    h = wcat.shape[0]
    nj = FFT_N2 // FFT_SUB
    seq = pl.BlockSpec((None, h, FFT_JT, FFT_SUB, LANES), lambda b, j, c: (b, 0, j, 0, c))
    out = pl.pallas_call(
        _ifft1_kernel,
        grid=(B, nj // FFT_JT, C // LANES),
        in_specs=[
            pl.BlockSpec((None, rows, FFT_JT, FFT_SUB, LANES), lambda b, j, c: (b, 0, j, 0, c)),
            pl.BlockSpec(wcat.shape, lambda b, j, c: (0, 0)),
            seq, seq,
            pl.BlockSpec((1, LANES), lambda b, j, c: (0, c)),
        ],
        out_specs=seq,
        out_shape=jax.ShapeDtypeStruct((B, h, nj, FFT_SUB, C), F32),
        compiler_params=_cparams("parallel", "parallel", "parallel"),
        name="ifft1",
    )(d.reshape(B, rows, nj, FFT_SUB, C), wcat, u.reshape(B, h, nj, FFT_SUB, C), x2c.reshape(B, h, nj, FFT_SUB, C),
      skip)
    return out.reshape(B * h * FFT_N2, C)


def _hilo_cat(m):
    hi = m.astype(BF16)
    lo = (m - hi.astype(F32)).astype(BF16)
    return jnp.concatenate([hi, hi, lo], axis=1)


def _fft_tables(L):
    n = 2 * L
    n2 = FFT_N2
    n1 = n // n2
    h = n1 // 2
    kp = h + FFT_KB
    k1 = jnp.arange(kp, dtype=jnp.int32)

    def outer(hin):
        nn = jnp.arange(hin, dtype=jnp.int32)
        ang = (2.0 * math.pi / n1) * ((k1[:, None] * nn[None, :]) % n1).astype(F32)
        return _hilo_cat(jnp.concatenate([jnp.cos(ang), -jnp.sin(ang)], axis=0))

    f_data = outer(h)
    f_filt = outer(n1)
    j = jnp.arange(n2, dtype=jnp.int32)
    ang2 = (2.0 * math.pi / n2) * ((j[:, None] * j[None, :]) % n2).astype(F32)
    c2, s2 = jnp.cos(ang2), jnp.sin(ang2)
    m_fwd = _hilo_cat(jnp.block([[c2, s2], [-s2, c2]]))
    m_inv = _hilo_cat(jnp.block([[c2, -s2], [s2, c2]]))
    angt = (2.0 * math.pi / n) * (k1[:, None] * j[None, :]).astype(F32)
    tw_r = jnp.broadcast_to(jnp.cos(angt)[:, :, None], (kp, n2, LANES))
    tw_i = jnp.broadcast_to(-jnp.sin(angt)[:, :, None], (kp, n2, LANES))
    wgt = jnp.where((k1 == 0) | (k1 == h), 1.0, jnp.where(k1 < h, 2.0, 0.0)).astype(F32) / n
    nn = jnp.arange(h, dtype=jnp.int32)
    angi = (2.0 * math.pi / n1) * ((nn[:, None] * k1[None, :]) % n1).astype(F32)
    w_inv = _hilo_cat(jnp.concatenate([jnp.cos(angi) * wgt, -jnp.sin(angi) * wgt], axis=1))
    return dict(n1=n1, h=h, kp=kp, f_data=f_data, f_filt=f_filt, m_fwd=m_fwd, m_inv=m_inv,
                tw_r=tw_r, tw_i=tw_i, w_inv=w_inv)


def _filter_feats(L):
    t = jnp.linspace(0.0, 1.0, L, dtype=F32)[:, None]
    w = 2.0 * math.pi * jnp.arange(L, dtype=F32) / L
    fb = jnp.linspace(1e-4, HY_BANDS - 1, HY_BANDS, dtype=F32)
    ph = w[:, None] * fb
    feats = jnp.concatenate([t, jnp.cos(ph), -jnp.sin(ph), jnp.ones((L, 1), F32)], axis=-1)
    back = jnp.concatenate([jnp.zeros((1, HY_EMB + 1), F32), jnp.flip(feats[1:], axis=0)], axis=0)
    ext = jnp.concatenate([feats, back], axis=0)
    return jnp.pad(ext, ((0, 0), (0, LANES - HY_EMB - 1)))


def _hyena_spectrum(tabs, feats_ext, lp, L):
    kf, ssq = _hy_filt(feats_ext, lp["hy_w1p"], lp["hy_b1"], lp["hy_f1"], lp["hy_w2"], lp["hy_b2"], lp["hy_f2"],
                       lp["hy_w3s"], lp["hy_deltas"], L)
    a = _fft1(kf, tabs["f_filt"], 1, tabs["n1"])
    ks = _fft2(a, tabs["tw_r"], tabs["tw_i"], tabs["m_fwd"], tabs["kp"])
    return ks, ssq


def _hyena(z, tabs, ks, ssq, lp, B, L):
    u, x2c = _hy_pre(z, lp["hy_conv_w"], lp["hy_conv_b"], L)
    kp = tabs["kp"]
    a = _fft1(u, tabs["f_data"], B, tabs["h"])
    xs = _fft2(a, tabs["tw_r"], tabs["tw_i"], tabs["m_fwd"], kp)
    d = _ifft2(xs, ks, ssq, tabs["tw_r"], tabs["tw_i"], tabs["m_inv"])
    return _ifft1(d.reshape(B, 2 * kp, FFT_N2, HY_W), tabs["w_inv"], u, x2c, lp["hy_skip"], B)


def _gla_kernel(qf_ref, kf_ref, vf_ref, lf_ref, qb_ref, kb_ref, vb_ref, lb_ref, gup_ref, gb_ref,
                of_ref, ob_ref, sf_ref, sb_ref, *, nchunk):
    C = GLA_CHUNK
    R = nchunk * C

    @pl.when(pl.program_id(1) == 0)
    def _():
        sf_ref[...] = jnp.zeros_like(sf_ref)
        sb_ref[...] = jnp.zeros_like(sb_ref)

    rr = lax.broadcasted_iota(jnp.int32, (R, R), 0)
    cc = lax.broadcasted_iota(jnp.int32, (R, R), 1)
    same_chunk = (rr // C) == (cc // C)
    lane = lax.broadcasted_iota(jnp.int32, (C, GLA_K), 1) // GLA_DK
    ri4 = lax.broadcasted_iota(jnp.int32, (GLA_HEADS * C, C), 0) % C
    ci4 = lax.broadcasted_iota(jnp.int32, (GLA_HEADS * C, C), 1)

    def stack_heads(x):
        return jnp.concatenate([jnp.where(lane == h, x, 0.0) for h in range(GLA_HEADS)], axis=0).astype(BF16)

    def per_chunk_rows(b, row):
        return jnp.concatenate([jnp.broadcast_to(b[c * C + row:c * C + row + 1], (C, GLA_K)) for c in range(nchunk)],
                               axis=0)

    dirs = []
    for d, (q_ref, k_ref, v_ref, l_ref) in enumerate(((qf_ref, kf_ref, vf_ref, lf_ref),
                                                       (qb_ref, kb_ref, vb_ref, lb_ref))):
        logit = _dot3(l_ref[...], gup_ref[:, d * GLA_K:(d + 1) * GLA_K]) + gb_ref[:, d * GLA_K:(d + 1) * GLA_K]
        la = (jnp.minimum(logit, 0.0) - jnp.log(1.0 + jnp.exp(-jnp.abs(logit)))) * (1.0 / GLA_TAU)
        if d == 0:
            tri = jnp.where(same_chunk & (rr >= cc), 1.0, 0.0).astype(BF16)
            keep = ri4 >= ci4
            mid, last = C // 2 - 1, C - 1
            order = list(range(nchunk))
        else:
            tri = jnp.where(same_chunk & (cc >= rr), 1.0, 0.0).astype(BF16)
            keep = ci4 > ri4
            mid, last = C // 2, 0
            order = list(range(nchunk - 1, -1, -1))
        hi, lo = _split2(la)
        lo2 = (la - hi.astype(F32) - lo.astype(F32)).astype(BF16)
        b = _dot(tri, hi) + _dot(tri, lo) + _dot(tri, lo2)
        b_mid = per_chunk_rows(b, mid)
        b_last = per_chunk_rows(b, last)
        q = q_ref[...] * (GLA_DK ** -0.5)
        k = k_ref[...]
        dirs.append(dict(
            keep=keep, order=order, v=v_ref[...].astype(BF16),
            qt=q * jnp.exp(b - b_mid), kt=(k * jnp.exp(b_mid - b)).astype(BF16),
            kh=k * jnp.exp(b_last - b), qh=q * jnp.exp(b), dec=jnp.exp(b_last)))

    for dd in dirs:
        dd["a"] = []
        for c in range(nchunk):
            rs = slice(c * C, (c + 1) * C)
            a = lax.dot_general(stack_heads(dd["qt"][rs]), dd["kt"][rs], (((1,), (1,)), ((), ())),
                                preferred_element_type=F32)
            dd["a"].append(jnp.where(dd["keep"], a, 0.0).astype(BF16))

    for dd in dirs:
        dd["o"], dd["u"] = [], []
        for c in range(nchunk):
            rs = slice(c * C, (c + 1) * C)
            k_t = jnp.transpose(dd["kh"][rs]).astype(BF16)
            outs, ups = [], []
            for h in range(GLA_HEADS):
                vh = dd["v"][rs, h * GLA_DV:(h + 1) * GLA_DV]
                outs.append(_dot(dd["a"][c][h * C:(h + 1) * C], vh))
                ups.append(_dot(k_t[h * GLA_DK:(h + 1) * GLA_DK], vh))
            dd["o"].append(outs)
            dd["u"].append(jnp.concatenate(ups, axis=0))

    for dd, s_ref in zip(dirs, (sf_ref, sb_ref)):
        s = s_ref[...]
        dd["s_prev"] = {}
        for c in dd["order"]:
            dd["s_prev"][c] = s.astype(BF16)
            decay = jnp.transpose(jnp.broadcast_to(dd["dec"][c * C:c * C + 1], (GLA_DV, GLA_K)))
            s = decay * s + dd["u"][c]
        s_ref[...] = s

    for dd, o_ref in zip(dirs, (of_ref, ob_ref)):
        for c in range(nchunk):
            rs = slice(c * C, (c + 1) * C)
            o_inter = _dot(stack_heads(dd["qh"][rs]), dd["s_prev"][c])
            o_ref[rs, :] = jnp.concatenate(
                [dd["o"][c][h] + o_inter[h * C:(h + 1) * C] for h in range(GLA_HEADS)], axis=1)


def _gla(z, gup, gb, B, L, nchunk=4):
    n = z.shape[0]
    R = nchunk * GLA_CHUNK
    T = L // R
    fwd = lambda b, t: b * T + t
    bwd = lambda b, t: b * T + (T - 1 - t)

    def specs(row):
        return [
            pl.BlockSpec((R, GLA_K), lambda b, t: (row(b, t), Z_GQ // GLA_K)),
            pl.BlockSpec((R, GLA_K), lambda b, t: (row(b, t), Z_GK // GLA_K)),
            pl.BlockSpec((R, GLA_V), lambda b, t: (row(b, t), Z_GV // GLA_V)),
            pl.BlockSpec((R, LANES), lambda b, t: (row(b, t), Z_GLOW // LANES)),
        ]

    return pl.pallas_call(
        functools.partial(_gla_kernel, nchunk=nchunk),
        grid=(B, T),
        in_specs=specs(fwd) + specs(bwd) + [
            pl.BlockSpec(gup.shape, lambda b, t: (0, 0)),
            pl.BlockSpec(gb.shape, lambda b, t: (0, 0)),
        ],
        out_specs=[
            pl.BlockSpec((R, GLA_V), lambda b, t: (fwd(b, t), 0)),
            pl.BlockSpec((R, GLA_V), lambda b, t: (bwd(b, t), 0)),
        ],
        out_shape=[jax.ShapeDtypeStruct((n, GLA_V), F32)] * 2,
        scratch_shapes=[pltpu.VMEM((GLA_K, GLA_DV), F32)] * 2,
        compiler_params=_cparams("parallel", "arbitrary"),
        name="gla",
    )(*([z] * 8), gup, gb)


def _mix_out_kernel(x_ref, ya_ref, yh_ref, of_ref, ob_ref, og_ref, g0_ref, g1_ref, g2_ref,
                    wb_ref, wo_ref, gn_ref, o_ref):
    o = of_ref[...] + ob_ref[...]
    og = og_ref[...]
    gn = gn_ref[...]
    parts = []
    for h in range(GLA_HEADS):
        blk = o[:, h * GLA_DV:(h + 1) * GLA_DV]
        ms = jnp.mean(blk * blk, axis=-1, keepdims=True)
        gate = og[:, h * GLA_DV:(h + 1) * GLA_DV]
        parts.append(blk * lax.rsqrt(ms + EPS) * gn * (gate * jax.nn.sigmoid(gate)))
    y_gla = jnp.concatenate(parts, axis=1)
    proj_att = lax.dot_general(ya_ref[...].astype(BF16), wb_ref[0], (((0,), (0,)), ((), ())),
                               preferred_element_type=F32)
    merged = jax.nn.sigmoid(g0_ref[...]) * proj_att
    merged += jax.nn.sigmoid(g1_ref[...]) * _dot(yh_ref[...].astype(BF16), wb_ref[1])
    merged += jax.nn.sigmoid(g2_ref[...]) * _dot(y_gla.astype(BF16), wb_ref[2])
    o_ref[...] = x_ref[...] + _dot(merged.astype(BF16), wo_ref[...])


def _mix_out(x, y_att, y_hy, o_f, o_b, z, wb, wo, gn, tm=256):
    n = x.shape[0]
    row = lambda w: pl.BlockSpec((tm, w), lambda i: (i, 0))
    return pl.pallas_call(
        _mix_out_kernel,
        grid=(n // tm,),
        in_specs=[
            row(D_MODEL), pl.BlockSpec((MIX_W, tm), lambda i: (0, i)), row(MIX_W), row(MIX_W), row(MIX_W),
            pl.BlockSpec((tm, GLA_V), lambda i: (i, Z_GOG // GLA_V)),
            pl.BlockSpec((tm, D_MODEL), lambda i: (i, 0)),
            pl.BlockSpec((tm, D_MODEL), lambda i: (i, 1)),
            pl.BlockSpec((tm, D_MODEL), lambda i: (i, 2)),
            pl.BlockSpec(wb.shape, lambda i: (0, 0, 0)),
            pl.BlockSpec(wo.shape, lambda i: (0, 0)),
            pl.BlockSpec((1, GLA_DV), lambda i: (0, 0)),
        ],
        out_specs=row(D_MODEL),
        out_shape=jax.ShapeDtypeStruct((n, D_MODEL), F32),
        compiler_params=_cparams("parallel"),
        name="mix_out",
    )(x, y_att, y_hy, o_f, o_b, z, z, z, z, wb, wo, gn)


def _ffn_kernel(x_ref, g_ref, wg_ref, wu_ref, wd_ref, o_ref, h_ref, acc_ref):
    j = pl.program_id(1)

    @pl.when(j == 0)
    def _():
        x = x_ref[...]
        ms = jnp.mean(x * x, axis=-1, keepdims=True)
        h_ref[...] = (x * lax.rsqrt(ms + EPS) * g_ref[...]).astype(BF16)
        acc_ref[...] = jnp.zeros_like(acc_ref)

    h = h_ref[...]
    a = _dot(h, wg_ref[...])
    act = (a * jax.nn.sigmoid(a)) * _dot(h, wu_ref[...])
    acc_ref[...] += _dot(act.astype(BF16), wd_ref[...])

    @pl.when(j == pl.num_programs(1) - 1)
    def _():
        o_ref[...] = x_ref[...] + acc_ref[...]


def _ffn(x, g, wg, wu, wd, tm=512, tf=1408):
    n = x.shape[0]
    return pl.pallas_call(
        _ffn_kernel,
        grid=(n // tm, D_FF // tf),
        in_specs=[
            pl.BlockSpec((tm, D_MODEL), lambda i, j: (i, 0)),
            pl.BlockSpec((1, D_MODEL), lambda i, j: (0, 0)),
            pl.BlockSpec((D_MODEL, tf), lambda i, j: (0, j)),
            pl.BlockSpec((D_MODEL, tf), lambda i, j: (0, j)),
            pl.BlockSpec((tf, D_MODEL), lambda i, j: (j, 0)),
        ],
        out_specs=pl.BlockSpec((tm, D_MODEL), lambda i, j: (i, 0)),
        out_shape=jax.ShapeDtypeStruct((n, D_MODEL), F32),
        scratch_shapes=[pltpu.VMEM((tm, D_MODEL), BF16), pltpu.VMEM((tm, D_MODEL), F32)],
        compiler_params=_cparams("parallel", "arbitrary"),
        name="ffn",
    )(x, g, wg, wu, wd)


def _final_norm_kernel(x_ref, g_ref, o_ref):
    x = x_ref[...]
    ms = jnp.mean(x * x, axis=-1, keepdims=True)
    o_ref[...] = x * lax.rsqrt(ms + EPS) * g_ref[...]


def _final_norm(x, g, tm=512):
    n = x.shape[0]
    return pl.pallas_call(
        _final_norm_kernel,
        grid=(n // tm,),
        in_specs=[pl.BlockSpec((tm, D_MODEL), lambda i: (i, 0)), pl.BlockSpec((1, D_MODEL), lambda i: (0, 0))],
        out_specs=pl.BlockSpec((tm, D_MODEL), lambda i: (i, 0)),
        out_shape=jax.ShapeDtypeStruct((n, D_MODEL), F32),
        compiler_params=_cparams("parallel"),
        name="final_norm",
    )(x, g)


def _rope_tables(L):
    rows = L // GRID_W
    r = jnp.repeat(jnp.arange(rows, dtype=F32), GRID_W)
    c = jnp.tile(jnp.arange(GRID_W, dtype=F32), rows)
    inv = ROPE_THETA ** (-jnp.arange(0, ROPE_AXIS_DIM, 2, dtype=F32) / ROPE_AXIS_DIM)
    ang = jnp.concatenate([r[:, None] * inv, c[:, None] * inv], axis=-1)
    cos, sin = jnp.cos(ang), jnp.sin(ang)
    cos_t = jnp.tile(cos, (1, 2 * LANES // HEAD_DIM))
    sin_t = jnp.tile(jnp.concatenate([-sin, sin], axis=-1), (1, LANES // HEAD_DIM))
    return cos_t, sin_t


def _layer_params(l, norm_mix_g, w_in, q_norm_g, k_norm_g, hy_conv_w, hy_conv_b, hy_w1, hy_b1, hy_f1, hy_w2,
                  hy_b2, hy_f2, hy_w3, hy_skip, gla_gate_up, gla_gate_b, gla_norm_g, w_branch, w_out, norm_ffn_g,
                  w_ffn_gate, w_ffn_up, w_ffn_down):
    w = w_in[l]
    offs = {}
    off = 0
    for name, size in (("aq", ATT_W), ("ak", KV_W), ("av", KV_W), ("hz", 3 * HY_W), ("gq", GLA_K), ("gk", GLA_K),
                       ("gv", GLA_V), ("gog", GLA_V), ("glow", 2 * GLA_RANK), ("gates", N_BRANCH * D_MODEL)):
        offs[name] = w[:, off:off + size]
        off += size
    w_cat = jnp.concatenate(
        [offs["gates"], offs["aq"], offs["gv"], offs["gog"], offs["hz"], offs["gq"], offs["gk"], offs["ak"],
         offs["av"], offs["glow"], jnp.zeros((D_MODEL, Z_COLS - Z_GLOW - 2 * GLA_RANK), F32)], axis=1).astype(BF16)
    gup = jnp.zeros((LANES, 2 * GLA_K), F32)
    gup = gup.at[0:GLA_RANK, 0:GLA_K].set(gla_gate_up[l, 0])
    gup = gup.at[GLA_RANK:2 * GLA_RANK, GLA_K:2 * GLA_K].set(gla_gate_up[l, 1])
    deltas = jnp.linspace(abs(math.log(HY_TARGET) / HY_SLOW), abs(math.log(HY_TARGET) / HY_FAST), HY_W, dtype=F32)
    return dict(
        norm_mix_g=norm_mix_g[l][None, :],
        w_cat=w_cat,
        gq=jnp.tile(q_norm_g[l], LANES // HEAD_DIM)[None, :],
        gk=jnp.tile(k_norm_g[l], LANES // HEAD_DIM)[None, :],
        hy_conv_w=jnp.transpose(hy_conv_w[l].reshape(3, 3, HY_W), (1, 0, 2)),
        hy_conv_b=hy_conv_b[l].reshape(3, HY_W),
        hy_w1p=jnp.pad(hy_w1[l], ((0, LANES - HY_EMB), (0, 0))),
        hy_b1=hy_b1[l][None, :], hy_f1=hy_f1[l][None, :],
        hy_w2=hy_w2[l], hy_b2=hy_b2[l][None, :], hy_f2=hy_f2[l][None, :],
        hy_w3s=jnp.transpose(hy_w3[l].reshape(HY_FILTER_W, 2, HY_W), (1, 0, 2)),
        hy_deltas=deltas[None, :],
        hy_skip=hy_skip[l][None, :],
        gup=gup, gb=gla_gate_b[l].reshape(1, 2 * GLA_K),
        gn=gla_norm_g[l][None, :],
        wb=w_branch[l].astype(BF16), wo=w_out[l].astype(BF16),
        norm_ffn_g=norm_ffn_g[l][None, :],
        wg=w_ffn_gate[l].astype(BF16), wu=w_ffn_up[l].astype(BF16), wd=w_ffn_down[l].astype(BF16),
    )


def _encoder_layer(x, lp, B, L, rope, tabs, ks, ssq):
    z = _in_proj(x, lp["norm_mix_g"], lp["w_cat"])
    qt, kr, vt = _attn_prep(z, rope[0], rope[1], lp["gq"], lp["gk"], L)
    y_att = _flash(qt, kr, vt, B, L)
    y_hy = _hyena(z, tabs, ks, ssq, lp, B, L)
    o_f, o_b = _gla(z, lp["gup"], lp["gb"], B, L)
    x = _mix_out(x, y_att, y_hy, o_f, o_b, z, lp["wb"], lp["wo"], lp["gn"])
    return _ffn(x, lp["norm_ffn_g"], lp["wg"], lp["wu"], lp["wd"])


def kernel(x_prompt, x_sample, norm_mix_g, w_in, q_norm_g, k_norm_g, hy_conv_w, hy_conv_b, hy_w1, hy_b1, hy_f1, hy_w2, hy_b2, hy_f2, hy_w3, hy_skip, gla_gate_up, gla_gate_b, gla_norm_g, w_branch, w_out, norm_ffn_g, w_ffn_gate, w_ffn_up, w_ffn_down, final_norm_g):
    streams = []
    for xin in (x_prompt, x_sample):
        B, L, _ = xin.shape
        streams.append(dict(x=xin.reshape(B * L, D_MODEL), B=B, L=L, rope=_rope_tables(L), tabs=_fft_tables(L),
                            feats=_filter_feats(L)))
    for l in range(DEPTH):
        lp = _layer_params(l, norm_mix_g, w_in, q_norm_g, k_norm_g, hy_conv_w, hy_conv_b, hy_w1, hy_b1, hy_f1,
                           hy_w2, hy_b2, hy_f2, hy_w3, hy_skip, gla_gate_up, gla_gate_b, gla_norm_g, w_branch,
                           w_out, norm_ffn_g, w_ffn_gate, w_ffn_up, w_ffn_down)
        for s in streams:
            ks, ssq = _hyena_spectrum(s["tabs"], s["feats"], lp, s["L"])
            s["x"] = _encoder_layer(s["x"], lp, s["B"], s["L"], s["rope"], s["tabs"], ks, ssq)
    outs = []
    for s, xin in zip(streams, (x_prompt, x_sample)):
        outs.append(_final_norm(s["x"], final_norm_g[None, :]).reshape(xin.shape))
    return tuple(outs)
```

```python
import functools
import math

import jax
import jax.numpy as jnp
from jax import lax
from jax.experimental import pallas as pl
from jax.experimental.pallas import tpu as pltpu

F32 = jnp.float32
BF16 = jnp.bfloat16

D_MODEL = 1024
DEPTH = 4
GRID_W = 64
N_HEADS = 8
N_KV_HEADS = 2
HEAD_DIM = 64
GROUP = N_HEADS // N_KV_HEADS
ROPE_THETA = 10000.0
ROPE_AXIS_DIM = HEAD_DIM // 2
ATT_W = N_HEADS * HEAD_DIM
KV_W = N_KV_HEADS * HEAD_DIM
HY_W = D_MODEL // 2
HY_EMB = 33
HY_BANDS = (HY_EMB - 1) // 2
HY_FILTER_W = 64
HY_TARGET = 1e-2
HY_FAST = 0.3
HY_SLOW = 1.5
GLA_HEADS = 4
GLA_DK = 64
GLA_DV = 128
GLA_K = GLA_HEADS * GLA_DK
GLA_V = GLA_HEADS * GLA_DV
GLA_RANK = 16
GLA_TAU = 16.0
GLA_CHUNK = 64
N_BRANCH = 3
MIX_W = 512
D_FF = 2816
EPS = 1e-6

LANES = 128
VMEM_LIMIT = 56 * 1024 * 1024

Z_GATES = 0
Z_AQ = 3072
Z_GV = 3584
Z_GOG = 4096
Z_HZ = 4608
Z_GQ = 6144
Z_GK = 6400
Z_AK = 6656
Z_AV = 6784
Z_GLOW = 6912
Z_COLS = 7168

FFT_N2 = 128
FFT_KB = 8
FFT_SUB = 8
FFT_JT = 1
FFT_CT = 512


def _cparams(*sem):
    return pltpu.CompilerParams(dimension_semantics=sem, vmem_limit_bytes=VMEM_LIMIT)


def _split2(x):
    hi = x.astype(BF16)
    lo = (x - hi.astype(F32)).astype(BF16)
    return hi, lo


def _dot(a, b):
    return jnp.dot(a, b, preferred_element_type=F32)


def _dot3(a, b):
    ah, al = _split2(a)
    bh, bl = _split2(b)
    return _dot(ah, bh) + _dot(al, bh) + _dot(ah, bl)


def _in_proj_kernel(x_ref, g_ref, w_ref, o_ref, h_ref):
    @pl.when(pl.program_id(1) == 0)
    def _():
        x = x_ref[...]
        ms = jnp.mean(x * x, axis=-1, keepdims=True)
        h_ref[...] = (x * lax.rsqrt(ms + EPS) * g_ref[...]).astype(BF16)

    o_ref[...] = _dot(h_ref[...], w_ref[...])


def _in_proj(x, g, w, tm=1024, tn=1024):
    n = x.shape[0]
    return pl.pallas_call(
        _in_proj_kernel,
        grid=(n // tm, Z_COLS // tn),
        in_specs=[
            pl.BlockSpec((tm, D_MODEL), lambda i, j: (i, 0)),
            pl.BlockSpec((1, D_MODEL), lambda i, j: (0, 0)),
            pl.BlockSpec((D_MODEL, tn), lambda i, j: (0, j)),
        ],
        out_specs=pl.BlockSpec((tm, tn), lambda i, j: (i, j)),
        out_shape=jax.ShapeDtypeStruct((n, Z_COLS), F32),
        scratch_shapes=[pltpu.VMEM((tm, D_MODEL), BF16)],
        compiler_params=_cparams("parallel", "arbitrary"),
        name="in_proj",
    )(x, g, w)


V_ROWS = HEAD_DIM + 16
Q_SCALE = (HEAD_DIM ** -0.5) * math.log2(math.e)


def _attn_prep_kernel(q_ref, k_ref, v_ref, cos_ref, sin_ref, gq_ref, gk_ref, qt_ref, kr_ref, vt_ref):
    tm = q_ref.shape[0]
    cos = cos_ref[...]
    sin = sin_ref[...]
    lane = lax.broadcasted_iota(jnp.int32, (tm, LANES), 1)
    first_half = (lane % HEAD_DIM) < (HEAD_DIM // 2)
    left = lane < HEAD_DIM
    r = lax.broadcasted_iota(jnp.int32, (LANES, LANES), 0) // HEAD_DIM
    c = lax.broadcasted_iota(jnp.int32, (LANES, LANES), 1) // HEAD_DIM
    head_ones = jnp.where(r == c, 1.0, 0.0).astype(BF16)

    def norm_rope(x, g):
        xsq = x * x
        hi, lo = _split2(xsq)
        lo2 = (xsq - hi.astype(F32) - lo.astype(F32)).astype(BF16)
        ssq = _dot(hi, head_ones) + _dot(lo, head_ones) + _dot(lo2, head_ones)
        y = x * lax.rsqrt(ssq * (1.0 / HEAD_DIM) + EPS) * g
        other = jnp.where(first_half, pltpu.roll(y, LANES - HEAD_DIM // 2, 1), pltpu.roll(y, HEAD_DIM // 2, 1))
        return y * cos + other * sin

    gq = gq_ref[...]
    for jb in range(ATT_W // LANES):
        y = norm_rope(q_ref[:, jb * LANES:(jb + 1) * LANES], gq) * Q_SCALE
        ys = pltpu.roll(y, HEAD_DIM, 1)
        if (2 * jb) // GROUP == 0:
            e0 = jnp.where(left, y, 0.0)
            e1 = jnp.where(left, ys, 0.0)
        else:
            e0 = jnp.where(left, 0.0, ys)
            e1 = jnp.where(left, 0.0, y)
        qt_ref[2 * jb] = jnp.transpose(e0).astype(BF16)
        qt_ref[2 * jb + 1] = jnp.transpose(e1).astype(BF16)
    kr_ref[...] = norm_rope(k_ref[...], gk_ref[...]).astype(BF16)
    vt = jnp.transpose(v_ref[...])
    ones = jnp.ones((V_ROWS - HEAD_DIM, tm), BF16)
    for kv in range(N_KV_HEADS):
        vt_ref[kv, 0:HEAD_DIM, :] = vt[kv * HEAD_DIM:(kv + 1) * HEAD_DIM].astype(BF16)
        vt_ref[kv, HEAD_DIM:V_ROWS, :] = ones


def _attn_prep(z, cos_t, sin_t, gq, gk, L, tm=512):
    n = z.shape[0]
    lt = L // tm
    return pl.pallas_call(
        _attn_prep_kernel,
        grid=(n // tm,),
        in_specs=[
            pl.BlockSpec((tm, ATT_W), lambda i: (i, Z_AQ // ATT_W)),
            pl.BlockSpec((tm, KV_W), lambda i: (i, Z_AK // KV_W)),
            pl.BlockSpec((tm, KV_W), lambda i: (i, Z_AV // KV_W)),
            pl.BlockSpec((tm, LANES), lambda i: (i % lt, 0)),
            pl.BlockSpec((tm, LANES), lambda i: (i % lt, 0)),
            pl.BlockSpec((1, LANES), lambda i: (0, 0)),
            pl.BlockSpec((1, LANES), lambda i: (0, 0)),
        ],
        out_specs=[
            pl.BlockSpec((N_HEADS, LANES, tm), lambda i: (0, 0, i)),
            pl.BlockSpec((tm, KV_W), lambda i: (i, 0)),
            pl.BlockSpec((N_KV_HEADS, V_ROWS, tm), lambda i: (0, 0, i)),
        ],
        out_shape=[
            jax.ShapeDtypeStruct((N_HEADS, LANES, n), BF16),
            jax.ShapeDtypeStruct((n, KV_W), BF16),
            jax.ShapeDtypeStruct((N_KV_HEADS, V_ROWS, n), BF16),
        ],
        compiler_params=_cparams("parallel"),
        name="attn_prep",
    )(z, z, z, cos_t, sin_t, gq, gk)


def _flash_kernel(qt_ref, k_ref, vt_ref, o_ref, s_ref, p_ref, *, tk):
    tq = qt_ref.shape[2]
    L = k_ref.shape[0]
    nt = L // tk
    heads = range(N_HEADS)

    def scores(j, slot):
        kj = k_ref[pl.ds(pl.multiple_of(j * tk, tk), tk), :]
        for h in heads:
            s_ref[slot, h] = _dot(kj, qt_ref[h])

    def values(j, slot, alphas, accs):
        start = pl.multiple_of(j * tk, tk)
        vjs = [vt_ref[kv, :, pl.ds(start, tk)] for kv in range(N_KV_HEADS)]
        return [alphas[h] * accs[h] + _dot(vjs[h // GROUP], p_ref[slot, h]) for h in heads]

    scores(0, 0)
    p_ref[1] = jnp.zeros(p_ref.shape[1:], BF16)

    def half_step(jv, js, other, cur, ms, alphas, accs):
        kj = k_ref[pl.ds(pl.multiple_of(js * tk, tk), tk), :]
        start = pl.multiple_of(jv * tk, tk)
        vjs = [vt_ref[kv, :, pl.ds(start, tk)] for kv in range(N_KV_HEADS)]
        ms, alphas, accs = list(ms), list(alphas), list(accs)
        for h in heads:
            accs[h] = alphas[h] * accs[h] + _dot(vjs[h // GROUP], p_ref[other, h])
            s = s_ref[cur, h]
            m_new = jnp.maximum(ms[h], jnp.max(s, axis=0, keepdims=True))
            alphas[h] = jnp.exp2(ms[h] - m_new)
            ms[h] = m_new
            p_ref[cur, h] = jnp.exp2(s - m_new).astype(BF16)
            s_ref[other, h] = _dot(kj, qt_ref[h])
        return ms, alphas, accs

    def body(jj, carry):
        ms, alphas, accs = carry
        j = 2 * jj
        ms, alphas, accs = half_step(jnp.maximum(j - 1, 0), j + 1, 1, 0, ms, alphas, accs)
        return half_step(j, jnp.minimum(j + 2, nt - 1), 0, 1, ms, alphas, accs)

    init = ([jnp.full((1, tq), -jnp.inf, F32)] * N_HEADS, [jnp.ones((1, tq), F32)] * N_HEADS,
            [jnp.zeros((V_ROWS, tq), F32)] * N_HEADS)
    _, alphas, accs = lax.fori_loop(0, nt // 2, body, init)
    accs = values(nt - 1, 1, alphas, accs)
    for h in heads:
        o_ref[h * HEAD_DIM:(h + 1) * HEAD_DIM, :] = accs[h][0:HEAD_DIM] / accs[h][HEAD_DIM:HEAD_DIM + 1]


def _flash(qt, kr, vt, B, L, tq=256, tk=512):
    n = kr.shape[0]
    lt = L // tq
    assert (L // tk) % 2 == 0
    return pl.pallas_call(
        functools.partial(_flash_kernel, tk=tk),
        grid=(B, lt),
        in_specs=[
            pl.BlockSpec((N_HEADS, LANES, tq), lambda b, i: (0, 0, b * lt + i)),
            pl.BlockSpec((L, KV_W), lambda b, i: (b, 0)),
            pl.BlockSpec((N_KV_HEADS, V_ROWS, L), lambda b, i: (0, 0, b)),
        ],
        out_specs=pl.BlockSpec((ATT_W, tq), lambda b, i: (0, b * lt + i)),
        out_shape=jax.ShapeDtypeStruct((ATT_W, n), F32),
        scratch_shapes=[pltpu.VMEM((2, N_HEADS, tk, tq), F32), pltpu.VMEM((2, N_HEADS, tk, tq), BF16)],
        compiler_params=_cparams("parallel", "parallel"),
        name="flash",
    )(qt, kr, vt)


def _hy_pre_kernel(v_ref, vp_ref, vn_ref, a_ref, ap_ref, an_ref, b_ref, bp_ref, bn_ref, w_ref, cb_ref,
                   u_ref, x2_ref, *, lt):
    tm = v_ref.shape[0]
    i = pl.program_id(0)
    first = (i % lt) == 0
    last = (i % lt) == lt - 1
    rows = lax.broadcasted_iota(jnp.int32, v_ref.shape, 0)

    def conv3(m_ref, p_ref, n_ref, part):
        x = m_ref[...]
        prev = jnp.where(first, 0.0, p_ref[7:8, :])
        nxt = jnp.where(last, 0.0, n_ref[0:1, :])
        dn = jnp.where(rows == 0, prev, pltpu.roll(x, 1, 0))
        up = jnp.where(rows == tm - 1, nxt, pltpu.roll(x, tm - 1, 0))
        w = w_ref[part]
        return dn * w[0:1] + x * w[1:2] + up * w[2:3] + cb_ref[part:part + 1, :]

    u_ref[...] = conv3(v_ref, vp_ref, vn_ref, 0) * conv3(a_ref, ap_ref, an_ref, 1)
    x2_ref[...] = conv3(b_ref, bp_ref, bn_ref, 2)


def _hy_pre(z, conv_w, conv_b, L, tm=512, tc=512):
    n = z.shape[0]
    lt = L // tm
    r8 = tm // 8
    nb8 = n // 8
    specs = []
    for part in range(3):
        c0 = (Z_HZ + part * HY_W) // tc
        specs.append(pl.BlockSpec((tm, tc), lambda i, c, c0=c0: (i, c0 + c)))
        specs.append(pl.BlockSpec((8, tc), lambda i, c, c0=c0: (jnp.maximum(i * r8 - 1, 0), c0 + c)))
        specs.append(pl.BlockSpec((8, tc), lambda i, c, c0=c0: (jnp.minimum((i + 1) * r8, nb8 - 1), c0 + c)))
    specs.append(pl.BlockSpec((3, 3, tc), lambda i, c: (0, 0, c)))
    specs.append(pl.BlockSpec((3, tc), lambda i, c: (0, c)))
    return pl.pallas_call(
        functools.partial(_hy_pre_kernel, lt=lt),
        grid=(n // tm, HY_W // tc),
        in_specs=specs,
        out_specs=[pl.BlockSpec((tm, tc), lambda i, c: (i, c))] * 2,
        out_shape=[jax.ShapeDtypeStruct((n, HY_W), F32)] * 2,
        compiler_params=_cparams("parallel", "parallel"),
        name="hy_pre",
    )(*([z] * 9), conv_w, conv_b)


def _hy_filt_kernel(f_ref, w1_ref, b1_ref, f1_ref, w2_ref, b2_ref, f2_ref, w3_ref, dl_ref, kf_ref, ssq_ref):
    feats = f_ref[...]
    half = feats.shape[0] // 2
    halves = (feats[:half], feats[half:])
    h = jnp.sin(f1_ref[...] * (_dot3(jnp.concatenate(halves, axis=1), w1_ref[...]) + b1_ref[...]))
    h = jnp.sin(f2_ref[...] * (_dot3(h, w2_ref[...]) + b2_ref[...]))

    @pl.when(pl.program_id(0) == 0)
    def _():
        ssq_ref[...] = jnp.zeros_like(ssq_ref)

    for i, f in enumerate(halves):
        t = f[:, 0:1]
        valid = f[:, HY_EMB:HY_EMB + 1]
        kf = _dot3(h, w3_ref[i]) * jnp.exp(-t * dl_ref[...]) * valid
        kf_ref[i * half:(i + 1) * half, :] = kf
        ssq_ref[...] += jnp.sum(kf * kf, axis=0, keepdims=True)


def _hy_filt(feats_ext, w1p, b1, f1, w2, b2, f2, w3s, deltas, L, tr=512):
    lt = L // tr
    full = lambda a: pl.BlockSpec(a.shape, lambda i: (0,) * a.ndim)
    return pl.pallas_call(
        _hy_filt_kernel,
        grid=(2 * lt,),
        in_specs=[
            pl.BlockSpec((tr, LANES), lambda i: (i, 0)),
            full(w1p), full(b1), full(f1), full(w2), full(b2), full(f2),
            pl.BlockSpec((None, 2, 2 * HY_FILTER_W, HY_W), lambda i: (i // lt, 0, 0, 0)),
            full(deltas),
        ],
        out_specs=[pl.BlockSpec((tr, HY_W), lambda i: (i, 0)), pl.BlockSpec((1, HY_W), lambda i: (0, 0))],
        out_shape=[jax.ShapeDtypeStruct((2 * L, HY_W), F32), jax.ShapeDtypeStruct((1, HY_W), F32)],
        compiler_params=_cparams("arbitrary"),
        name="hy_filt",
    )(feats_ext, w1p, b1, f1, w2, b2, f2, w3s, deltas)


def _fft1_kernel(x_ref, f_ref, o_ref):
    hin, rows = x_ref.shape[0], o_ref.shape[0]
    npos = x_ref.shape[1] * x_ref.shape[2]
    x2 = x_ref.reshape(hin * npos, LANES)
    o2 = o_ref.reshape(rows * npos, LANES)
    f = f_ref[...]
    for p in range(0, npos, 2):
        x = jnp.concatenate([x2[pl.ds(p, hin, stride=npos), :], x2[pl.ds(p + 1, hin, stride=npos), :]], axis=1)
        xh, xl = _split2(x)
        y = _dot(f, jnp.concatenate([xh, xl, xh], axis=0))
        o2[pl.ds(p, rows, stride=npos), :] = y[:, :LANES]
        o2[pl.ds(p + 1, rows, stride=npos), :] = y[:, LANES:]


def _fft1(x, fcat, B, hin):
    C = x.shape[-1]
    rows = fcat.shape[0]
    nj = FFT_N2 // FFT_SUB
    out = pl.pallas_call(
        _fft1_kernel,
        grid=(B, nj // FFT_JT, C // LANES),
        in_specs=[
            pl.BlockSpec((None, hin, FFT_JT, FFT_SUB, LANES), lambda b, j, c: (b, 0, j, 0, c)),
            pl.BlockSpec(fcat.shape, lambda b, j, c: (0, 0)),
        ],
        out_specs=pl.BlockSpec((None, rows, FFT_JT, FFT_SUB, LANES), lambda b, j, c: (b, 0, j, 0, c)),
        out_shape=jax.ShapeDtypeStruct((B, rows, nj, FFT_SUB, C), F32),
        compiler_params=_cparams("parallel", "parallel", "parallel"),
        name="fft1",
    )(x.reshape(B, hin, nj, FFT_SUB, C), fcat)
    return out.reshape(B, rows, FFT_N2, C)


def _fft2_kernel(ar_ref, ai_ref, tr_ref, ti_ref, m_ref, o_ref):
    m = m_ref[...]
    for r in range(FFT_KB):
        ar, ai = ar_ref[r], ai_ref[r]
        tr, ti = (jnp.tile(t[r], (1, FFT_CT // LANES)) for t in (tr_ref, ti_ref))
        b = jnp.concatenate([ar * tr - ai * ti, ar * ti + ai * tr], axis=0)
        bh, bl = _split2(b)
        o_ref[r] = _dot(m, jnp.concatenate([bh, bl, bh], axis=0))


def _fft2(a, tw_r, tw_i, mcat, kp):
    B = a.shape[0]
    C = a.shape[-1]
    nk = kp // FFT_KB
    return pl.pallas_call(
        _fft2_kernel,
        grid=(B, C // FFT_CT, nk),
        in_specs=[
            pl.BlockSpec((None, FFT_KB, FFT_N2, FFT_CT), lambda b, c, k: (b, k, 0, c)),
            pl.BlockSpec((None, FFT_KB, FFT_N2, FFT_CT), lambda b, c, k: (b, nk + k, 0, c)),
            pl.BlockSpec((FFT_KB, FFT_N2, LANES), lambda b, c, k: (k, 0, 0)),
            pl.BlockSpec((FFT_KB, FFT_N2, LANES), lambda b, c, k: (k, 0, 0)),
            pl.BlockSpec(mcat.shape, lambda b, c, k: (0, 0)),
        ],
        out_specs=pl.BlockSpec((None, FFT_KB, 2 * FFT_N2, FFT_CT), lambda b, c, k: (b, k, 0, c)),
        out_shape=jax.ShapeDtypeStruct((B, kp, 2 * FFT_N2, C), F32),
        compiler_params=_cparams("parallel", "parallel", "parallel"),
        name="fft2",
    )(a, a, tw_r, tw_i, mcat)


def _ifft2_kernel(x_ref, k_ref, ssq_ref, tr_ref, ti_ref, m_ref, o_ref):
    m = m_ref[...]
    scale = lax.rsqrt(ssq_ref[...] + EPS)
    n2 = FFT_N2
    for r in range(FFT_KB):
        x = x_ref[r]
        kf = k_ref[r]
        xr, xi = x[:n2], x[n2:]
        kr, ki = kf[:n2] * scale, kf[n2:] * scale
        p = jnp.concatenate([xr * kr - xi * ki, xr * ki + xi * kr], axis=0)
        ph, plo = _split2(p)
        c = _dot(m, jnp.concatenate([ph, plo, ph], axis=0))
        cr, ci = c[:n2], c[n2:]
        tr, ti = (jnp.tile(t[r], (1, FFT_CT // LANES)) for t in (tr_ref, ti_ref))
        o_ref[0, r] = cr * tr + ci * ti
        o_ref[1, r] = ci * tr - cr * ti


def _ifft2(xs, ks, ssq, tw_r, tw_i, mcat_inv):
    B, kp, _, C = xs.shape
    nk = kp // FFT_KB
    return pl.pallas_call(
        _ifft2_kernel,
        grid=(B, C // FFT_CT, nk),
        in_specs=[
            pl.BlockSpec((None, FFT_KB, 2 * FFT_N2, FFT_CT), lambda b, c, k: (b, k, 0, c)),
            pl.BlockSpec((None, FFT_KB, 2 * FFT_N2, FFT_CT), lambda b, c, k: (0, k, 0, c)),
            pl.BlockSpec((1, FFT_CT), lambda b, c, k: (0, c)),
            pl.BlockSpec((FFT_KB, FFT_N2, LANES), lambda b, c, k: (k, 0, 0)),
            pl.BlockSpec((FFT_KB, FFT_N2, LANES), lambda b, c, k: (k, 0, 0)),
            pl.BlockSpec(mcat_inv.shape, lambda b, c, k: (0, 0)),
        ],
        out_specs=pl.BlockSpec((None, 2, FFT_KB, FFT_N2, FFT_CT), lambda b, c, k: (b, 0, k, 0, c)),
        out_shape=jax.ShapeDtypeStruct((B, 2, kp, FFT_N2, C), F32),
        compiler_params=_cparams("parallel", "parallel", "parallel"),
        name="ifft2",
    )(xs, ks, ssq, tw_r, tw_i, mcat_inv)


def _ifft1_kernel(d_ref, w_ref, u_ref, x2_ref, skip_ref, o_ref):
    rows, h = d_ref.shape[0], o_ref.shape[0]
    npos = d_ref.shape[1] * d_ref.shape[2]
    d2 = d_ref.reshape(rows * npos, LANES)
    u2, x22, o2 = (r.reshape(h * npos, LANES) for r in (u_ref, x2_ref, o_ref))
    w = w_ref[...]
    skip = skip_ref[...]
    for p in range(0, npos, 2):
        d = jnp.concatenate([d2[pl.ds(p, rows, stride=npos), :], d2[pl.ds(p + 1, rows, stride=npos), :]], axis=1)
        dh, dl = _split2(d)
        y = _dot(w, jnp.concatenate([dh, dl, dh], axis=0))
        for i in range(2):
            sl = pl.ds(p + i, h, stride=npos)
            o2[sl, :] = x22[sl, :] * (y[:, i * LANES:(i + 1) * LANES] + u2[sl, :] * skip)


def _ifft1(d, wcat, u, x2c, skip, B):
    rows, C = d.shape[1], d.shape[-1]
    h = wcat.shape[0]
    nj = FFT_N2 // FFT_SUB
    seq = pl.BlockSpec((None, h, FFT_JT, FFT_SUB, LANES), lambda b, j, c: (b, 0, j, 0, c))
    out = pl.pallas_call(
        _ifft1_kernel,
        grid=(B, nj // FFT_JT, C // LANES),
        in_specs=[
            pl.BlockSpec((None, rows, FFT_JT, FFT_SUB, LANES), lambda b, j, c: (b, 0, j, 0, c)),
            pl.BlockSpec(wcat.shape, lambda b, j, c: (0, 0)),
            seq, seq,
            pl.BlockSpec((1, LANES), lambda b, j, c: (0, c)),
        ],
        out_specs=seq,
        out_shape=jax.ShapeDtypeStruct((B, h, nj, FFT_SUB, C), F32),
        compiler_params=_cparams("parallel", "parallel", "parallel"),
        name="ifft1",
    )(d.reshape(B, rows, nj, FFT_SUB, C), wcat, u.reshape(B, h, nj, FFT_SUB, C), x2c.reshape(B, h, nj, FFT_SUB, C),
      skip)
    return out.reshape(B * h * FFT_N2, C)


def _hilo_cat(m):
    hi = m.astype(BF16)
    lo = (m - hi.astype(F32)).astype(BF16)
    return jnp.concatenate([hi, hi, lo], axis=1)


def _fft_tables(L):
    n = 2 * L
    n2 = FFT_N2
    n1 = n // n2
    h = n1 // 2
    kp = h + FFT_KB
    k1 = jnp.arange(kp, dtype=jnp.int32)

    def outer(hin):
        nn = jnp.arange(hin, dtype=jnp.int32)
        ang = (2.0 * math.pi / n1) * ((k1[:, None] * nn[None, :]) % n1).astype(F32)
        return _hilo_cat(jnp.concatenate([jnp.cos(ang), -jnp.sin(ang)], axis=0))

    f_data = outer(h)
    f_filt = outer(n1)
    j = jnp.arange(n2, dtype=jnp.int32)
    ang2 = (2.0 * math.pi / n2) * ((j[:, None] * j[None, :]) % n2).astype(F32)
    c2, s2 = jnp.cos(ang2), jnp.sin(ang2)
    m_fwd = _hilo_cat(jnp.block([[c2, s2], [-s2, c2]]))
    m_inv = _hilo_cat(jnp.block([[c2, -s2], [s2, c2]]))
    angt = (2.0 * math.pi / n) * (k1[:, None] * j[None, :]).astype(F32)
    tw_r = jnp.broadcast_to(jnp.cos(angt)[:, :, None], (kp, n2, LANES))
    tw_i = jnp.broadcast_to(-jnp.sin(angt)[:, :, None], (kp, n2, LANES))
    wgt = jnp.where((k1 == 0) | (k1 == h), 1.0, jnp.where(k1 < h, 2.0, 0.0)).astype(F32) / n
    nn = jnp.arange(h, dtype=jnp.int32)
    angi = (2.0 * math.pi / n1) * ((nn[:, None] * k1[None, :]) % n1).astype(F32)
    w_inv = _hilo_cat(jnp.concatenate([jnp.cos(angi) * wgt, -jnp.sin(angi) * wgt], axis=1))
    return dict(n1=n1, h=h, kp=kp, f_data=f_data, f_filt=f_filt, m_fwd=m_fwd, m_inv=m_inv,
                tw_r=tw_r, tw_i=tw_i, w_inv=w_inv)


def _filter_feats(L):
    i = jnp.arange(2 * L, dtype=jnp.int32)
    n = jnp.where(i < L, i, 2 * L - i).astype(F32)
    t = (n / (L - 1))[:, None]
    w = 2.0 * math.pi * n / L
    fb = jnp.linspace(1e-4, HY_BANDS - 1, HY_BANDS, dtype=F32)
    ph = w[:, None] * fb
    valid = jnp.where(i == L, 0.0, 1.0).astype(F32)[:, None]
    ext = jnp.concatenate([t, jnp.cos(ph), -jnp.sin(ph), valid], axis=-1)
    return jnp.pad(ext, ((0, 0), (0, LANES - HY_EMB - 1)))


def _hyena_spectrum(tabs, feats_ext, lp, L):
    kf, ssq = _hy_filt(feats_ext, lp["hy_w1p"], lp["hy_b1"], lp["hy_f1"], lp["hy_w2"], lp["hy_b2"], lp["hy_f2"],
                       lp["hy_w3s"], lp["hy_deltas"], L)
    a = _fft1(kf, tabs["f_filt"], 1, tabs["n1"])
    ks = _fft2(a, tabs["tw_r"], tabs["tw_i"], tabs["m_fwd"], tabs["kp"])
    return ks, ssq


def _hyena(z, tabs, ks, ssq, lp, B, L):
    u, x2c = _hy_pre(z, lp["hy_conv_w"], lp["hy_conv_b"], L)
    kp = tabs["kp"]
    a = _fft1(u, tabs["f_data"], B, tabs["h"])
    xs = _fft2(a, tabs["tw_r"], tabs["tw_i"], tabs["m_fwd"], kp)
    d = _ifft2(xs, ks, ssq, tabs["tw_r"], tabs["tw_i"], tabs["m_inv"])
    return _ifft1(d.reshape(B, 2 * kp, FFT_N2, HY_W), tabs["w_inv"], u, x2c, lp["hy_skip"], B)


def _gla_kernel(qf_ref, kf_ref, vf_ref, lf_ref, qb_ref, kb_ref, vb_ref, lb_ref, gup_ref, gb_ref,
                of_ref, ob_ref, sf_ref, sb_ref, *, nchunk):
    C = GLA_CHUNK
    R = nchunk * C

    @pl.when(pl.program_id(1) == 0)
    def _():
        sf_ref[...] = jnp.zeros_like(sf_ref)
        sb_ref[...] = jnp.zeros_like(sb_ref)

    rr = lax.broadcasted_iota(jnp.int32, (R, R), 0)
    cc = lax.broadcasted_iota(jnp.int32, (R, R), 1)
    same_chunk = (rr // C) == (cc // C)
    lane = lax.broadcasted_iota(jnp.int32, (C, GLA_K), 1) // GLA_DK
    ri4 = lax.broadcasted_iota(jnp.int32, (GLA_HEADS * C, C), 0) % C
    ci4 = lax.broadcasted_iota(jnp.int32, (GLA_HEADS * C, C), 1)

    def stack_heads(x):
        return jnp.concatenate([jnp.where(lane == h, x, 0.0) for h in range(GLA_HEADS)], axis=0).astype(BF16)

    def per_chunk_rows(b, row):
        return jnp.concatenate([jnp.broadcast_to(b[c * C + row:c * C + row + 1], (C, GLA_K)) for c in range(nchunk)],
                               axis=0)

    dirs = []
    for d, (q_ref, k_ref, v_ref, l_ref) in enumerate(((qf_ref, kf_ref, vf_ref, lf_ref),
                                                       (qb_ref, kb_ref, vb_ref, lb_ref))):
        logit = _dot3(l_ref[...], gup_ref[:, d * GLA_K:(d + 1) * GLA_K]) + gb_ref[:, d * GLA_K:(d + 1) * GLA_K]
        la = (jnp.minimum(logit, 0.0) - jnp.log(1.0 + jnp.exp(-jnp.abs(logit)))) * (1.0 / GLA_TAU)
        if d == 0:
            tri = jnp.where(same_chunk & (rr >= cc), 1.0, 0.0).astype(BF16)
            keep = ri4 >= ci4
            mid, last = C // 2 - 1, C - 1
            order = list(range(nchunk))
        else:
            tri = jnp.where(same_chunk & (cc >= rr), 1.0, 0.0).astype(BF16)
            keep = ci4 > ri4
            mid, last = C // 2, 0
            order = list(range(nchunk - 1, -1, -1))
        hi, lo = _split2(la)
        lo2 = (la - hi.astype(F32) - lo.astype(F32)).astype(BF16)
        b = _dot(tri, hi) + _dot(tri, lo) + _dot(tri, lo2)
        b_mid = per_chunk_rows(b, mid)
        b_last = per_chunk_rows(b, last)
        q = q_ref[...] * (GLA_DK ** -0.5)
        k = k_ref[...]
        dirs.append(dict(
            keep=keep, order=order, v=v_ref[...].astype(BF16),
            qt=q * jnp.exp(b - b_mid), kt=(k * jnp.exp(b_mid - b)).astype(BF16),
            kh=k * jnp.exp(b_last - b), qh=q * jnp.exp(b), dec=jnp.exp(b_last)))

    for dd in dirs:
        dd["a"] = []
        for c in range(nchunk):
            rs = slice(c * C, (c + 1) * C)
            a = lax.dot_general(stack_heads(dd["qt"][rs]), dd["kt"][rs], (((1,), (1,)), ((), ())),
                                preferred_element_type=F32)
            dd["a"].append(jnp.where(dd["keep"], a, 0.0).astype(BF16))

    for dd in dirs:
        dd["o"], dd["u"] = [], []
        for c in range(nchunk):
            rs = slice(c * C, (c + 1) * C)
            k_t = jnp.transpose(dd["kh"][rs]).astype(BF16)
            outs, ups = [], []
            for h in range(GLA_HEADS):
                vh = dd["v"][rs, h * GLA_DV:(h + 1) * GLA_DV]
                outs.append(_dot(dd["a"][c][h * C:(h + 1) * C], vh))
                ups.append(_dot(k_t[h * GLA_DK:(h + 1) * GLA_DK], vh))
            dd["o"].append(outs)
            dd["u"].append(jnp.concatenate(ups, axis=0))

    for dd, s_ref in zip(dirs, (sf_ref, sb_ref)):
        s = s_ref[...]
        dd["s_prev"] = {}
        for c in dd["order"]:
            dd["s_prev"][c] = s.astype(BF16)
            decay = jnp.transpose(jnp.broadcast_to(dd["dec"][c * C:c * C + 1], (GLA_DV, GLA_K)))
            s = decay * s + dd["u"][c]
        s_ref[...] = s

    for dd, o_ref in zip(dirs, (of_ref, ob_ref)):
        for c in range(nchunk):
            rs = slice(c * C, (c + 1) * C)
            o_inter = _dot(stack_heads(dd["qh"][rs]), dd["s_prev"][c])
            o_ref[rs, :] = jnp.concatenate(
                [dd["o"][c][h] + o_inter[h * C:(h + 1) * C] for h in range(GLA_HEADS)], axis=1)


def _gla(z, gup, gb, B, L, nchunk=4):
    n = z.shape[0]
    R = nchunk * GLA_CHUNK
    T = L // R
    fwd = lambda b, t: b * T + t
    bwd = lambda b, t: b * T + (T - 1 - t)

    def specs(row):
        return [
            pl.BlockSpec((R, GLA_K), lambda b, t: (row(b, t), Z_GQ // GLA_K)),
            pl.BlockSpec((R, GLA_K), lambda b, t: (row(b, t), Z_GK // GLA_K)),
            pl.BlockSpec((R, GLA_V), lambda b, t: (row(b, t), Z_GV // GLA_V)),
            pl.BlockSpec((R, LANES), lambda b, t: (row(b, t), Z_GLOW // LANES)),
        ]

    return pl.pallas_call(
        functools.partial(_gla_kernel, nchunk=nchunk),
        grid=(B, T),
        in_specs=specs(fwd) + specs(bwd) + [
            pl.BlockSpec(gup.shape, lambda b, t: (0, 0)),
            pl.BlockSpec(gb.shape, lambda b, t: (0, 0)),
        ],
        out_specs=[
            pl.BlockSpec((R, GLA_V), lambda b, t: (fwd(b, t), 0)),
            pl.BlockSpec((R, GLA_V), lambda b, t: (bwd(b, t), 0)),
        ],
        out_shape=[jax.ShapeDtypeStruct((n, GLA_V), F32)] * 2,
        scratch_shapes=[pltpu.VMEM((GLA_K, GLA_DV), F32)] * 2,
        compiler_params=_cparams("parallel", "arbitrary"),
        name="gla",
    )(*([z] * 8), gup, gb)


def _mix_out_kernel(x_ref, ya_ref, yh_ref, of_ref, ob_ref, og_ref, g0_ref, g1_ref, g2_ref,
                    wb_ref, wo_ref, gn_ref, o_ref):
    o = of_ref[...] + ob_ref[...]
    og = og_ref[...]
    gn = gn_ref[...]
    parts = []
    for h in range(GLA_HEADS):
        blk = o[:, h * GLA_DV:(h + 1) * GLA_DV]
        ms = jnp.mean(blk * blk, axis=-1, keepdims=True)
        gate = og[:, h * GLA_DV:(h + 1) * GLA_DV]
        parts.append(blk * lax.rsqrt(ms + EPS) * gn * (gate * jax.nn.sigmoid(gate)))
    y_gla = jnp.concatenate(parts, axis=1)
    proj_att = lax.dot_general(ya_ref[...].astype(BF16), wb_ref[0], (((0,), (0,)), ((), ())),
                               preferred_element_type=F32)
    merged = jax.nn.sigmoid(g0_ref[...]) * proj_att
    merged += jax.nn.sigmoid(g1_ref[...]) * _dot(yh_ref[...].astype(BF16), wb_ref[1])
    merged += jax.nn.sigmoid(g2_ref[...]) * _dot(y_gla.astype(BF16), wb_ref[2])
    o_ref[...] = x_ref[...] + _dot(merged.astype(BF16), wo_ref[...])


def _mix_out(x, y_att, y_hy, o_f, o_b, z, wb, wo, gn, tm=256):
    n = x.shape[0]
    row = lambda w: pl.BlockSpec((tm, w), lambda i: (i, 0))
    return pl.pallas_call(
        _mix_out_kernel,
        grid=(n // tm,),
        in_specs=[
            row(D_MODEL), pl.BlockSpec((MIX_W, tm), lambda i: (0, i)), row(MIX_W), row(MIX_W), row(MIX_W),
            pl.BlockSpec((tm, GLA_V), lambda i: (i, Z_GOG // GLA_V)),
            pl.BlockSpec((tm, D_MODEL), lambda i: (i, 0)),
            pl.BlockSpec((tm, D_MODEL), lambda i: (i, 1)),
            pl.BlockSpec((tm, D_MODEL), lambda i: (i, 2)),
            pl.BlockSpec(wb.shape, lambda i: (0, 0, 0)),
            pl.BlockSpec(wo.shape, lambda i: (0, 0)),
            pl.BlockSpec((1, GLA_DV), lambda i: (0, 0)),
        ],
        out_specs=row(D_MODEL),
        out_shape=jax.ShapeDtypeStruct((n, D_MODEL), F32),
        compiler_params=_cparams("parallel"),
        name="mix_out",
    )(x, y_att, y_hy, o_f, o_b, z, z, z, z, wb, wo, gn)


def _ffn_kernel(x_ref, g_ref, wg_ref, wu_ref, wd_ref, o_ref, h_ref, acc_ref):
    j = pl.program_id(1)

    @pl.when(j == 0)
    def _():
        x = x_ref[...]
        ms = jnp.mean(x * x, axis=-1, keepdims=True)
        h_ref[...] = (x * lax.rsqrt(ms + EPS) * g_ref[...]).astype(BF16)
        acc_ref[...] = jnp.zeros_like(acc_ref)

    h = h_ref[...]
    a = _dot(h, wg_ref[...])
    act = (a * jax.nn.sigmoid(a)) * _dot(h, wu_ref[...])
    acc_ref[...] += _dot(act.astype(BF16), wd_ref[...])

    @pl.when(j == pl.num_programs(1) - 1)
    def _():
        o_ref[...] = x_ref[...] + acc_ref[...]


def _ffn(x, g, wg, wu, wd, tm=512, tf=1408):
    n = x.shape[0]
    return pl.pallas_call(
        _ffn_kernel,
        grid=(n // tm, D_FF // tf),
        in_specs=[
            pl.BlockSpec((tm, D_MODEL), lambda i, j: (i, 0)),
            pl.BlockSpec((1, D_MODEL), lambda i, j: (0, 0)),
            pl.BlockSpec((D_MODEL, tf), lambda i, j: (0, j)),
            pl.BlockSpec((D_MODEL, tf), lambda i, j: (0, j)),
            pl.BlockSpec((tf, D_MODEL), lambda i, j: (j, 0)),
        ],
        out_specs=pl.BlockSpec((tm, D_MODEL), lambda i, j: (i, 0)),
        out_shape=jax.ShapeDtypeStruct((n, D_MODEL), F32),
        scratch_shapes=[pltpu.VMEM((tm, D_MODEL), BF16), pltpu.VMEM((tm, D_MODEL), F32)],
        compiler_params=_cparams("parallel", "arbitrary"),
        name="ffn",
    )(x, g, wg, wu, wd)


def _final_norm_kernel(x_ref, g_ref, o_ref):
    x = x_ref[...]
    ms = jnp.mean(x * x, axis=-1, keepdims=True)
    o_ref[...] = x * lax.rsqrt(ms + EPS) * g_ref[...]


def _final_norm(x, g, tm=512):
    n = x.shape[0]
    return pl.pallas_call(
        _final_norm_kernel,
        grid=(n // tm,),
        in_specs=[pl.BlockSpec((tm, D_MODEL), lambda i: (i, 0)), pl.BlockSpec((1, D_MODEL), lambda i: (0, 0))],
        out_specs=pl.BlockSpec((tm, D_MODEL), lambda i: (i, 0)),
        out_shape=jax.ShapeDtypeStruct((n, D_MODEL), F32),
        compiler_params=_cparams("parallel"),
        name="final_norm",
    )(x, g)


def _rope_tables(L):
    rows = L // GRID_W
    r = jnp.repeat(jnp.arange(rows, dtype=F32), GRID_W)
    c = jnp.tile(jnp.arange(GRID_W, dtype=F32), rows)
    inv = ROPE_THETA ** (-jnp.arange(0, ROPE_AXIS_DIM, 2, dtype=F32) / ROPE_AXIS_DIM)
    ang = jnp.concatenate([r[:, None] * inv, c[:, None] * inv], axis=-1)
    cos, sin = jnp.cos(ang), jnp.sin(ang)
    cos_t = jnp.tile(cos, (1, 2 * LANES // HEAD_DIM))
    sin_t = jnp.tile(jnp.concatenate([-sin, sin], axis=-1), (1, LANES // HEAD_DIM))
    return cos_t, sin_t


def _layer_params(l, norm_mix_g, w_in, q_norm_g, k_norm_g, hy_conv_w, hy_conv_b, hy_w1, hy_b1, hy_f1, hy_w2,
                  hy_b2, hy_f2, hy_w3, hy_skip, gla_gate_up, gla_gate_b, gla_norm_g, w_branch, w_out, norm_ffn_g,
                  w_ffn_gate, w_ffn_up, w_ffn_down):
    w = w_in[l]
    offs = {}
    off = 0
    for name, size in (("aq", ATT_W), ("ak", KV_W), ("av", KV_W), ("hz", 3 * HY_W), ("gq", GLA_K), ("gk", GLA_K),
                       ("gv", GLA_V), ("gog", GLA_V), ("glow", 2 * GLA_RANK), ("gates", N_BRANCH * D_MODEL)):
        offs[name] = w[:, off:off + size]
        off += size
    w_cat = jnp.concatenate(
        [offs["gates"], offs["aq"], offs["gv"], offs["gog"], offs["hz"], offs["gq"], offs["gk"], offs["ak"],
         offs["av"], offs["glow"], jnp.zeros((D_MODEL, Z_COLS - Z_GLOW - 2 * GLA_RANK), F32)], axis=1).astype(BF16)
    gup = jnp.zeros((LANES, 2 * GLA_K), F32)
    gup = gup.at[0:GLA_RANK, 0:GLA_K].set(gla_gate_up[l, 0])
    gup = gup.at[GLA_RANK:2 * GLA_RANK, GLA_K:2 * GLA_K].set(gla_gate_up[l, 1])
    deltas = jnp.linspace(abs(math.log(HY_TARGET) / HY_SLOW), abs(math.log(HY_TARGET) / HY_FAST), HY_W, dtype=F32)
    return dict(
        norm_mix_g=norm_mix_g[l][None, :],
        w_cat=w_cat,
        gq=jnp.tile(q_norm_g[l], LANES // HEAD_DIM)[None, :],
        gk=jnp.tile(k_norm_g[l], LANES // HEAD_DIM)[None, :],
        hy_conv_w=jnp.transpose(hy_conv_w[l].reshape(3, 3, HY_W), (1, 0, 2)),
        hy_conv_b=hy_conv_b[l].reshape(3, HY_W),
        hy_w1p=jnp.kron(jnp.eye(2, dtype=F32), jnp.pad(hy_w1[l], ((0, LANES - HY_EMB), (0, 0)))),
        hy_b1=jnp.tile(hy_b1[l], 2)[None, :], hy_f1=jnp.tile(hy_f1[l], 2)[None, :],
        hy_w2=jnp.kron(jnp.eye(2, dtype=F32), hy_w2[l]),
        hy_b2=jnp.tile(hy_b2[l], 2)[None, :], hy_f2=jnp.tile(hy_f2[l], 2)[None, :],
        hy_w3s=jnp.einsum("gh,kfc->fghkc", jnp.eye(2, dtype=F32), hy_w3[l].reshape(HY_FILTER_W, 2, HY_W)
                          ).reshape(2, 2, 2 * HY_FILTER_W, HY_W),
        hy_deltas=deltas[None, :],
        hy_skip=hy_skip[l][None, :],
        gup=gup, gb=gla_gate_b[l].reshape(1, 2 * GLA_K),
        gn=gla_norm_g[l][None, :],
        wb=w_branch[l].astype(BF16), wo=w_out[l].astype(BF16),
        norm_ffn_g=norm_ffn_g[l][None, :],
        wg=w_ffn_gate[l].astype(BF16), wu=w_ffn_up[l].astype(BF16), wd=w_ffn_down[l].astype(BF16),
    )


def _encoder_layer(x, lp, B, L, rope, tabs, ks, ssq):
    z = _in_proj(x, lp["norm_mix_g"], lp["w_cat"])
    qt, kr, vt = _attn_prep(z, rope[0], rope[1], lp["gq"], lp["gk"], L)
    y_att = _flash(qt, kr, vt, B, L)
    y_hy = _hyena(z, tabs, ks, ssq, lp, B, L)
    o_f, o_b = _gla(z, lp["gup"], lp["gb"], B, L)
    x = _mix_out(x, y_att, y_hy, o_f, o_b, z, lp["wb"], lp["wo"], lp["gn"])
    return _ffn(x, lp["norm_ffn_g"], lp["wg"], lp["wu"], lp["wd"])


def kernel(x_prompt, x_sample, norm_mix_g, w_in, q_norm_g, k_norm_g, hy_conv_w, hy_conv_b, hy_w1, hy_b1, hy_f1, hy_w2, hy_b2, hy_f2, hy_w3, hy_skip, gla_gate_up, gla_gate_b, gla_norm_g, w_branch, w_out, norm_ffn_g, w_ffn_gate, w_ffn_up, w_ffn_down, final_norm_g):
    streams = []
    for xin in (x_prompt, x_sample):
        B, L, _ = xin.shape
        streams.append(dict(x=xin.reshape(B * L, D_MODEL), B=B, L=L, rope=_rope_tables(L), tabs=_fft_tables(L),
                            feats=_filter_feats(L)))
    for l in range(DEPTH):
        lp = _layer_params(l, norm_mix_g, w_in, q_norm_g, k_norm_g, hy_conv_w, hy_conv_b, hy_w1, hy_b1, hy_f1,
                           hy_w2, hy_b2, hy_f2, hy_w3, hy_skip, gla_gate_up, gla_gate_b, gla_norm_g, w_branch,
                           w_out, norm_ffn_g, w_ffn_gate, w_ffn_up, w_ffn_down)
        for s in streams:
            ks, ssq = _hyena_spectrum(s["tabs"], s["feats"], lp, s["L"])
            s["x"] = _encoder_layer(s["x"], lp, s["B"], s["L"], s["rope"], s["tabs"], ks, ssq)
    outs = []
    for s, xin in zip(streams, (x_prompt, x_sample)):
        outs.append(_final_norm(s["x"], final_norm_g[None, :]).reshape(xin.shape))
    return tuple(outs)
```

```python
import functools
import math

import jax
import jax.numpy as jnp
from jax import lax
from jax.experimental import pallas as pl
from jax.experimental.pallas import tpu as pltpu

F32 = jnp.float32
BF16 = jnp.bfloat16

D_MODEL = 1024
DEPTH = 4
GRID_W = 64
N_HEADS = 8
N_KV_HEADS = 2
HEAD_DIM = 64
GROUP = N_HEADS // N_KV_HEADS
ROPE_THETA = 10000.0
ROPE_AXIS_DIM = HEAD_DIM // 2
ATT_W = N_HEADS * HEAD_DIM
KV_W = N_KV_HEADS * HEAD_DIM
HY_W = D_MODEL // 2
HY_EMB = 33
HY_BANDS = (HY_EMB - 1) // 2
HY_FILTER_W = 64
HY_TARGET = 1e-2
HY_FAST = 0.3
HY_SLOW = 1.5
GLA_HEADS = 4
GLA_DK = 64
GLA_DV = 128
GLA_K = GLA_HEADS * GLA_DK
GLA_V = GLA_HEADS * GLA_DV
GLA_RANK = 16
GLA_TAU = 16.0
GLA_CHUNK = 64
N_BRANCH = 3
MIX_W = 512
D_FF = 2816
EPS = 1e-6

LANES = 128
VMEM_LIMIT = 56 * 1024 * 1024

Z_GATES = 0
Z_AQ = 3072
Z_GV = 3584
Z_GOG = 4096
Z_HZ = 4608
Z_GQ = 6144
Z_GK = 6400
Z_AK = 6656
Z_AV = 6784
Z_GLOW = 6912
Z_COLS = 7168

FFT_N2 = 128
FFT_KB = 8
FFT_SUB = 8
FFT_CT = 512


def _cparams(*sem):
    return pltpu.CompilerParams(dimension_semantics=sem, vmem_limit_bytes=VMEM_LIMIT)


def _split2(x):
    hi = x.astype(BF16)
    lo = (x - hi.astype(F32)).astype(BF16)
    return hi, lo


def _dot(a, b):
    return jnp.dot(a, b, preferred_element_type=F32)


def _dot3(a, b):
    ah, al = _split2(a)
    bh, bl = _split2(b)
    return _dot(ah, bh) + _dot(al, bh) + _dot(ah, bl)


def _in_proj_kernel(x_ref, g_ref, w_ref, o_ref, h_ref):
    @pl.when(pl.program_id(1) == 0)
    def _():
        x = x_ref[...]
        ms = jnp.mean(x * x, axis=-1, keepdims=True)
        h_ref[...] = (x * lax.rsqrt(ms + EPS) * g_ref[...]).astype(BF16)

    o_ref[...] = _dot(h_ref[...], w_ref[...])


def _in_proj(x, g, w, tm=1024, tn=1024):
    n = x.shape[0]
    return pl.pallas_call(
        _in_proj_kernel,
        grid=(n // tm, Z_COLS // tn),
        in_specs=[
            pl.BlockSpec((tm, D_MODEL), lambda i, j: (i, 0)),
            pl.BlockSpec((1, D_MODEL), lambda i, j: (0, 0)),
            pl.BlockSpec((D_MODEL, tn), lambda i, j: (0, j)),
        ],
        out_specs=pl.BlockSpec((tm, tn), lambda i, j: (i, j)),
        out_shape=jax.ShapeDtypeStruct((n, Z_COLS), F32),
        scratch_shapes=[pltpu.VMEM((tm, D_MODEL), BF16)],
        compiler_params=_cparams("parallel", "arbitrary"),
        name="in_proj",
    )(x, g, w)


V_ROWS = HEAD_DIM + 16
Q_SCALE = (HEAD_DIM ** -0.5) * math.log2(math.e)


def _attn_prep_kernel(q_ref, k_ref, v_ref, cos_ref, sin_ref, gq_ref, gk_ref, qt_ref, kr_ref, vt_ref):
    tm = q_ref.shape[0]
    cos = cos_ref[...]
    sin = sin_ref[...]
    lane = lax.broadcasted_iota(jnp.int32, (tm, LANES), 1)
    first_half = (lane % HEAD_DIM) < (HEAD_DIM // 2)
    left = lane < HEAD_DIM
    r = lax.broadcasted_iota(jnp.int32, (LANES, LANES), 0) // HEAD_DIM
    c = lax.broadcasted_iota(jnp.int32, (LANES, LANES), 1) // HEAD_DIM
    head_ones = jnp.where(r == c, 1.0, 0.0).astype(BF16)

    def norm_rope(x, g):
        xsq = x * x
        hi, lo = _split2(xsq)
        lo2 = (xsq - hi.astype(F32) - lo.astype(F32)).astype(BF16)
        ssq = _dot(hi, head_ones) + _dot(lo, head_ones) + _dot(lo2, head_ones)
        y = x * lax.rsqrt(ssq * (1.0 / HEAD_DIM) + EPS) * g
        other = jnp.where(first_half, pltpu.roll(y, LANES - HEAD_DIM // 2, 1), pltpu.roll(y, HEAD_DIM // 2, 1))
        return y * cos + other * sin

    gq = gq_ref[...]
    for jb in range(ATT_W // LANES):
        y = norm_rope(q_ref[:, jb * LANES:(jb + 1) * LANES], gq) * Q_SCALE
        ys = pltpu.roll(y, HEAD_DIM, 1)
        if (2 * jb) // GROUP == 0:
            e0 = jnp.where(left, y, 0.0)
            e1 = jnp.where(left, ys, 0.0)
        else:
            e0 = jnp.where(left, 0.0, ys)
            e1 = jnp.where(left, 0.0, y)
        qt_ref[2 * jb] = jnp.transpose(e0).astype(BF16)
        qt_ref[2 * jb + 1] = jnp.transpose(e1).astype(BF16)
    kr_ref[...] = norm_rope(k_ref[...], gk_ref[...]).astype(BF16)
    vt = jnp.transpose(v_ref[...])
    ones = jnp.ones((V_ROWS - HEAD_DIM, tm), BF16)
    for kv in range(N_KV_HEADS):
        vt_ref[kv, 0:HEAD_DIM, :] = vt[kv * HEAD_DIM:(kv + 1) * HEAD_DIM].astype(BF16)
        vt_ref[kv, HEAD_DIM:V_ROWS, :] = ones


def _attn_prep(z, cos_t, sin_t, gq, gk, L, tm=512):
    n = z.shape[0]
    lt = L // tm
    return pl.pallas_call(
        _attn_prep_kernel,
        grid=(n // tm,),
        in_specs=[
            pl.BlockSpec((tm, ATT_W), lambda i: (i, Z_AQ // ATT_W)),
            pl.BlockSpec((tm, KV_W), lambda i: (i, Z_AK // KV_W)),
            pl.BlockSpec((tm, KV_W), lambda i: (i, Z_AV // KV_W)),
            pl.BlockSpec((tm, LANES), lambda i: (i % lt, 0)),
            pl.BlockSpec((tm, LANES), lambda i: (i % lt, 0)),
            pl.BlockSpec((1, LANES), lambda i: (0, 0)),
            pl.BlockSpec((1, LANES), lambda i: (0, 0)),
        ],
        out_specs=[
            pl.BlockSpec((N_HEADS, LANES, tm), lambda i: (0, 0, i)),
            pl.BlockSpec((tm, KV_W), lambda i: (i, 0)),
            pl.BlockSpec((N_KV_HEADS, V_ROWS, tm), lambda i: (0, 0, i)),
        ],
        out_shape=[
            jax.ShapeDtypeStruct((N_HEADS, LANES, n), BF16),
            jax.ShapeDtypeStruct((n, KV_W), BF16),
            jax.ShapeDtypeStruct((N_KV_HEADS, V_ROWS, n), BF16),
        ],
        compiler_params=_cparams("parallel"),
        name="attn_prep",
    )(z, z, z, cos_t, sin_t, gq, gk)


def _flash_kernel(qt_ref, k_ref, vt_ref, o_ref, s_ref, p_ref, *, tk):
    tq = qt_ref.shape[2]
    L = k_ref.shape[0]
    nt = L // tk
    heads = range(N_HEADS)

    def scores(j, slot):
        kj = k_ref[pl.ds(pl.multiple_of(j * tk, tk), tk), :]
        for h in heads:
            s_ref[slot, h] = _dot(kj, qt_ref[h])

    def values(j, slot, alphas, accs):
        start = pl.multiple_of(j * tk, tk)
        vjs = [vt_ref[kv, :, pl.ds(start, tk)] for kv in range(N_KV_HEADS)]
        return [alphas[h] * accs[h] + _dot(vjs[h // GROUP], p_ref[slot, h]) for h in heads]

    scores(0, 0)
    p_ref[1] = jnp.zeros(p_ref.shape[1:], BF16)

    def half_step(jv, js, other, cur, ms, alphas, accs):
        kj = k_ref[pl.ds(pl.multiple_of(js * tk, tk), tk), :]
        start = pl.multiple_of(jv * tk, tk)
        vjs = [vt_ref[kv, :, pl.ds(start, tk)] for kv in range(N_KV_HEADS)]
        ms, alphas, accs = list(ms), list(alphas), list(accs)
        for h in heads:
            accs[h] = alphas[h] * accs[h] + _dot(vjs[h // GROUP], p_ref[other, h])
            s = s_ref[cur, h]
            m_new = jnp.maximum(ms[h], jnp.max(s, axis=0, keepdims=True))
            alphas[h] = jnp.exp2(ms[h] - m_new)
            ms[h] = m_new
            p_ref[cur, h] = jnp.exp2(s - m_new).astype(BF16)
            s_ref[other, h] = _dot(kj, qt_ref[h])
        return ms, alphas, accs

    def body(jj, carry):
        ms, alphas, accs = carry
        j = 2 * jj
        ms, alphas, accs = half_step(jnp.maximum(j - 1, 0), j + 1, 1, 0, ms, alphas, accs)
        return half_step(j, jnp.minimum(j + 2, nt - 1), 0, 1, ms, alphas, accs)

    init = ([jnp.full((1, tq), -jnp.inf, F32)] * N_HEADS, [jnp.ones((1, tq), F32)] * N_HEADS,
            [jnp.zeros((V_ROWS, tq), F32)] * N_HEADS)
    _, alphas, accs = lax.fori_loop(0, nt // 2, body, init)
    accs = values(nt - 1, 1, alphas, accs)
    for h in heads:
        o_ref[h * HEAD_DIM:(h + 1) * HEAD_DIM, :] = accs[h][0:HEAD_DIM] / accs[h][HEAD_DIM:HEAD_DIM + 1]


def _flash(qt, kr, vt, B, L, tq=256, tk=512):
    n = kr.shape[0]
    lt = L // tq
    assert (L // tk) % 2 == 0
    return pl.pallas_call(
        functools.partial(_flash_kernel, tk=tk),
        grid=(B, lt),
        in_specs=[
            pl.BlockSpec((N_HEADS, LANES, tq), lambda b, i: (0, 0, b * lt + i)),
            pl.BlockSpec((L, KV_W), lambda b, i: (b, 0)),
            pl.BlockSpec((N_KV_HEADS, V_ROWS, L), lambda b, i: (0, 0, b)),
        ],
        out_specs=pl.BlockSpec((ATT_W, tq), lambda b, i: (0, b * lt + i)),
        out_shape=jax.ShapeDtypeStruct((ATT_W, n), F32),
        scratch_shapes=[pltpu.VMEM((2, N_HEADS, tk, tq), F32), pltpu.VMEM((2, N_HEADS, tk, tq), BF16)],
        compiler_params=_cparams("parallel", "parallel"),
        name="flash",
    )(qt, kr, vt)


def _hy_pre_kernel(v_ref, vp_ref, vn_ref, a_ref, ap_ref, an_ref, b_ref, bp_ref, bn_ref, w_ref, cb_ref,
                   u_ref, x2_ref, *, lt):
    tm = v_ref.shape[0]
    i = pl.program_id(0)
    first = (i % lt) == 0
    last = (i % lt) == lt - 1
    rows = lax.broadcasted_iota(jnp.int32, v_ref.shape, 0)

    def conv3(m_ref, p_ref, n_ref, part):
        x = m_ref[...]
        prev = jnp.where(first, 0.0, p_ref[7:8, :])
        nxt = jnp.where(last, 0.0, n_ref[0:1, :])
        dn = jnp.where(rows == 0, prev, pltpu.roll(x, 1, 0))
        up = jnp.where(rows == tm - 1, nxt, pltpu.roll(x, tm - 1, 0))
        w = w_ref[part]
        return dn * w[0:1] + x * w[1:2] + up * w[2:3] + cb_ref[part:part + 1, :]

    u_ref[...] = conv3(v_ref, vp_ref, vn_ref, 0) * conv3(a_ref, ap_ref, an_ref, 1)
    x2_ref[...] = conv3(b_ref, bp_ref, bn_ref, 2)


def _hy_pre(z, conv_w, conv_b, L, tm=512, tc=512):
    n = z.shape[0]
    lt = L // tm
    r8 = tm // 8
    nb8 = n // 8
    specs = []
    for part in range(3):
        c0 = (Z_HZ + part * HY_W) // tc
        specs.append(pl.BlockSpec((tm, tc), lambda i, c, c0=c0: (i, c0 + c)))
        specs.append(pl.BlockSpec((8, tc), lambda i, c, c0=c0: (jnp.maximum(i * r8 - 1, 0), c0 + c)))
        specs.append(pl.BlockSpec((8, tc), lambda i, c, c0=c0: (jnp.minimum((i + 1) * r8, nb8 - 1), c0 + c)))
    specs.append(pl.BlockSpec((3, 3, tc), lambda i, c: (0, 0, c)))
    specs.append(pl.BlockSpec((3, tc), lambda i, c: (0, c)))
    return pl.pallas_call(
        functools.partial(_hy_pre_kernel, lt=lt),
        grid=(n // tm, HY_W // tc),
        in_specs=specs,
        out_specs=[pl.BlockSpec((tm, tc), lambda i, c: (i, c))] * 2,
        out_shape=[jax.ShapeDtypeStruct((n, HY_W), F32)] * 2,
        compiler_params=_cparams("parallel", "parallel"),
        name="hy_pre",
    )(*([z] * 9), conv_w, conv_b)


def _hy_filt_kernel(f_ref, w1_ref, b1_ref, f1_ref, w2_ref, b2_ref, f2_ref, w3_ref, dl_ref, kf_ref, ssq_ref):
    feats = f_ref[...]
    half = feats.shape[0] // 2
    halves = (feats[:half], feats[half:])
    h = jnp.sin(f1_ref[...] * (_dot3(jnp.concatenate(halves, axis=1), w1_ref[...]) + b1_ref[...]))
    h = jnp.sin(f2_ref[...] * (_dot3(h, w2_ref[...]) + b2_ref[...]))

    @pl.when(pl.program_id(0) == 0)
    def _():
        ssq_ref[...] = jnp.zeros_like(ssq_ref)

    for i, f in enumerate(halves):
        t = f[:, 0:1]
        valid = f[:, HY_EMB:HY_EMB + 1]
        kf = _dot3(h, w3_ref[i]) * jnp.exp(-t * dl_ref[...]) * valid
        kf_ref[i * half:(i + 1) * half, :] = kf
        ssq_ref[...] += jnp.sum(kf * kf, axis=0, keepdims=True)


def _hy_filt(feats_ext, w1p, b1, f1, w2, b2, f2, w3s, deltas, L, tr=512):
    lt = L // tr
    full = lambda a: pl.BlockSpec(a.shape, lambda i: (0,) * a.ndim)
    return pl.pallas_call(
        _hy_filt_kernel,
        grid=(2 * lt,),
        in_specs=[
            pl.BlockSpec((tr, LANES), lambda i: (i, 0)),
            full(w1p), full(b1), full(f1), full(w2), full(b2), full(f2),
            pl.BlockSpec((None, 2, 2 * HY_FILTER_W, HY_W), lambda i: (i // lt, 0, 0, 0)),
            full(deltas),
        ],
        out_specs=[pl.BlockSpec((tr, HY_W), lambda i: (i, 0)), pl.BlockSpec((1, HY_W), lambda i: (0, 0))],
        out_shape=[jax.ShapeDtypeStruct((2 * L, HY_W), F32), jax.ShapeDtypeStruct((1, HY_W), F32)],
        compiler_params=_cparams("arbitrary"),
        name="hy_filt",
    )(feats_ext, w1p, b1, f1, w2, b2, f2, w3s, deltas)


def _to_slabs(src_ref, slab_ref):
    src = src_ref.reshape(slab_ref.shape[1], src_ref.shape[-1])
    for c in range(slab_ref.shape[0]):
        slab_ref[c] = src[:, c * LANES:(c + 1) * LANES]


def _fft1_kernel(x_ref, f_ref, o_ref, xs_ref, os_ref):
    hin, rows = x_ref.shape[0], o_ref.shape[0]
    _to_slabs(x_ref, xs_ref)
    o2 = o_ref.reshape(rows * FFT_SUB, o_ref.shape[-1])
    f = f_ref[...]
    for c in range(xs_ref.shape[0]):
        for p in range(0, FFT_SUB, 2):
            x = jnp.concatenate([xs_ref[c, pl.ds(p, hin, stride=FFT_SUB), :],
                                 xs_ref[c, pl.ds(p + 1, hin, stride=FFT_SUB), :]], axis=1)
            xh, xl = _split2(x)
            y = _dot(f, jnp.concatenate([xh, xl, xh], axis=0))
            os_ref[c, pl.ds(p, rows, stride=FFT_SUB), :] = y[:, :LANES]
            os_ref[c, pl.ds(p + 1, rows, stride=FFT_SUB), :] = y[:, LANES:]
        o2[:, c * LANES:(c + 1) * LANES] = os_ref[c]


def _fft1(x, fcat, B, hin):
    C = x.shape[-1]
    rows = fcat.shape[0]
    nj = FFT_N2 // FFT_SUB
    out = pl.pallas_call(
        _fft1_kernel,
        grid=(B, nj),
        in_specs=[
            pl.BlockSpec((None, hin, None, FFT_SUB, C), lambda b, j: (b, 0, j, 0, 0)),
            pl.BlockSpec(fcat.shape, lambda b, j: (0, 0)),
        ],
        out_specs=pl.BlockSpec((None, rows, None, FFT_SUB, C), lambda b, j: (b, 0, j, 0, 0)),
        out_shape=jax.ShapeDtypeStruct((B, rows, nj, FFT_SUB, C), F32),
        scratch_shapes=[pltpu.VMEM((C // LANES, hin * FFT_SUB, LANES), F32),
                        pltpu.VMEM((C // LANES, rows * FFT_SUB, LANES), F32)],
        compiler_params=_cparams("parallel", "parallel"),
        name="fft1",
    )(x.reshape(B, hin, nj, FFT_SUB, C), fcat)
    return out.reshape(B, rows, FFT_N2, C)


def _fft2_kernel(ar_ref, ai_ref, tr_ref, ti_ref, m_ref, o_ref):
    m = m_ref[...]
    for r in range(FFT_KB):
        ar, ai = ar_ref[r], ai_ref[r]
        tr, ti = (jnp.tile(t[r], (1, FFT_CT // LANES)) for t in (tr_ref, ti_ref))
        b = jnp.concatenate([ar * tr - ai * ti, ar * ti + ai * tr], axis=0)
        bh, bl = _split2(b)
        o_ref[r] = _dot(m, jnp.concatenate([bh, bl, bh], axis=0))


def _fft2(a, tw_r, tw_i, mcat, kp):
    B = a.shape[0]
    C = a.shape[-1]
    nk = kp // FFT_KB
    return pl.pallas_call(
        _fft2_kernel,
        grid=(B, C // FFT_CT, nk),
        in_specs=[
            pl.BlockSpec((None, FFT_KB, FFT_N2, FFT_CT), lambda b, c, k: (b, k, 0, c)),
            pl.BlockSpec((None, FFT_KB, FFT_N2, FFT_CT), lambda b, c, k: (b, nk + k, 0, c)),
            pl.BlockSpec((FFT_KB, FFT_N2, LANES), lambda b, c, k: (k, 0, 0)),
            pl.BlockSpec((FFT_KB, FFT_N2, LANES), lambda b, c, k: (k, 0, 0)),
            pl.BlockSpec(mcat.shape, lambda b, c, k: (0, 0)),
        ],
        out_specs=pl.BlockSpec((None, FFT_KB, 2 * FFT_N2, FFT_CT), lambda b, c, k: (b, k, 0, c)),
        out_shape=jax.ShapeDtypeStruct((B, kp, 2 * FFT_N2, C), F32),
        compiler_params=_cparams("parallel", "parallel", "parallel"),
        name="fft2",
    )(a, a, tw_r, tw_i, mcat)


def _ifft2_kernel(x_ref, k_ref, ssq_ref, tr_ref, ti_ref, m_ref, o_ref):
    m = m_ref[...]
    scale = lax.rsqrt(ssq_ref[...] + EPS)
    n2 = FFT_N2
    for r in range(FFT_KB):
        x = x_ref[r]
        kf = k_ref[r]
        xr, xi = x[:n2], x[n2:]
        kr, ki = kf[:n2] * scale, kf[n2:] * scale
        p = jnp.concatenate([xr * kr - xi * ki, xr * ki + xi * kr], axis=0)
        ph, plo = _split2(p)
        c = _dot(m, jnp.concatenate([ph, plo, ph], axis=0))
        cr, ci = c[:n2], c[n2:]
        tr, ti = (jnp.tile(t[r], (1, FFT_CT // LANES)) for t in (tr_ref, ti_ref))
        o_ref[0, r] = cr * tr + ci * ti
        o_ref[1, r] = ci * tr - cr * ti


def _ifft2(xs, ks, ssq, tw_r, tw_i, mcat_inv):
    B, kp, _, C = xs.shape
    nk = kp // FFT_KB
    return pl.pallas_call(
        _ifft2_kernel,
        grid=(B, C // FFT_CT, nk),
        in_specs=[
            pl.BlockSpec((None, FFT_KB, 2 * FFT_N2, FFT_CT), lambda b, c, k: (b, k, 0, c)),
            pl.BlockSpec((None, FFT_KB, 2 * FFT_N2, FFT_CT), lambda b, c, k: (0, k, 0, c)),
            pl.BlockSpec((1, FFT_CT), lambda b, c, k: (0, c)),
            pl.BlockSpec((FFT_KB, FFT_N2, LANES), lambda b, c, k: (k, 0, 0)),
            pl.BlockSpec((FFT_KB, FFT_N2, LANES), lambda b, c, k: (k, 0, 0)),
            pl.BlockSpec(mcat_inv.shape, lambda b, c, k: (0, 0)),
        ],
        out_specs=pl.BlockSpec((None, 2, FFT_KB, FFT_N2, FFT_CT), lambda b, c, k: (b, 0, k, 0, c)),
        out_shape=jax.ShapeDtypeStruct((B, 2, kp, FFT_N2, C), F32),
        compiler_params=_cparams("parallel", "parallel", "parallel"),
        name="ifft2",
    )(xs, ks, ssq, tw_r, tw_i, mcat_inv)


def _ifft1_kernel(d_ref, w_ref, u_ref, x2_ref, skip_ref, o_ref, ds_ref, ys_ref):
    rows, h = d_ref.shape[0], o_ref.shape[0]
    C = o_ref.shape[-1]
    _to_slabs(d_ref, ds_ref)
    u2, x22, o2 = (r.reshape(h * FFT_SUB, C) for r in (u_ref, x2_ref, o_ref))
    w = w_ref[...]
    for c in range(ds_ref.shape[0]):
        lanes = slice(c * LANES, (c + 1) * LANES)
        for p in range(0, FFT_SUB, 2):
            d = jnp.concatenate([ds_ref[c, pl.ds(p, rows, stride=FFT_SUB), :],
                                 ds_ref[c, pl.ds(p + 1, rows, stride=FFT_SUB), :]], axis=1)
            dh, dl = _split2(d)
            y = _dot(w, jnp.concatenate([dh, dl, dh], axis=0))
            ys_ref[c, pl.ds(p, h, stride=FFT_SUB), :] = y[:, :LANES]
            ys_ref[c, pl.ds(p + 1, h, stride=FFT_SUB), :] = y[:, LANES:]
        o2[:, lanes] = x22[:, lanes] * (ys_ref[c] + u2[:, lanes] * skip_ref[:, lanes])


def _ifft1(d, wcat, u, x2c, skip, B):
    rows, C = d.shape[1], d.shape[-1]
    h = wcat.shape[0]
    nj = FFT_N2 // FFT_SUB
    seq = pl.BlockSpec((None, h, None, FFT_SUB, C), lambda b, j: (b, 0, j, 0, 0))
    out = pl.pallas_call(
        _ifft1_kernel,
        grid=(B, nj),
        in_specs=[
            pl.BlockSpec((None, rows, None, FFT_SUB, C), lambda b, j: (b, 0, j, 0, 0)),
            pl.BlockSpec(wcat.shape, lambda b, j: (0, 0)),
            seq, seq,
            pl.BlockSpec((1, C), lambda b, j: (0, 0)),
        ],
        out_specs=seq,
        out_shape=jax.ShapeDtypeStruct((B, h, nj, FFT_SUB, C), F32),
        scratch_shapes=[pltpu.VMEM((C // LANES, rows * FFT_SUB, LANES), F32),
                        pltpu.VMEM((C // LANES, h * FFT_SUB, LANES), F32)],
        compiler_params=_cparams("parallel", "parallel"),
        name="ifft1",
    )(d.reshape(B, rows, nj, FFT_SUB, C), wcat, u.reshape(B, h, nj, FFT_SUB, C), x2c.reshape(B, h, nj, FFT_SUB, C),
      skip)
    return out.reshape(B * h * FFT_N2, C)


def _hilo_cat(m):
    hi = m.astype(BF16)
    lo = (m - hi.astype(F32)).astype(BF16)
    return jnp.concatenate([hi, hi, lo], axis=1)


def _fft_tables(L):
    n = 2 * L
    n2 = FFT_N2
    n1 = n // n2
    h = n1 // 2
    kp = h + FFT_KB
    k1 = jnp.arange(kp, dtype=jnp.int32)

    def outer(hin):
        nn = jnp.arange(hin, dtype=jnp.int32)
        ang = (2.0 * math.pi / n1) * ((k1[:, None] * nn[None, :]) % n1).astype(F32)
        return _hilo_cat(jnp.concatenate([jnp.cos(ang), -jnp.sin(ang)], axis=0))

    f_data = outer(h)
    f_filt = outer(n1)
    j = jnp.arange(n2, dtype=jnp.int32)
    ang2 = (2.0 * math.pi / n2) * ((j[:, None] * j[None, :]) % n2).astype(F32)
    c2, s2 = jnp.cos(ang2), jnp.sin(ang2)
    m_fwd = _hilo_cat(jnp.block([[c2, s2], [-s2, c2]]))
    m_inv = _hilo_cat(jnp.block([[c2, -s2], [s2, c2]]))
    angt = (2.0 * math.pi / n) * (k1[:, None] * j[None, :]).astype(F32)
    tw_r = jnp.broadcast_to(jnp.cos(angt)[:, :, None], (kp, n2, LANES))
    tw_i = jnp.broadcast_to(-jnp.sin(angt)[:, :, None], (kp, n2, LANES))
    wgt = jnp.where((k1 == 0) | (k1 == h), 1.0, jnp.where(k1 < h, 2.0, 0.0)).astype(F32) / n
    nn = jnp.arange(h, dtype=jnp.int32)
    angi = (2.0 * math.pi / n1) * ((nn[:, None] * k1[None, :]) % n1).astype(F32)
    w_inv = _hilo_cat(jnp.concatenate([jnp.cos(angi) * wgt, -jnp.sin(angi) * wgt], axis=1))
    return dict(n1=n1, h=h, kp=kp, f_data=f_data, f_filt=f_filt, m_fwd=m_fwd, m_inv=m_inv,
                tw_r=tw_r, tw_i=tw_i, w_inv=w_inv)


def _filter_feats(L):
    i = jnp.arange(2 * L, dtype=jnp.int32)
    n = jnp.where(i < L, i, 2 * L - i).astype(F32)
    t = (n / (L - 1))[:, None]
    w = 2.0 * math.pi * n / L
    fb = jnp.linspace(1e-4, HY_BANDS - 1, HY_BANDS, dtype=F32)
    ph = w[:, None] * fb
    valid = jnp.where(i == L, 0.0, 1.0).astype(F32)[:, None]
    ext = jnp.concatenate([t, jnp.cos(ph), -jnp.sin(ph), valid], axis=-1)
    return jnp.pad(ext, ((0, 0), (0, LANES - HY_EMB - 1)))


def _hyena_spectrum(tabs, feats_ext, lp, L):
    kf, ssq = _hy_filt(feats_ext, lp["hy_w1p"], lp["hy_b1"], lp["hy_f1"], lp["hy_w2"], lp["hy_b2"], lp["hy_f2"],
                       lp["hy_w3s"], lp["hy_deltas"], L)
    a = _fft1(kf, tabs["f_filt"], 1, tabs["n1"])
    ks = _fft2(a, tabs["tw_r"], tabs["tw_i"], tabs["m_fwd"], tabs["kp"])
    return ks, ssq


def _hyena(z, tabs, ks, ssq, lp, B, L):
    u, x2c = _hy_pre(z, lp["hy_conv_w"], lp["hy_conv_b"], L)
    kp = tabs["kp"]
    a = _fft1(u, tabs["f_data"], B, tabs["h"])
    xs = _fft2(a, tabs["tw_r"], tabs["tw_i"], tabs["m_fwd"], kp)
    d = _ifft2(xs, ks, ssq, tabs["tw_r"], tabs["tw_i"], tabs["m_inv"])
    return _ifft1(d.reshape(B, 2 * kp, FFT_N2, HY_W), tabs["w_inv"], u, x2c, lp["hy_skip"], B)


def _gla_kernel(qf_ref, kf_ref, vf_ref, lf_ref, qb_ref, kb_ref, vb_ref, lb_ref, gup_ref, gb_ref,
                of_ref, ob_ref, sf_ref, sb_ref, *, nchunk):
    C = GLA_CHUNK
    R = nchunk * C

    @pl.when(pl.program_id(1) == 0)
    def _():
        sf_ref[...] = jnp.zeros_like(sf_ref)
        sb_ref[...] = jnp.zeros_like(sb_ref)

    rr = lax.broadcasted_iota(jnp.int32, (R, R), 0)
    cc = lax.broadcasted_iota(jnp.int32, (R, R), 1)
    same_chunk = (rr // C) == (cc // C)
    lane = lax.broadcasted_iota(jnp.int32, (C, GLA_K), 1) // GLA_DK
    ri4 = lax.broadcasted_iota(jnp.int32, (GLA_HEADS * C, C), 0) % C
    ci4 = lax.broadcasted_iota(jnp.int32, (GLA_HEADS * C, C), 1)

    def stack_heads(x):
        return jnp.concatenate([jnp.where(lane == h, x, 0.0) for h in range(GLA_HEADS)], axis=0).astype(BF16)

    def per_chunk_rows(b, row):
        return jnp.concatenate([jnp.broadcast_to(b[c * C + row:c * C + row + 1], (C, GLA_K)) for c in range(nchunk)],
                               axis=0)

    dirs = []
    for d, (q_ref, k_ref, v_ref, l_ref) in enumerate(((qf_ref, kf_ref, vf_ref, lf_ref),
                                                       (qb_ref, kb_ref, vb_ref, lb_ref))):
        logit = _dot3(l_ref[...], gup_ref[:, d * GLA_K:(d + 1) * GLA_K]) + gb_ref[:, d * GLA_K:(d + 1) * GLA_K]
        la = (jnp.minimum(logit, 0.0) - jnp.log(1.0 + jnp.exp(-jnp.abs(logit)))) * (1.0 / GLA_TAU)
        if d == 0:
            tri = jnp.where(same_chunk & (rr >= cc), 1.0, 0.0).astype(BF16)
            keep = ri4 >= ci4
            mid, last = C // 2 - 1, C - 1
            order = list(range(nchunk))
        else:
            tri = jnp.where(same_chunk & (cc >= rr), 1.0, 0.0).astype(BF16)
            keep = ci4 > ri4
            mid, last = C // 2, 0
            order = list(range(nchunk - 1, -1, -1))
        hi, lo = _split2(la)
        lo2 = (la - hi.astype(F32) - lo.astype(F32)).astype(BF16)
        b = _dot(tri, hi) + _dot(tri, lo) + _dot(tri, lo2)
        b_mid = per_chunk_rows(b, mid)
        b_last = per_chunk_rows(b, last)
        q = q_ref[...] * (GLA_DK ** -0.5)
        k = k_ref[...]
        dirs.append(dict(
            keep=keep, order=order, v=v_ref[...].astype(BF16),
            qt=q * jnp.exp(b - b_mid), kt=(k * jnp.exp(b_mid - b)).astype(BF16),
            kh=k * jnp.exp(b_last - b), qh=q * jnp.exp(b), dec=jnp.exp(b_last)))

    for dd in dirs:
        dd["a"] = []
        for c in range(nchunk):
            rs = slice(c * C, (c + 1) * C)
            a = lax.dot_general(stack_heads(dd["qt"][rs]), dd["kt"][rs], (((1,), (1,)), ((), ())),
                                preferred_element_type=F32)
            dd["a"].append(jnp.where(dd["keep"], a, 0.0).astype(BF16))

    for dd in dirs:
        dd["o"], dd["u"] = [], []
        for c in range(nchunk):
            rs = slice(c * C, (c + 1) * C)
            k_t = jnp.transpose(dd["kh"][rs]).astype(BF16)
            outs, ups = [], []
            for h in range(GLA_HEADS):
                vh = dd["v"][rs, h * GLA_DV:(h + 1) * GLA_DV]
                outs.append(_dot(dd["a"][c][h * C:(h + 1) * C], vh))
                ups.append(_dot(k_t[h * GLA_DK:(h + 1) * GLA_DK], vh))
            dd["o"].append(outs)
            dd["u"].append(jnp.concatenate(ups, axis=0))

    for dd, s_ref in zip(dirs, (sf_ref, sb_ref)):
        s = s_ref[...]
        dd["s_prev"] = {}
        for c in dd["order"]:
            dd["s_prev"][c] = s.astype(BF16)
            decay = jnp.transpose(jnp.broadcast_to(dd["dec"][c * C:c * C + 1], (GLA_DV, GLA_K)))
            s = decay * s + dd["u"][c]
        s_ref[...] = s

    for dd, o_ref in zip(dirs, (of_ref, ob_ref)):
        for c in range(nchunk):
            rs = slice(c * C, (c + 1) * C)
            o_inter = _dot(stack_heads(dd["qh"][rs]), dd["s_prev"][c])
            o_ref[rs, :] = jnp.concatenate(
                [dd["o"][c][h] + o_inter[h * C:(h + 1) * C] for h in range(GLA_HEADS)], axis=1)


def _gla(z, gup, gb, B, L, nchunk=4):
    n = z.shape[0]
    R = nchunk * GLA_CHUNK
    T = L // R
    fwd = lambda b, t: b * T + t
    bwd = lambda b, t: b * T + (T - 1 - t)

    def specs(row):
        return [
            pl.BlockSpec((R, GLA_K), lambda b, t: (row(b, t), Z_GQ // GLA_K)),
            pl.BlockSpec((R, GLA_K), lambda b, t: (row(b, t), Z_GK // GLA_K)),
            pl.BlockSpec((R, GLA_V), lambda b, t: (row(b, t), Z_GV // GLA_V)),
            pl.BlockSpec((R, LANES), lambda b, t: (row(b, t), Z_GLOW // LANES)),
        ]

    return pl.pallas_call(
        functools.partial(_gla_kernel, nchunk=nchunk),
        grid=(B, T),
        in_specs=specs(fwd) + specs(bwd) + [
            pl.BlockSpec(gup.shape, lambda b, t: (0, 0)),
            pl.BlockSpec(gb.shape, lambda b, t: (0, 0)),
        ],
        out_specs=[
            pl.BlockSpec((R, GLA_V), lambda b, t: (fwd(b, t), 0)),
            pl.BlockSpec((R, GLA_V), lambda b, t: (bwd(b, t), 0)),
        ],
        out_shape=[jax.ShapeDtypeStruct((n, GLA_V), F32)] * 2,
        scratch_shapes=[pltpu.VMEM((GLA_K, GLA_DV), F32)] * 2,
        compiler_params=_cparams("parallel", "arbitrary"),
        name="gla",
    )(*([z] * 8), gup, gb)


def _mix_out_kernel(x_ref, ya_ref, yh_ref, of_ref, ob_ref, og_ref, g0_ref, g1_ref, g2_ref,
                    wb_ref, wo_ref, gn_ref, o_ref):
    o = of_ref[...] + ob_ref[...]
    og = og_ref[...]
    gn = gn_ref[...]
    parts = []
    for h in range(GLA_HEADS):
        blk = o[:, h * GLA_DV:(h + 1) * GLA_DV]
        ms = jnp.mean(blk * blk, axis=-1, keepdims=True)
        gate = og[:, h * GLA_DV:(h + 1) * GLA_DV]
        parts.append(blk * lax.rsqrt(ms + EPS) * gn * (gate * jax.nn.sigmoid(gate)))
    y_gla = jnp.concatenate(parts, axis=1)
    proj_att = lax.dot_general(ya_ref[...].astype(BF16), wb_ref[0], (((0,), (0,)), ((), ())),
                               preferred_element_type=F32)
    merged = jax.nn.sigmoid(g0_ref[...]) * proj_att
    merged += jax.nn.sigmoid(g1_ref[...]) * _dot(yh_ref[...].astype(BF16), wb_ref[1])
    merged += jax.nn.sigmoid(g2_ref[...]) * _dot(y_gla.astype(BF16), wb_ref[2])
    o_ref[...] = x_ref[...] + _dot(merged.astype(BF16), wo_ref[...])


def _mix_out(x, y_att, y_hy, o_f, o_b, z, wb, wo, gn, tm=256):
    n = x.shape[0]
    row = lambda w: pl.BlockSpec((tm, w), lambda i: (i, 0))
    return pl.pallas_call(
        _mix_out_kernel,
        grid=(n // tm,),
        in_specs=[
            row(D_MODEL), pl.BlockSpec((MIX_W, tm), lambda i: (0, i)), row(MIX_W), row(MIX_W), row(MIX_W),
            pl.BlockSpec((tm, GLA_V), lambda i: (i, Z_GOG // GLA_V)),
            pl.BlockSpec((tm, D_MODEL), lambda i: (i, 0)),
            pl.BlockSpec((tm, D_MODEL), lambda i: (i, 1)),
            pl.BlockSpec((tm, D_MODEL), lambda i: (i, 2)),
            pl.BlockSpec(wb.shape, lambda i: (0, 0, 0)),
            pl.BlockSpec(wo.shape, lambda i: (0, 0)),
            pl.BlockSpec((1, GLA_DV), lambda i: (0, 0)),
        ],
        out_specs=row(D_MODEL),
        out_shape=jax.ShapeDtypeStruct((n, D_MODEL), F32),
        compiler_params=_cparams("parallel"),
        name="mix_out",
    )(x, y_att, y_hy, o_f, o_b, z, z, z, z, wb, wo, gn)


def _ffn_kernel(x_ref, g_ref, wg_ref, wu_ref, wd_ref, o_ref, h_ref, acc_ref):
    j = pl.program_id(1)

    @pl.when(j == 0)
    def _():
        x = x_ref[...]
        ms = jnp.mean(x * x, axis=-1, keepdims=True)
        h_ref[...] = (x * lax.rsqrt(ms + EPS) * g_ref[...]).astype(BF16)
        acc_ref[...] = jnp.zeros_like(acc_ref)

    h = h_ref[...]
    a = _dot(h, wg_ref[...])
    act = (a * jax.nn.sigmoid(a)) * _dot(h, wu_ref[...])
    acc_ref[...] += _dot(act.astype(BF16), wd_ref[...])

    @pl.when(j == pl.num_programs(1) - 1)
    def _():
        o_ref[...] = x_ref[...] + acc_ref[...]


def _ffn(x, g, wg, wu, wd, tm=512, tf=1408):
    n = x.shape[0]
    return pl.pallas_call(
        _ffn_kernel,
        grid=(n // tm, D_FF // tf),
        in_specs=[
            pl.BlockSpec((tm, D_MODEL), lambda i, j: (i, 0)),
            pl.BlockSpec((1, D_MODEL), lambda i, j: (0, 0)),
            pl.BlockSpec((D_MODEL, tf), lambda i, j: (0, j)),
            pl.BlockSpec((D_MODEL, tf), lambda i, j: (0, j)),
            pl.BlockSpec((tf, D_MODEL), lambda i, j: (j, 0)),
        ],
        out_specs=pl.BlockSpec((tm, D_MODEL), lambda i, j: (i, 0)),
        out_shape=jax.ShapeDtypeStruct((n, D_MODEL), F32),
        scratch_shapes=[pltpu.VMEM((tm, D_MODEL), BF16), pltpu.VMEM((tm, D_MODEL), F32)],
        compiler_params=_cparams("parallel", "arbitrary"),
        name="ffn",
    )(x, g, wg, wu, wd)


def _final_norm_kernel(x_ref, g_ref, o_ref):
    x = x_ref[...]
    ms = jnp.mean(x * x, axis=-1, keepdims=True)
    o_ref[...] = x * lax.rsqrt(ms + EPS) * g_ref[...]


def _final_norm(x, g, tm=512):
    n = x.shape[0]
    return pl.pallas_call(
        _final_norm_kernel,
        grid=(n // tm,),
        in_specs=[pl.BlockSpec((tm, D_MODEL), lambda i: (i, 0)), pl.BlockSpec((1, D_MODEL), lambda i: (0, 0))],
        out_specs=pl.BlockSpec((tm, D_MODEL), lambda i: (i, 0)),
        out_shape=jax.ShapeDtypeStruct((n, D_MODEL), F32),
        compiler_params=_cparams("parallel"),
        name="final_norm",
    )(x, g)


def _rope_tables(L):
    rows = L // GRID_W
    r = jnp.repeat(jnp.arange(rows, dtype=F32), GRID_W)
    c = jnp.tile(jnp.arange(GRID_W, dtype=F32), rows)
    inv = ROPE_THETA ** (-jnp.arange(0, ROPE_AXIS_DIM, 2, dtype=F32) / ROPE_AXIS_DIM)
    ang = jnp.concatenate([r[:, None] * inv, c[:, None] * inv], axis=-1)
    cos, sin = jnp.cos(ang), jnp.sin(ang)
    cos_t = jnp.tile(cos, (1, 2 * LANES // HEAD_DIM))
    sin_t = jnp.tile(jnp.concatenate([-sin, sin], axis=-1), (1, LANES // HEAD_DIM))
    return cos_t, sin_t


def _layer_params(l, norm_mix_g, w_in, q_norm_g, k_norm_g, hy_conv_w, hy_conv_b, hy_w1, hy_b1, hy_f1, hy_w2,
                  hy_b2, hy_f2, hy_w3, hy_skip, gla_gate_up, gla_gate_b, gla_norm_g, w_branch, w_out, norm_ffn_g,
                  w_ffn_gate, w_ffn_up, w_ffn_down):
    w = w_in[l]
    offs = {}
    off = 0
    for name, size in (("aq", ATT_W), ("ak", KV_W), ("av", KV_W), ("hz", 3 * HY_W), ("gq", GLA_K), ("gk", GLA_K),
                       ("gv", GLA_V), ("gog", GLA_V), ("glow", 2 * GLA_RANK), ("gates", N_BRANCH * D_MODEL)):
        offs[name] = w[:, off:off + size]
        off += size
    w_cat = jnp.concatenate(
        [offs["gates"], offs["aq"], offs["gv"], offs["gog"], offs["hz"], offs["gq"], offs["gk"], offs["ak"],
         offs["av"], offs["glow"], jnp.zeros((D_MODEL, Z_COLS - Z_GLOW - 2 * GLA_RANK), F32)], axis=1).astype(BF16)
    gup = jnp.zeros((LANES, 2 * GLA_K), F32)
    gup = gup.at[0:GLA_RANK, 0:GLA_K].set(gla_gate_up[l, 0])
    gup = gup.at[GLA_RANK:2 * GLA_RANK, GLA_K:2 * GLA_K].set(gla_gate_up[l, 1])
    deltas = jnp.linspace(abs(math.log(HY_TARGET) / HY_SLOW), abs(math.log(HY_TARGET) / HY_FAST), HY_W, dtype=F32)
    return dict(
        norm_mix_g=norm_mix_g[l][None, :],
        w_cat=w_cat,
        gq=jnp.tile(q_norm_g[l], LANES // HEAD_DIM)[None, :],
        gk=jnp.tile(k_norm_g[l], LANES // HEAD_DIM)[None, :],
        hy_conv_w=jnp.transpose(hy_conv_w[l].reshape(3, 3, HY_W), (1, 0, 2)),
        hy_conv_b=hy_conv_b[l].reshape(3, HY_W),
        hy_w1p=jnp.kron(jnp.eye(2, dtype=F32), jnp.pad(hy_w1[l], ((0, LANES - HY_EMB), (0, 0)))),
        hy_b1=jnp.tile(hy_b1[l], 2)[None, :], hy_f1=jnp.tile(hy_f1[l], 2)[None, :],
        hy_w2=jnp.kron(jnp.eye(2, dtype=F32), hy_w2[l]),
        hy_b2=jnp.tile(hy_b2[l], 2)[None, :], hy_f2=jnp.tile(hy_f2[l], 2)[None, :],
        hy_w3s=jnp.einsum("gh,kfc->fghkc", jnp.eye(2, dtype=F32), hy_w3[l].reshape(HY_FILTER_W, 2, HY_W)
                          ).reshape(2, 2, 2 * HY_FILTER_W, HY_W),
        hy_deltas=deltas[None, :],
        hy_skip=hy_skip[l][None, :],
        gup=gup, gb=gla_gate_b[l].reshape(1, 2 * GLA_K),
        gn=gla_norm_g[l][None, :],
        wb=w_branch[l].astype(BF16), wo=w_out[l].astype(BF16),
        norm_ffn_g=norm_ffn_g[l][None, :],
        wg=w_ffn_gate[l].astype(BF16), wu=w_ffn_up[l].astype(BF16), wd=w_ffn_down[l].astype(BF16),
    )


def _encoder_layer(x, lp, B, L, rope, tabs, ks, ssq):
    z = _in_proj(x, lp["norm_mix_g"], lp["w_cat"])
    qt, kr, vt = _attn_prep(z, rope[0], rope[1], lp["gq"], lp["gk"], L)
    y_att = _flash(qt, kr, vt, B, L)
    y_hy = _hyena(z, tabs, ks, ssq, lp, B, L)
    o_f, o_b = _gla(z, lp["gup"], lp["gb"], B, L)
    x = _mix_out(x, y_att, y_hy, o_f, o_b, z, lp["wb"], lp["wo"], lp["gn"])
    return _ffn(x, lp["norm_ffn_g"], lp["wg"], lp["wu"], lp["wd"])


def kernel(x_prompt, x_sample, norm_mix_g, w_in, q_norm_g, k_norm_g, hy_conv_w, hy_conv_b, hy_w1, hy_b1, hy_f1, hy_w2, hy_b2, hy_f2, hy_w3, hy_skip, gla_gate_up, gla_gate_b, gla_norm_g, w_branch, w_out, norm_ffn_g, w_ffn_gate, w_ffn_up, w_ffn_down, final_norm_g):
    streams = []
    for xin in (x_prompt, x_sample):
        B, L, _ = xin.shape
        streams.append(dict(x=xin.reshape(B * L, D_MODEL), B=B, L=L, rope=_rope_tables(L), tabs=_fft_tables(L),
                            feats=_filter_feats(L)))
    for l in range(DEPTH):
        lp = _layer_params(l, norm_mix_g, w_in, q_norm_g, k_norm_g, hy_conv_w, hy_conv_b, hy_w1, hy_b1, hy_f1,
                           hy_w2, hy_b2, hy_f2, hy_w3, hy_skip, gla_gate_up, gla_gate_b, gla_norm_g, w_branch,
                           w_out, norm_ffn_g, w_ffn_gate, w_ffn_up, w_ffn_down)
        for s in streams:
            ks, ssq = _hyena_spectrum(s["tabs"], s["feats"], lp, s["L"])
            s["x"] = _encoder_layer(s["x"], lp, s["B"], s["L"], s["rope"], s["tabs"], ks, ssq)
    outs = []
    for s, xin in zip(streams, (x_prompt, x_sample)):
        outs.append(_final_norm(s["x"], final_norm_g[None, :]).reshape(xin.shape))
    return tuple(outs)
```

```python
import functools
import math

import jax
import jax.numpy as jnp
from jax import lax
from jax.experimental import pallas as pl
from jax.experimental.pallas import tpu as pltpu

F32 = jnp.float32
BF16 = jnp.bfloat16

D_MODEL = 1024
DEPTH = 4
GRID_W = 64
N_HEADS = 8
N_KV_HEADS = 2
HEAD_DIM = 64
GROUP = N_HEADS // N_KV_HEADS
ROPE_THETA = 10000.0
ROPE_AXIS_DIM = HEAD_DIM // 2
ATT_W = N_HEADS * HEAD_DIM
KV_W = N_KV_HEADS * HEAD_DIM
HY_W = D_MODEL // 2
HY_EMB = 33
HY_BANDS = (HY_EMB - 1) // 2
HY_FILTER_W = 64
HY_TARGET = 1e-2
HY_FAST = 0.3
HY_SLOW = 1.5
GLA_HEADS = 4
GLA_DK = 64
GLA_DV = 128
GLA_K = GLA_HEADS * GLA_DK
GLA_V = GLA_HEADS * GLA_DV
GLA_RANK = 16
GLA_TAU = 16.0
GLA_CHUNK = 64
N_BRANCH = 3
MIX_W = 512
D_FF = 2816
EPS = 1e-6

LANES = 128
VMEM_LIMIT = 56 * 1024 * 1024

Z_AQ = 0
Z_GV = 512
Z_GOG = 1024
Z_HZ = 1536
Z_GQ = 3072
Z_GK = 3328
Z_AK = 3584
Z_AV = 3712
Z_GLOW = 3840
Z_COLS = 4096

FFT_N2 = 128
FFT_KB = 8
FFT_SUB = 8
FFT_CT = 512


def _cparams(*sem):
    return pltpu.CompilerParams(dimension_semantics=sem, vmem_limit_bytes=VMEM_LIMIT)


def _split2(x):
    hi = x.astype(BF16)
    lo = (x - hi.astype(F32)).astype(BF16)
    return hi, lo


def _dot(a, b):
    return jnp.dot(a, b, preferred_element_type=F32)


def _dot3(a, b):
    ah, al = _split2(a)
    bh, bl = _split2(b)
    return _dot(ah, bh) + _dot(al, bh) + _dot(ah, bl)


def _in_proj_kernel(x_ref, g_ref, w_ref, o_ref, h_ref):
    @pl.when(pl.program_id(1) == 0)
    def _():
        x = x_ref[...]
        ms = jnp.mean(x * x, axis=-1, keepdims=True)
        h_ref[...] = (x * lax.rsqrt(ms + EPS) * g_ref[...]).astype(BF16)

    o_ref[...] = _dot(h_ref[...], w_ref[...]).astype(o_ref.dtype)


def _in_proj(x, g, w, out_dtype, tm=1024, tn=1024):
    n = x.shape[0]
    cols = w.shape[1]
    return pl.pallas_call(
        _in_proj_kernel,
        grid=(n // tm, cols // tn),
        in_specs=[
            pl.BlockSpec((tm, D_MODEL), lambda i, j: (i, 0)),
            pl.BlockSpec((1, D_MODEL), lambda i, j: (0, 0)),
            pl.BlockSpec((D_MODEL, tn), lambda i, j: (0, j)),
        ],
        out_specs=pl.BlockSpec((tm, tn), lambda i, j: (i, j)),
        out_shape=jax.ShapeDtypeStruct((n, cols), out_dtype),
        scratch_shapes=[pltpu.VMEM((tm, D_MODEL), BF16)],
        compiler_params=_cparams("parallel", "arbitrary"),
        name="in_proj",
    )(x, g, w)


V_ROWS = HEAD_DIM + 16
Q_SCALE = (HEAD_DIM ** -0.5) * math.log2(math.e)


def _attn_prep_kernel(q_ref, k_ref, v_ref, cos_ref, sin_ref, gq_ref, gk_ref, qt_ref, kr_ref, vt_ref):
    tm = q_ref.shape[0]
    cos = cos_ref[...]
    sin = sin_ref[...]
    lane = lax.broadcasted_iota(jnp.int32, (tm, LANES), 1)
    first_half = (lane % HEAD_DIM) < (HEAD_DIM // 2)
    left = lane < HEAD_DIM
    r = lax.broadcasted_iota(jnp.int32, (LANES, LANES), 0) // HEAD_DIM
    c = lax.broadcasted_iota(jnp.int32, (LANES, LANES), 1) // HEAD_DIM
    head_ones = jnp.where(r == c, 1.0, 0.0).astype(BF16)

    def norm_rope(x, g):
        xsq = x * x
        hi, lo = _split2(xsq)
        lo2 = (xsq - hi.astype(F32) - lo.astype(F32)).astype(BF16)
        ssq = _dot(hi, head_ones) + _dot(lo, head_ones) + _dot(lo2, head_ones)
        y = x * lax.rsqrt(ssq * (1.0 / HEAD_DIM) + EPS) * g
        other = jnp.where(first_half, pltpu.roll(y, LANES - HEAD_DIM // 2, 1), pltpu.roll(y, HEAD_DIM // 2, 1))
        return y * cos + other * sin

    gq = gq_ref[...]
    for jb in range(ATT_W // LANES):
        y = norm_rope(q_ref[:, jb * LANES:(jb + 1) * LANES], gq) * Q_SCALE
        ys = pltpu.roll(y, HEAD_DIM, 1)
        if (2 * jb) // GROUP == 0:
            e0 = jnp.where(left, y, 0.0)
            e1 = jnp.where(left, ys, 0.0)
        else:
            e0 = jnp.where(left, 0.0, ys)
            e1 = jnp.where(left, 0.0, y)
        qt_ref[2 * jb] = jnp.transpose(e0).astype(BF16)
        qt_ref[2 * jb + 1] = jnp.transpose(e1).astype(BF16)
    kr_ref[...] = norm_rope(k_ref[...], gk_ref[...]).astype(BF16)
    vt = jnp.transpose(v_ref[...])
    ones = jnp.ones((V_ROWS - HEAD_DIM, tm), BF16)
    for kv in range(N_KV_HEADS):
        vt_ref[kv, 0:HEAD_DIM, :] = vt[kv * HEAD_DIM:(kv + 1) * HEAD_DIM].astype(BF16)
        vt_ref[kv, HEAD_DIM:V_ROWS, :] = ones


def _attn_prep(z, cos_t, sin_t, gq, gk, L, tm=512):
    n = z.shape[0]
    lt = L // tm
    return pl.pallas_call(
        _attn_prep_kernel,
        grid=(n // tm,),
        in_specs=[
            pl.BlockSpec((tm, ATT_W), lambda i: (i, Z_AQ // ATT_W)),
            pl.BlockSpec((tm, KV_W), lambda i: (i, Z_AK // KV_W)),
            pl.BlockSpec((tm, KV_W), lambda i: (i, Z_AV // KV_W)),
            pl.BlockSpec((tm, LANES), lambda i: (i % lt, 0)),
            pl.BlockSpec((tm, LANES), lambda i: (i % lt, 0)),
            pl.BlockSpec((1, LANES), lambda i: (0, 0)),
            pl.BlockSpec((1, LANES), lambda i: (0, 0)),
        ],
        out_specs=[
            pl.BlockSpec((N_HEADS, LANES, tm), lambda i: (0, 0, i)),
            pl.BlockSpec((tm, KV_W), lambda i: (i, 0)),
            pl.BlockSpec((N_KV_HEADS, V_ROWS, tm), lambda i: (0, 0, i)),
        ],
        out_shape=[
            jax.ShapeDtypeStruct((N_HEADS, LANES, n), BF16),
            jax.ShapeDtypeStruct((n, KV_W), BF16),
            jax.ShapeDtypeStruct((N_KV_HEADS, V_ROWS, n), BF16),
        ],
        compiler_params=_cparams("parallel"),
        name="attn_prep",
    )(z, z, z, cos_t, sin_t, gq, gk)


def _flash_kernel(qt_ref, k_ref, vt_ref, o_ref, s_ref, p_ref, *, tk):
    tq = qt_ref.shape[2]
    L = k_ref.shape[0]
    nt = L // tk
    heads = range(N_HEADS)

    def scores(j, slot):
        kj = k_ref[pl.ds(pl.multiple_of(j * tk, tk), tk), :]
        for h in heads:
            s_ref[slot, h] = _dot(kj, qt_ref[h])

    def values(j, slot, alphas, accs):
        start = pl.multiple_of(j * tk, tk)
        vjs = [vt_ref[kv, :, pl.ds(start, tk)] for kv in range(N_KV_HEADS)]
        return [alphas[h] * accs[h] + _dot(vjs[h // GROUP], p_ref[slot, h]) for h in heads]

    scores(0, 0)
    p_ref[1] = jnp.zeros(p_ref.shape[1:], BF16)

    def half_step(jv, js, other, cur, ms, alphas, accs):
        kj = k_ref[pl.ds(pl.multiple_of(js * tk, tk), tk), :]
        start = pl.multiple_of(jv * tk, tk)
        vjs = [vt_ref[kv, :, pl.ds(start, tk)] for kv in range(N_KV_HEADS)]
        ms, alphas, accs = list(ms), list(alphas), list(accs)
        for h in heads:
            accs[h] = alphas[h] * accs[h] + _dot(vjs[h // GROUP], p_ref[other, h])
            s = s_ref[cur, h]
            m_new = jnp.maximum(ms[h], jnp.max(s, axis=0, keepdims=True))
            alphas[h] = jnp.exp2(ms[h] - m_new)
            ms[h] = m_new
            p_ref[cur, h] = jnp.exp2(s - m_new).astype(BF16)
            s_ref[other, h] = _dot(kj, qt_ref[h])
        return ms, alphas, accs

    def body(jj, carry):
        ms, alphas, accs = carry
        j = 2 * jj
        ms, alphas, accs = half_step(jnp.maximum(j - 1, 0), j + 1, 1, 0, ms, alphas, accs)
        return half_step(j, jnp.minimum(j + 2, nt - 1), 0, 1, ms, alphas, accs)

    init = ([jnp.full((1, tq), -jnp.inf, F32)] * N_HEADS, [jnp.ones((1, tq), F32)] * N_HEADS,
            [jnp.zeros((V_ROWS, tq), F32)] * N_HEADS)
    _, alphas, accs = lax.fori_loop(0, nt // 2, body, init)
    accs = values(nt - 1, 1, alphas, accs)
    for h in heads:
        o_ref[h * HEAD_DIM:(h + 1) * HEAD_DIM, :] = accs[h][0:HEAD_DIM] / accs[h][HEAD_DIM:HEAD_DIM + 1]


def _flash(qt, kr, vt, B, L, tq=256, tk=512):
    n = kr.shape[0]
    lt = L // tq
    assert (L // tk) % 2 == 0
    return pl.pallas_call(
        functools.partial(_flash_kernel, tk=tk),
        grid=(B, lt),
        in_specs=[
            pl.BlockSpec((N_HEADS, LANES, tq), lambda b, i: (0, 0, b * lt + i)),
            pl.BlockSpec((L, KV_W), lambda b, i: (b, 0)),
            pl.BlockSpec((N_KV_HEADS, V_ROWS, L), lambda b, i: (0, 0, b)),
        ],
        out_specs=pl.BlockSpec((ATT_W, tq), lambda b, i: (0, b * lt + i)),
        out_shape=jax.ShapeDtypeStruct((ATT_W, n), F32),
        scratch_shapes=[pltpu.VMEM((2, N_HEADS, tk, tq), F32), pltpu.VMEM((2, N_HEADS, tk, tq), BF16)],
        compiler_params=_cparams("parallel", "parallel"),
        name="flash",
    )(qt, kr, vt)


def _hy_pre_kernel(v_ref, vp_ref, vn_ref, a_ref, ap_ref, an_ref, b_ref, bp_ref, bn_ref, w_ref, cb_ref,
                   u_ref, x2_ref, *, lt):
    tm = v_ref.shape[0]
    i = pl.program_id(0)
    first = (i % lt) == 0
    last = (i % lt) == lt - 1
    rows = lax.broadcasted_iota(jnp.int32, v_ref.shape, 0)

    def conv3(m_ref, p_ref, n_ref, part):
        x = m_ref[...]
        prev = jnp.where(first, 0.0, p_ref[7:8, :])
        nxt = jnp.where(last, 0.0, n_ref[0:1, :])
        dn = jnp.where(rows == 0, prev, pltpu.roll(x, 1, 0))
        up = jnp.where(rows == tm - 1, nxt, pltpu.roll(x, tm - 1, 0))
        w = w_ref[part]
        return dn * w[0:1] + x * w[1:2] + up * w[2:3] + cb_ref[part:part + 1, :]

    u_ref[...] = conv3(v_ref, vp_ref, vn_ref, 0) * conv3(a_ref, ap_ref, an_ref, 1)
    x2_ref[...] = conv3(b_ref, bp_ref, bn_ref, 2)


def _hy_pre(z, conv_w, conv_b, L, tm=512, tc=512):
    n = z.shape[0]
    lt = L // tm
    r8 = tm // 8
    nb8 = n // 8
    specs = []
    for part in range(3):
        c0 = (Z_HZ + part * HY_W) // tc
        specs.append(pl.BlockSpec((tm, tc), lambda i, c, c0=c0: (i, c0 + c)))
        specs.append(pl.BlockSpec((8, tc), lambda i, c, c0=c0: (jnp.maximum(i * r8 - 1, 0), c0 + c)))
        specs.append(pl.BlockSpec((8, tc), lambda i, c, c0=c0: (jnp.minimum((i + 1) * r8, nb8 - 1), c0 + c)))
    specs.append(pl.BlockSpec((3, 3, tc), lambda i, c: (0, 0, c)))
    specs.append(pl.BlockSpec((3, tc), lambda i, c: (0, c)))
    return pl.pallas_call(
        functools.partial(_hy_pre_kernel, lt=lt),
        grid=(n // tm, HY_W // tc),
        in_specs=specs,
        out_specs=[pl.BlockSpec((tm, tc), lambda i, c: (i, c))] * 2,
        out_shape=[jax.ShapeDtypeStruct((n, HY_W), F32)] * 2,
        compiler_params=_cparams("parallel", "parallel"),
        name="hy_pre",
    )(*([z] * 9), conv_w, conv_b)


def _hy_filt_kernel(f_ref, w1_ref, b1_ref, f1_ref, w2_ref, b2_ref, f2_ref, w3_ref, dl_ref, kf_ref, ssq_ref):
    feats = f_ref[...]
    half = feats.shape[0] // 2
    halves = (feats[:half], feats[half:])
    h = jnp.sin(f1_ref[...] * (_dot3(jnp.concatenate(halves, axis=1), w1_ref[...]) + b1_ref[...]))
    h = jnp.sin(f2_ref[...] * (_dot3(h, w2_ref[...]) + b2_ref[...]))

    @pl.when(pl.program_id(0) == 0)
    def _():
        ssq_ref[...] = jnp.zeros_like(ssq_ref)

    for i, f in enumerate(halves):
        t = f[:, 0:1]
        valid = f[:, HY_EMB:HY_EMB + 1]
        kf = _dot3(h, w3_ref[i]) * jnp.exp(-t * dl_ref[...]) * valid
        kf_ref[i * half:(i + 1) * half, :] = kf
        ssq_ref[...] += jnp.sum(kf * kf, axis=0, keepdims=True)


def _hy_filt(feats_ext, w1p, b1, f1, w2, b2, f2, w3s, deltas, L, tr=512):
    lt = L // tr
    full = lambda a: pl.BlockSpec(a.shape, lambda i: (0,) * a.ndim)
    return pl.pallas_call(
        _hy_filt_kernel,
        grid=(2 * lt,),
        in_specs=[
            pl.BlockSpec((tr, LANES), lambda i: (i, 0)),
            full(w1p), full(b1), full(f1), full(w2), full(b2), full(f2),
            pl.BlockSpec((None, 2, 2 * HY_FILTER_W, HY_W), lambda i: (i // lt, 0, 0, 0)),
            full(deltas),
        ],
        out_specs=[pl.BlockSpec((tr, HY_W), lambda i: (i, 0)), pl.BlockSpec((1, HY_W), lambda i: (0, 0))],
        out_shape=[jax.ShapeDtypeStruct((2 * L, HY_W), F32), jax.ShapeDtypeStruct((1, HY_W), F32)],
        compiler_params=_cparams("arbitrary"),
        name="hy_filt",
    )(feats_ext, w1p, b1, f1, w2, b2, f2, w3s, deltas)


def _flat_rows(ref):
    return ref.reshape(ref.shape[0] * ref.shape[1] * FFT_SUB, ref.shape[-1])


def _to_slabs(src2d, row0, slab_ref):
    n = slab_ref.shape[1]
    for c in range(slab_ref.shape[0]):
        slab_ref[c] = src2d[row0:row0 + n, c * LANES:(c + 1) * LANES]


def _outer_batch(B, outer_rows):
    return max(1, min(B, 128 // outer_rows))


def _fft1_kernel(x_ref, f_ref, o_ref, xs_ref, os_ref):
    hin, rows = x_ref.shape[1], o_ref.shape[1]
    x2, o2 = _flat_rows(x_ref), _flat_rows(o_ref)
    f = f_ref[...]
    for b in range(x_ref.shape[0]):
        _to_slabs(x2, b * hin * FFT_SUB, xs_ref)
        for c in range(xs_ref.shape[0]):
            for p in range(0, FFT_SUB, 2):
                x = jnp.concatenate([xs_ref[c, pl.ds(p, hin, stride=FFT_SUB), :],
                                     xs_ref[c, pl.ds(p + 1, hin, stride=FFT_SUB), :]], axis=1)
                xh, xl = _split2(x)
                y = _dot(f, jnp.concatenate([xh, xl, xh], axis=0))
                os_ref[c, pl.ds(p, rows, stride=FFT_SUB), :] = y[:, :LANES]
                os_ref[c, pl.ds(p + 1, rows, stride=FFT_SUB), :] = y[:, LANES:]
            o2[b * rows * FFT_SUB:(b + 1) * rows * FFT_SUB, c * LANES:(c + 1) * LANES] = os_ref[c]


def _fft1(x, fcat, B, hin):
    C = x.shape[-1]
    rows = fcat.shape[0]
    nj = FFT_N2 // FFT_SUB
    bt = _outer_batch(B, hin)
    out = pl.pallas_call(
        _fft1_kernel,
        grid=(B // bt, nj),
        in_specs=[
            pl.BlockSpec((bt, hin, None, FFT_SUB, C), lambda b, j: (b, 0, j, 0, 0)),
            pl.BlockSpec(fcat.shape, lambda b, j: (0, 0)),
        ],
        out_specs=pl.BlockSpec((bt, rows, None, FFT_SUB, C), lambda b, j: (b, 0, j, 0, 0)),
        out_shape=jax.ShapeDtypeStruct((B, rows, nj, FFT_SUB, C), F32),
        scratch_shapes=[pltpu.VMEM((C // LANES, hin * FFT_SUB, LANES), F32),
                        pltpu.VMEM((C // LANES, rows * FFT_SUB, LANES), F32)],
        compiler_params=_cparams("parallel", "parallel"),
        name="fft1",
    )(x.reshape(B, hin, nj, FFT_SUB, C), fcat)
    return out.reshape(B, rows, FFT_N2, C)


def _fft2_kernel(ar_ref, ai_ref, tr_ref, ti_ref, m_ref, o_ref):
    m = m_ref[...]
    for r in range(FFT_KB):
        ar, ai = ar_ref[r], ai_ref[r]
        tr, ti = (jnp.tile(t[r], (1, FFT_CT // LANES)) for t in (tr_ref, ti_ref))
        b = jnp.concatenate([ar * tr - ai * ti, ar * ti + ai * tr], axis=0)
        bh, bl = _split2(b)
        o_ref[r] = _dot(m, jnp.concatenate([bh, bl, bh], axis=0))


def _fft2(a, tw_r, tw_i, mcat, kp):
    B = a.shape[0]
    C = a.shape[-1]
    nk = kp // FFT_KB
    return pl.pallas_call(
        _fft2_kernel,
        grid=(B, C // FFT_CT, nk),
        in_specs=[
            pl.BlockSpec((None, FFT_KB, FFT_N2, FFT_CT), lambda b, c, k: (b, k, 0, c)),
            pl.BlockSpec((None, FFT_KB, FFT_N2, FFT_CT), lambda b, c, k: (b, nk + k, 0, c)),
            pl.BlockSpec((FFT_KB, FFT_N2, LANES), lambda b, c, k: (k, 0, 0)),
            pl.BlockSpec((FFT_KB, FFT_N2, LANES), lambda b, c, k: (k, 0, 0)),
            pl.BlockSpec(mcat.shape, lambda b, c, k: (0, 0)),
        ],
        out_specs=pl.BlockSpec((None, FFT_KB, 2 * FFT_N2, FFT_CT), lambda b, c, k: (b, k, 0, c)),
        out_shape=jax.ShapeDtypeStruct((B, kp, 2 * FFT_N2, C), F32),
        compiler_params=_cparams("parallel", "parallel", "parallel"),
        name="fft2",
    )(a, a, tw_r, tw_i, mcat)


def _ifft2_kernel(x_ref, k_ref, ssq_ref, tr_ref, ti_ref, m_ref, o_ref):
    m = m_ref[...]
    scale = lax.rsqrt(ssq_ref[...] + EPS)
    n2 = FFT_N2
    for r in range(FFT_KB):
        x = x_ref[r]
        kf = k_ref[r]
        xr, xi = x[:n2], x[n2:]
        kr, ki = kf[:n2] * scale, kf[n2:] * scale
        p = jnp.concatenate([xr * kr - xi * ki, xr * ki + xi * kr], axis=0)
        ph, plo = _split2(p)
        c = _dot(m, jnp.concatenate([ph, plo, ph], axis=0))
        cr, ci = c[:n2], c[n2:]
        tr, ti = (jnp.tile(t[r], (1, FFT_CT // LANES)) for t in (tr_ref, ti_ref))
        o_ref[0, r] = cr * tr + ci * ti
        o_ref[1, r] = ci * tr - cr * ti


def _ifft2(xs, ks, ssq, tw_r, tw_i, mcat_inv):
    B, kp, _, C = xs.shape
    nk = kp // FFT_KB
    return pl.pallas_call(
        _ifft2_kernel,
        grid=(B, C // FFT_CT, nk),
        in_specs=[
            pl.BlockSpec((None, FFT_KB, 2 * FFT_N2, FFT_CT), lambda b, c, k: (b, k, 0, c)),
            pl.BlockSpec((None, FFT_KB, 2 * FFT_N2, FFT_CT), lambda b, c, k: (0, k, 0, c)),
            pl.BlockSpec((1, FFT_CT), lambda b, c, k: (0, c)),
            pl.BlockSpec((FFT_KB, FFT_N2, LANES), lambda b, c, k: (k, 0, 0)),
            pl.BlockSpec((FFT_KB, FFT_N2, LANES), lambda b, c, k: (k, 0, 0)),
            pl.BlockSpec(mcat_inv.shape, lambda b, c, k: (0, 0)),
        ],
        out_specs=pl.BlockSpec((None, 2, FFT_KB, FFT_N2, FFT_CT), lambda b, c, k: (b, 0, k, 0, c)),
        out_shape=jax.ShapeDtypeStruct((B, 2, kp, FFT_N2, C), F32),
        compiler_params=_cparams("parallel", "parallel", "parallel"),
        name="ifft2",
    )(xs, ks, ssq, tw_r, tw_i, mcat_inv)


def _ifft1_kernel(d_ref, w_ref, u_ref, x2_ref, skip_ref, o_ref, ds_ref, ys_ref):
    rows, h = d_ref.shape[1], o_ref.shape[1]
    d2, u2, x22, o2 = (_flat_rows(r) for r in (d_ref, u_ref, x2_ref, o_ref))
    w = w_ref[...]
    for b in range(d_ref.shape[0]):
        _to_slabs(d2, b * rows * FFT_SUB, ds_ref)
        seq = slice(b * h * FFT_SUB, (b + 1) * h * FFT_SUB)
        for c in range(ds_ref.shape[0]):
            lanes = slice(c * LANES, (c + 1) * LANES)
            for p in range(0, FFT_SUB, 2):
                d = jnp.concatenate([ds_ref[c, pl.ds(p, rows, stride=FFT_SUB), :],
                                     ds_ref[c, pl.ds(p + 1, rows, stride=FFT_SUB), :]], axis=1)
                dh, dl = _split2(d)
                y = _dot(w, jnp.concatenate([dh, dl, dh], axis=0))
                ys_ref[c, pl.ds(p, h, stride=FFT_SUB), :] = y[:, :LANES]
                ys_ref[c, pl.ds(p + 1, h, stride=FFT_SUB), :] = y[:, LANES:]
            o2[seq, lanes] = x22[seq, lanes] * (ys_ref[c] + u2[seq, lanes] * skip_ref[:, lanes])


def _ifft1(d, wcat, u, x2c, skip, B):
    rows, C = d.shape[1], d.shape[-1]
    h = wcat.shape[0]
    nj = FFT_N2 // FFT_SUB
    bt = _outer_batch(B, h)
    seq = pl.BlockSpec((bt, h, None, FFT_SUB, C), lambda b, j: (b, 0, j, 0, 0))
    out = pl.pallas_call(
        _ifft1_kernel,
        grid=(B // bt, nj),
        in_specs=[
            pl.BlockSpec((bt, rows, None, FFT_SUB, C), lambda b, j: (b, 0, j, 0, 0)),
            pl.BlockSpec(wcat.shape, lambda b, j: (0, 0)),
            seq, seq,
            pl.BlockSpec((1, C), lambda b, j: (0, 0)),
        ],
        out_specs=seq,
        out_shape=jax.ShapeDtypeStruct((B, h, nj, FFT_SUB, C), F32),
        scratch_shapes=[pltpu.VMEM((C // LANES, rows * FFT_SUB, LANES), F32),
                        pltpu.VMEM((C // LANES, h * FFT_SUB, LANES), F32)],
        compiler_params=_cparams("parallel", "parallel"),
        name="ifft1",
    )(d.reshape(B, rows, nj, FFT_SUB, C), wcat, u.reshape(B, h, nj, FFT_SUB, C), x2c.reshape(B, h, nj, FFT_SUB, C),
      skip)
    return out.reshape(B * h * FFT_N2, C)


def _hilo_cat(m):
    hi = m.astype(BF16)
    lo = (m - hi.astype(F32)).astype(BF16)
    return jnp.concatenate([hi, hi, lo], axis=1)


def _fft_tables(L):
    n = 2 * L
    n2 = FFT_N2
    n1 = n // n2
    h = n1 // 2
    kp = h + FFT_KB
    k1 = jnp.arange(kp, dtype=jnp.int32)

    def outer(hin):
        nn = jnp.arange(hin, dtype=jnp.int32)
        ang = (2.0 * math.pi / n1) * ((k1[:, None] * nn[None, :]) % n1).astype(F32)
        return _hilo_cat(jnp.concatenate([jnp.cos(ang), -jnp.sin(ang)], axis=0))

    f_data = outer(h)
    f_filt = outer(n1)
    j = jnp.arange(n2, dtype=jnp.int32)
    ang2 = (2.0 * math.pi / n2) * ((j[:, None] * j[None, :]) % n2).astype(F32)
    c2, s2 = jnp.cos(ang2), jnp.sin(ang2)
    m_fwd = _hilo_cat(jnp.block([[c2, s2], [-s2, c2]]))
    m_inv = _hilo_cat(jnp.block([[c2, -s2], [s2, c2]]))
    angt = (2.0 * math.pi / n) * (k1[:, None] * j[None, :]).astype(F32)
    tw_r = jnp.broadcast_to(jnp.cos(angt)[:, :, None], (kp, n2, LANES))
    tw_i = jnp.broadcast_to(-jnp.sin(angt)[:, :, None], (kp, n2, LANES))
    wgt = jnp.where((k1 == 0) | (k1 == h), 1.0, jnp.where(k1 < h, 2.0, 0.0)).astype(F32) / n
    nn = jnp.arange(h, dtype=jnp.int32)
    angi = (2.0 * math.pi / n1) * ((nn[:, None] * k1[None, :]) % n1).astype(F32)
    w_inv = _hilo_cat(jnp.concatenate([jnp.cos(angi) * wgt, -jnp.sin(angi) * wgt], axis=1))
    return dict(n1=n1, h=h, kp=kp, f_data=f_data, f_filt=f_filt, m_fwd=m_fwd, m_inv=m_inv,
                tw_r=tw_r, tw_i=tw_i, w_inv=w_inv)


def _filter_feats(L):
    i = jnp.arange(2 * L, dtype=jnp.int32)
    n = jnp.where(i < L, i, 2 * L - i).astype(F32)
    t = (n / (L - 1))[:, None]
    w = 2.0 * math.pi * n / L
    fb = jnp.linspace(1e-4, HY_BANDS - 1, HY_BANDS, dtype=F32)
    ph = w[:, None] * fb
    valid = jnp.where(i == L, 0.0, 1.0).astype(F32)[:, None]
    ext = jnp.concatenate([t, jnp.cos(ph), -jnp.sin(ph), valid], axis=-1)
    return jnp.pad(ext, ((0, 0), (0, LANES - HY_EMB - 1)))


def _hyena_spectrum(tabs, feats_ext, lp, L):
    kf, ssq = _hy_filt(feats_ext, lp["hy_w1p"], lp["hy_b1"], lp["hy_f1"], lp["hy_w2"], lp["hy_b2"], lp["hy_f2"],
                       lp["hy_w3s"], lp["hy_deltas"], L)
    a = _fft1(kf, tabs["f_filt"], 1, tabs["n1"])
    ks = _fft2(a, tabs["tw_r"], tabs["tw_i"], tabs["m_fwd"], tabs["kp"])
    return ks, ssq


def _hyena(z, tabs, ks, ssq, lp, B, L):
    u, x2c = _hy_pre(z, lp["hy_conv_w"], lp["hy_conv_b"], L)
    kp = tabs["kp"]
    a = _fft1(u, tabs["f_data"], B, tabs["h"])
    xs = _fft2(a, tabs["tw_r"], tabs["tw_i"], tabs["m_fwd"], kp)
    d = _ifft2(xs, ks, ssq, tabs["tw_r"], tabs["tw_i"], tabs["m_inv"])
    return _ifft1(d.reshape(B, 2 * kp, FFT_N2, HY_W), tabs["w_inv"], u, x2c, lp["hy_skip"], B)


def _gla_kernel(qf_ref, kf_ref, vf_ref, lf_ref, qb_ref, kb_ref, vb_ref, lb_ref, gup_ref, gb_ref,
                of_ref, ob_ref, sf_ref, sb_ref, *, nchunk):
    C = GLA_CHUNK
    R = nchunk * C

    @pl.when(pl.program_id(1) == 0)
    def _():
        sf_ref[...] = jnp.zeros_like(sf_ref)
        sb_ref[...] = jnp.zeros_like(sb_ref)

    rr = lax.broadcasted_iota(jnp.int32, (R, R), 0)
    cc = lax.broadcasted_iota(jnp.int32, (R, R), 1)
    same_chunk = (rr // C) == (cc // C)
    lane = lax.broadcasted_iota(jnp.int32, (C, GLA_K), 1) // GLA_DK
    ri4 = lax.broadcasted_iota(jnp.int32, (GLA_HEADS * C, C), 0) % C
    ci4 = lax.broadcasted_iota(jnp.int32, (GLA_HEADS * C, C), 1)

    def stack_heads(x):
        return jnp.concatenate([jnp.where(lane == h, x, 0.0) for h in range(GLA_HEADS)], axis=0).astype(BF16)

    def per_chunk_rows(b, row):
        return jnp.concatenate([jnp.broadcast_to(b[c * C + row:c * C + row + 1], (C, GLA_K)) for c in range(nchunk)],
                               axis=0)

    dirs = []
    for d, (q_ref, k_ref, v_ref, l_ref) in enumerate(((qf_ref, kf_ref, vf_ref, lf_ref),
                                                       (qb_ref, kb_ref, vb_ref, lb_ref))):
        logit = _dot3(l_ref[...], gup_ref[:, d * GLA_K:(d + 1) * GLA_K]) + gb_ref[:, d * GLA_K:(d + 1) * GLA_K]
        la = (jnp.minimum(logit, 0.0) - jnp.log(1.0 + jnp.exp(-jnp.abs(logit)))) * (1.0 / GLA_TAU)
        if d == 0:
            tri = jnp.where(same_chunk & (rr >= cc), 1.0, 0.0).astype(BF16)
            keep = ri4 >= ci4
            mid, last = C // 2 - 1, C - 1
            order = list(range(nchunk))
        else:
            tri = jnp.where(same_chunk & (cc >= rr), 1.0, 0.0).astype(BF16)
            keep = ci4 > ri4
            mid, last = C // 2, 0
            order = list(range(nchunk - 1, -1, -1))
        hi, lo = _split2(la)
        lo2 = (la - hi.astype(F32) - lo.astype(F32)).astype(BF16)
        b = _dot(tri, hi) + _dot(tri, lo) + _dot(tri, lo2)
        b_mid = per_chunk_rows(b, mid)
        b_last = per_chunk_rows(b, last)
        q = q_ref[...] * (GLA_DK ** -0.5)
        k = k_ref[...]
        dirs.append(dict(
            keep=keep, order=order, v=v_ref[...].astype(BF16),
            qt=q * jnp.exp(b - b_mid), kt=(k * jnp.exp(b_mid - b)).astype(BF16),
            kh=k * jnp.exp(b_last - b), qh=q * jnp.exp(b), dec=jnp.exp(b_last)))

    for dd in dirs:
        dd["a"] = []
        for c in range(nchunk):
            rs = slice(c * C, (c + 1) * C)
            a = lax.dot_general(stack_heads(dd["qt"][rs]), dd["kt"][rs], (((1,), (1,)), ((), ())),
                                preferred_element_type=F32)
            dd["a"].append(jnp.where(dd["keep"], a, 0.0).astype(BF16))

    for dd in dirs:
        dd["o"], dd["u"] = [], []
        for c in range(nchunk):
            rs = slice(c * C, (c + 1) * C)
            k_t = jnp.transpose(dd["kh"][rs]).astype(BF16)
            outs, ups = [], []
            for h in range(GLA_HEADS):
                vh = dd["v"][rs, h * GLA_DV:(h + 1) * GLA_DV]
                outs.append(_dot(dd["a"][c][h * C:(h + 1) * C], vh))
                ups.append(_dot(k_t[h * GLA_DK:(h + 1) * GLA_DK], vh))
            dd["o"].append(outs)
            dd["u"].append(jnp.concatenate(ups, axis=0))

    for dd, s_ref in zip(dirs, (sf_ref, sb_ref)):
        s = s_ref[...]
        dd["s_prev"] = {}
        for c in dd["order"]:
            dd["s_prev"][c] = s.astype(BF16)
            decay = jnp.transpose(jnp.broadcast_to(dd["dec"][c * C:c * C + 1], (GLA_DV, GLA_K)))
            s = decay * s + dd["u"][c]
        s_ref[...] = s

    for dd, o_ref in zip(dirs, (of_ref, ob_ref)):
        for c in range(nchunk):
            rs = slice(c * C, (c + 1) * C)
            o_inter = _dot(stack_heads(dd["qh"][rs]), dd["s_prev"][c])
            o_ref[rs, :] = jnp.concatenate(
                [dd["o"][c][h] + o_inter[h * C:(h + 1) * C] for h in range(GLA_HEADS)], axis=1)


def _gla(z, gup, gb, B, L, nchunk=4):
    n = z.shape[0]
    R = nchunk * GLA_CHUNK
    T = L // R
    fwd = lambda b, t: b * T + t
    bwd = lambda b, t: b * T + (T - 1 - t)

    def specs(row):
        return [
            pl.BlockSpec((R, GLA_K), lambda b, t: (row(b, t), Z_GQ // GLA_K)),
            pl.BlockSpec((R, GLA_K), lambda b, t: (row(b, t), Z_GK // GLA_K)),
            pl.BlockSpec((R, GLA_V), lambda b, t: (row(b, t), Z_GV // GLA_V)),
            pl.BlockSpec((R, LANES), lambda b, t: (row(b, t), Z_GLOW // LANES)),
        ]

    return pl.pallas_call(
        functools.partial(_gla_kernel, nchunk=nchunk),
        grid=(B, T),
        in_specs=specs(fwd) + specs(bwd) + [
            pl.BlockSpec(gup.shape, lambda b, t: (0, 0)),
            pl.BlockSpec(gb.shape, lambda b, t: (0, 0)),
        ],
        out_specs=[
            pl.BlockSpec((R, GLA_V), lambda b, t: (fwd(b, t), 0)),
            pl.BlockSpec((R, GLA_V), lambda b, t: (bwd(b, t), 0)),
        ],
        out_shape=[jax.ShapeDtypeStruct((n, GLA_V), F32)] * 2,
        scratch_shapes=[pltpu.VMEM((GLA_K, GLA_DV), F32)] * 2,
        compiler_params=_cparams("parallel", "arbitrary"),
        name="gla",
    )(*([z] * 8), gup, gb)


def _mix_out_kernel(x_ref, ya_ref, yh_ref, of_ref, ob_ref, og_ref, g0_ref, g1_ref, g2_ref,
                    wb_ref, wo_ref, gn_ref, o_ref):
    o = of_ref[...] + ob_ref[...]
    og = og_ref[...]
    gn = gn_ref[...]
    parts = []
    for h in range(GLA_HEADS):
        blk = o[:, h * GLA_DV:(h + 1) * GLA_DV]
        ms = jnp.mean(blk * blk, axis=-1, keepdims=True)
        gate = og[:, h * GLA_DV:(h + 1) * GLA_DV]
        parts.append(blk * lax.rsqrt(ms + EPS) * gn * (gate * jax.nn.sigmoid(gate)))
    y_gla = jnp.concatenate(parts, axis=1)
    proj_att = lax.dot_general(ya_ref[...].astype(BF16), wb_ref[0], (((0,), (0,)), ((), ())),
                               preferred_element_type=F32)
    merged = jax.nn.sigmoid(g0_ref[...].astype(F32)) * proj_att
    merged += jax.nn.sigmoid(g1_ref[...].astype(F32)) * _dot(yh_ref[...].astype(BF16), wb_ref[1])
    merged += jax.nn.sigmoid(g2_ref[...].astype(F32)) * _dot(y_gla.astype(BF16), wb_ref[2])
    o_ref[...] = x_ref[...] + _dot(merged.astype(BF16), wo_ref[...])


def _mix_out(x, y_att, y_hy, o_f, o_b, z, zg, wb, wo, gn, tm=256):
    n = x.shape[0]
    row = lambda w: pl.BlockSpec((tm, w), lambda i: (i, 0))
    return pl.pallas_call(
        _mix_out_kernel,
        grid=(n // tm,),
        in_specs=[
            row(D_MODEL), pl.BlockSpec((MIX_W, tm), lambda i: (0, i)), row(MIX_W), row(MIX_W), row(MIX_W),
            pl.BlockSpec((tm, GLA_V), lambda i: (i, Z_GOG // GLA_V)),
            pl.BlockSpec((tm, D_MODEL), lambda i: (i, 0)),
            pl.BlockSpec((tm, D_MODEL), lambda i: (i, 1)),
            pl.BlockSpec((tm, D_MODEL), lambda i: (i, 2)),
            pl.BlockSpec(wb.shape, lambda i: (0, 0, 0)),
            pl.BlockSpec(wo.shape, lambda i: (0, 0)),
            pl.BlockSpec((1, GLA_DV), lambda i: (0, 0)),
        ],
        out_specs=row(D_MODEL),
        out_shape=jax.ShapeDtypeStruct((n, D_MODEL), F32),
        compiler_params=_cparams("parallel"),
        name="mix_out",
    )(x, y_att, y_hy, o_f, o_b, z, zg, zg, zg, wb, wo, gn)


def _ffn_kernel(x_ref, g_ref, wg_ref, wu_ref, wd_ref, o_ref, h_ref, acc_ref):
    j = pl.program_id(1)

    @pl.when(j == 0)
    def _():
        x = x_ref[...]
        ms = jnp.mean(x * x, axis=-1, keepdims=True)
        h_ref[...] = (x * lax.rsqrt(ms + EPS) * g_ref[...]).astype(BF16)
        acc_ref[...] = jnp.zeros_like(acc_ref)

    h = h_ref[...]
    a = _dot(h, wg_ref[...])
    act = (a * jax.nn.sigmoid(a)) * _dot(h, wu_ref[...])
    acc_ref[...] += _dot(act.astype(BF16), wd_ref[...])

    @pl.when(j == pl.num_programs(1) - 1)
    def _():
        o_ref[...] = x_ref[...] + acc_ref[...]


def _ffn(x, g, wg, wu, wd, tm=512, tf=1408):
    n = x.shape[0]
    return pl.pallas_call(
        _ffn_kernel,
        grid=(n // tm, D_FF // tf),
        in_specs=[
            pl.BlockSpec((tm, D_MODEL), lambda i, j: (i, 0)),
            pl.BlockSpec((1, D_MODEL), lambda i, j: (0, 0)),
            pl.BlockSpec((D_MODEL, tf), lambda i, j: (0, j)),
            pl.BlockSpec((D_MODEL, tf), lambda i, j: (0, j)),
            pl.BlockSpec((tf, D_MODEL), lambda i, j: (j, 0)),
        ],
        out_specs=pl.BlockSpec((tm, D_MODEL), lambda i, j: (i, 0)),
        out_shape=jax.ShapeDtypeStruct((n, D_MODEL), F32),
        scratch_shapes=[pltpu.VMEM((tm, D_MODEL), BF16), pltpu.VMEM((tm, D_MODEL), F32)],
        compiler_params=_cparams("parallel", "arbitrary"),
        name="ffn",
    )(x, g, wg, wu, wd)


def _final_norm_kernel(x_ref, g_ref, o_ref):
    x = x_ref[...]
    ms = jnp.mean(x * x, axis=-1, keepdims=True)
    o_ref[...] = x * lax.rsqrt(ms + EPS) * g_ref[...]


def _final_norm(x, g, tm=512):
    n = x.shape[0]
    return pl.pallas_call(
        _final_norm_kernel,
        grid=(n // tm,),
        in_specs=[pl.BlockSpec((tm, D_MODEL), lambda i: (i, 0)), pl.BlockSpec((1, D_MODEL), lambda i: (0, 0))],
        out_specs=pl.BlockSpec((tm, D_MODEL), lambda i: (i, 0)),
        out_shape=jax.ShapeDtypeStruct((n, D_MODEL), F32),
        compiler_params=_cparams("parallel"),
        name="final_norm",
    )(x, g)


def _rope_tables(L):
    rows = L // GRID_W
    r = jnp.repeat(jnp.arange(rows, dtype=F32), GRID_W)
    c = jnp.tile(jnp.arange(GRID_W, dtype=F32), rows)
    inv = ROPE_THETA ** (-jnp.arange(0, ROPE_AXIS_DIM, 2, dtype=F32) / ROPE_AXIS_DIM)
    ang = jnp.concatenate([r[:, None] * inv, c[:, None] * inv], axis=-1)
    cos, sin = jnp.cos(ang), jnp.sin(ang)
    cos_t = jnp.tile(cos, (1, 2 * LANES // HEAD_DIM))
    sin_t = jnp.tile(jnp.concatenate([-sin, sin], axis=-1), (1, LANES // HEAD_DIM))
    return cos_t, sin_t


def _layer_params(l, norm_mix_g, w_in, q_norm_g, k_norm_g, hy_conv_w, hy_conv_b, hy_w1, hy_b1, hy_f1, hy_w2,
                  hy_b2, hy_f2, hy_w3, hy_skip, gla_gate_up, gla_gate_b, gla_norm_g, w_branch, w_out, norm_ffn_g,
                  w_ffn_gate, w_ffn_up, w_ffn_down):
    w = w_in[l]
    offs = {}
    off = 0
    for name, size in (("aq", ATT_W), ("ak", KV_W), ("av", KV_W), ("hz", 3 * HY_W), ("gq", GLA_K), ("gk", GLA_K),
                       ("gv", GLA_V), ("gog", GLA_V), ("glow", 2 * GLA_RANK), ("gates", N_BRANCH * D_MODEL)):
        offs[name] = w[:, off:off + size]
        off += size
    w_cat = jnp.concatenate(
        [offs["aq"], offs["gv"], offs["gog"], offs["hz"], offs["gq"], offs["gk"], offs["ak"],
         offs["av"], offs["glow"], jnp.zeros((D_MODEL, Z_COLS - Z_GLOW - 2 * GLA_RANK), F32)], axis=1).astype(BF16)
    gup = jnp.zeros((LANES, 2 * GLA_K), F32)
    gup = gup.at[0:GLA_RANK, 0:GLA_K].set(gla_gate_up[l, 0])
    gup = gup.at[GLA_RANK:2 * GLA_RANK, GLA_K:2 * GLA_K].set(gla_gate_up[l, 1])
    deltas = jnp.linspace(abs(math.log(HY_TARGET) / HY_SLOW), abs(math.log(HY_TARGET) / HY_FAST), HY_W, dtype=F32)
    return dict(
        norm_mix_g=norm_mix_g[l][None, :],
        w_cat=w_cat, w_gates=offs["gates"].astype(BF16),
        gq=jnp.tile(q_norm_g[l], LANES // HEAD_DIM)[None, :],
        gk=jnp.tile(k_norm_g[l], LANES // HEAD_DIM)[None, :],
        hy_conv_w=jnp.transpose(hy_conv_w[l].reshape(3, 3, HY_W), (1, 0, 2)),
        hy_conv_b=hy_conv_b[l].reshape(3, HY_W),
        hy_w1p=jnp.kron(jnp.eye(2, dtype=F32), jnp.pad(hy_w1[l], ((0, LANES - HY_EMB), (0, 0)))),
        hy_b1=jnp.tile(hy_b1[l], 2)[None, :], hy_f1=jnp.tile(hy_f1[l], 2)[None, :],
        hy_w2=jnp.kron(jnp.eye(2, dtype=F32), hy_w2[l]),
        hy_b2=jnp.tile(hy_b2[l], 2)[None, :], hy_f2=jnp.tile(hy_f2[l], 2)[None, :],
        hy_w3s=jnp.einsum("gh,kfc->fghkc", jnp.eye(2, dtype=F32), hy_w3[l].reshape(HY_FILTER_W, 2, HY_W)
                          ).reshape(2, 2, 2 * HY_FILTER_W, HY_W),
        hy_deltas=deltas[None, :],
        hy_skip=hy_skip[l][None, :],
        gup=gup, gb=gla_gate_b[l].reshape(1, 2 * GLA_K),
        gn=gla_norm_g[l][None, :],
        wb=w_branch[l].astype(BF16), wo=w_out[l].astype(BF16),
        norm_ffn_g=norm_ffn_g[l][None, :],
        wg=w_ffn_gate[l].astype(BF16), wu=w_ffn_up[l].astype(BF16), wd=w_ffn_down[l].astype(BF16),
    )


def _encoder_layer(x, lp, B, L, rope, tabs, ks, ssq):
    z = _in_proj(x, lp["norm_mix_g"], lp["w_cat"], F32)
    zg = _in_proj(x, lp["norm_mix_g"], lp["w_gates"], BF16)
    qt, kr, vt = _attn_prep(z, rope[0], rope[1], lp["gq"], lp["gk"], L)
    y_att = _flash(qt, kr, vt, B, L)
    y_hy = _hyena(z, tabs, ks, ssq, lp, B, L)
    o_f, o_b = _gla(z, lp["gup"], lp["gb"], B, L)
    x = _mix_out(x, y_att, y_hy, o_f, o_b, z, zg, lp["wb"], lp["wo"], lp["gn"])
    return _ffn(x, lp["norm_ffn_g"], lp["wg"], lp["wu"], lp["wd"])


def kernel(x_prompt, x_sample, norm_mix_g, w_in, q_norm_g, k_norm_g, hy_conv_w, hy_conv_b, hy_w1, hy_b1, hy_f1, hy_w2, hy_b2, hy_f2, hy_w3, hy_skip, gla_gate_up, gla_gate_b, gla_norm_g, w_branch, w_out, norm_ffn_g, w_ffn_gate, w_ffn_up, w_ffn_down, final_norm_g):
    streams = []
    for xin in (x_prompt, x_sample):
        B, L, _ = xin.shape
        streams.append(dict(x=xin.reshape(B * L, D_MODEL), B=B, L=L, rope=_rope_tables(L), tabs=_fft_tables(L),
                            feats=_filter_feats(L)))
    for l in range(DEPTH):
        lp = _layer_params(l, norm_mix_g, w_in, q_norm_g, k_norm_g, hy_conv_w, hy_conv_b, hy_w1, hy_b1, hy_f1,
                           hy_w2, hy_b2, hy_f2, hy_w3, hy_skip, gla_gate_up, gla_gate_b, gla_norm_g, w_branch,
                           w_out, norm_ffn_g, w_ffn_gate, w_ffn_up, w_ffn_down)
        for s in streams:
            ks, ssq = _hyena_spectrum(s["tabs"], s["feats"], lp, s["L"])
            s["x"] = _encoder_layer(s["x"], lp, s["B"], s["L"], s["rope"], s["tabs"], ks, ssq)
    outs = []
    for s, xin in zip(streams, (x_prompt, x_sample)):
        outs.append(_final_norm(s["x"], final_norm_g[None, :]).reshape(xin.shape))
    return tuple(outs)
```

```python
import functools
import math

import jax
import jax.numpy as jnp
from jax import lax
from jax.experimental import pallas as pl
from jax.experimental.pallas import tpu as pltpu

F32 = jnp.float32
BF16 = jnp.bfloat16

D_MODEL = 1024
DEPTH = 4
GRID_W = 64
N_HEADS = 8
N_KV_HEADS = 2
HEAD_DIM = 64
GROUP = N_HEADS // N_KV_HEADS
ROPE_THETA = 10000.0
ROPE_AXIS_DIM = HEAD_DIM // 2
ATT_W = N_HEADS * HEAD_DIM
KV_W = N_KV_HEADS * HEAD_DIM
HY_W = D_MODEL // 2
HY_EMB = 33
HY_BANDS = (HY_EMB - 1) // 2
HY_FILTER_W = 64
HY_TARGET = 1e-2
HY_FAST = 0.3
HY_SLOW = 1.5
GLA_HEADS = 4
GLA_DK = 64
GLA_DV = 128
GLA_K = GLA_HEADS * GLA_DK
GLA_V = GLA_HEADS * GLA_DV
GLA_RANK = 16
GLA_TAU = 16.0
GLA_CHUNK = 64
N_BRANCH = 3
MIX_W = 512
D_FF = 2816
EPS = 1e-6

LANES = 128
VMEM_LIMIT = 56 * 1024 * 1024

Z_AQ = 0
Z_GV = 512
Z_GOG = 1024
Z_HZ = 1536
Z_GQ = 3072
Z_GK = 3328
Z_AK = 3584
Z_AV = 3712
Z_GLOW = 3840
Z_COLS = 4096

FFT_N2 = 128
FFT_KB = 8
FFT_SUB = 8
FFT_CT = 512


def _cparams(*sem):
    return pltpu.CompilerParams(dimension_semantics=sem, vmem_limit_bytes=VMEM_LIMIT)


def _split2(x):
    hi = x.astype(BF16)
    lo = (x - hi.astype(F32)).astype(BF16)
    return hi, lo


def _dot(a, b):
    return jnp.dot(a, b, preferred_element_type=F32)


def _dot3(a, b):
    ah, al = _split2(a)
    bh, bl = _split2(b)
    return _dot(ah, bh) + _dot(al, bh) + _dot(ah, bl)


def _in_proj_kernel(x_ref, g_ref, w_ref, wg_ref, z_ref, zg_ref, *, tn):
    x = x_ref[...]
    ms = jnp.mean(x * x, axis=-1, keepdims=True)
    h = (x * lax.rsqrt(ms + EPS) * g_ref[...]).astype(BF16)
    for c in range(0, z_ref.shape[1], tn):
        z_ref[:, c:c + tn] = _dot(h, w_ref[:, c:c + tn])
    for c in range(0, zg_ref.shape[1], tn):
        zg_ref[:, c:c + tn] = _dot(h, wg_ref[:, c:c + tn]).astype(zg_ref.dtype)


def _in_proj(x, g, w, w_gates, tm=512, tn=1024):
    n = x.shape[0]
    resident = lambda a: pl.BlockSpec(a.shape, lambda i: (0, 0), pipeline_mode=pl.Buffered(1))
    row = lambda a: pl.BlockSpec((tm, a.shape[1]), lambda i: (i, 0))
    return pl.pallas_call(
        functools.partial(_in_proj_kernel, tn=tn),
        grid=(n // tm,),
        in_specs=[row(x), pl.BlockSpec((1, D_MODEL), lambda i: (0, 0)), resident(w), resident(w_gates)],
        out_specs=[row(w), row(w_gates)],
        out_shape=[jax.ShapeDtypeStruct((n, w.shape[1]), F32), jax.ShapeDtypeStruct((n, w_gates.shape[1]), BF16)],
        compiler_params=_cparams("parallel"),
        name="in_proj",
    )(x, g, w, w_gates)


V_ROWS = HEAD_DIM + 16
Q_SCALE = (HEAD_DIM ** -0.5) * math.log2(math.e)


def _attn_prep_kernel(q_ref, k_ref, v_ref, cos_ref, sin_ref, gq_ref, gk_ref, qt_ref, kr_ref, vt_ref):
    tm = q_ref.shape[0]
    cos = cos_ref[...]
    sin = sin_ref[...]
    lane = lax.broadcasted_iota(jnp.int32, (tm, LANES), 1)
    first_half = (lane % HEAD_DIM) < (HEAD_DIM // 2)
    left = lane < HEAD_DIM
    r = lax.broadcasted_iota(jnp.int32, (LANES, LANES), 0) // HEAD_DIM
    c = lax.broadcasted_iota(jnp.int32, (LANES, LANES), 1) // HEAD_DIM
    head_ones = jnp.where(r == c, 1.0, 0.0).astype(BF16)

    def norm_rope(x, g):
        xsq = x * x
        hi, lo = _split2(xsq)
        lo2 = (xsq - hi.astype(F32) - lo.astype(F32)).astype(BF16)
        ssq = _dot(hi, head_ones) + _dot(lo, head_ones) + _dot(lo2, head_ones)
        y = x * lax.rsqrt(ssq * (1.0 / HEAD_DIM) + EPS) * g
        other = jnp.where(first_half, pltpu.roll(y, LANES - HEAD_DIM // 2, 1), pltpu.roll(y, HEAD_DIM // 2, 1))
        return y * cos + other * sin

    gq = gq_ref[...]
    for jb in range(ATT_W // LANES):
        y = norm_rope(q_ref[:, jb * LANES:(jb + 1) * LANES], gq) * Q_SCALE
        ys = pltpu.roll(y, HEAD_DIM, 1)
        if (2 * jb) // GROUP == 0:
            e0 = jnp.where(left, y, 0.0)
            e1 = jnp.where(left, ys, 0.0)
        else:
            e0 = jnp.where(left, 0.0, ys)
            e1 = jnp.where(left, 0.0, y)
        qt_ref[2 * jb] = jnp.transpose(e0).astype(BF16)
        qt_ref[2 * jb + 1] = jnp.transpose(e1).astype(BF16)
    kr_ref[...] = norm_rope(k_ref[...], gk_ref[...]).astype(BF16)
    vt = jnp.transpose(v_ref[...])
    ones = jnp.ones((V_ROWS - HEAD_DIM, tm), BF16)
    for kv in range(N_KV_HEADS):
        vt_ref[kv, 0:HEAD_DIM, :] = vt[kv * HEAD_DIM:(kv + 1) * HEAD_DIM].astype(BF16)
        vt_ref[kv, HEAD_DIM:V_ROWS, :] = ones


def _attn_prep(z, cos_t, sin_t, gq, gk, L, tm=512):
    n = z.shape[0]
    lt = L // tm
    return pl.pallas_call(
        _attn_prep_kernel,
        grid=(n // tm,),
        in_specs=[
            pl.BlockSpec((tm, ATT_W), lambda i: (i, Z_AQ // ATT_W)),
            pl.BlockSpec((tm, KV_W), lambda i: (i, Z_AK // KV_W)),
            pl.BlockSpec((tm, KV_W), lambda i: (i, Z_AV // KV_W)),
            pl.BlockSpec((tm, LANES), lambda i: (i % lt, 0)),
            pl.BlockSpec((tm, LANES), lambda i: (i % lt, 0)),
            pl.BlockSpec((1, LANES), lambda i: (0, 0)),
            pl.BlockSpec((1, LANES), lambda i: (0, 0)),
        ],
        out_specs=[
            pl.BlockSpec((N_HEADS, LANES, tm), lambda i: (0, 0, i)),
            pl.BlockSpec((tm, KV_W), lambda i: (i, 0)),
            pl.BlockSpec((N_KV_HEADS, V_ROWS, tm), lambda i: (0, 0, i)),
        ],
        out_shape=[
            jax.ShapeDtypeStruct((N_HEADS, LANES, n), BF16),
            jax.ShapeDtypeStruct((n, KV_W), BF16),
            jax.ShapeDtypeStruct((N_KV_HEADS, V_ROWS, n), BF16),
        ],
        compiler_params=_cparams("parallel"),
        name="attn_prep",
    )(z, z, z, cos_t, sin_t, gq, gk)


def _flash_kernel(qt_ref, k_ref, vt_ref, o_ref, s_ref, p_ref, *, tk):
    tq = qt_ref.shape[2]
    L = k_ref.shape[0]
    nt = L // tk
    heads = range(N_HEADS)

    def scores(j, slot):
        kj = k_ref[pl.ds(pl.multiple_of(j * tk, tk), tk), :]
        for h in heads:
            s_ref[slot, h] = _dot(kj, qt_ref[h])

    def values(j, slot, alphas, accs):
        start = pl.multiple_of(j * tk, tk)
        vjs = [vt_ref[kv, :, pl.ds(start, tk)] for kv in range(N_KV_HEADS)]
        return [alphas[h] * accs[h] + _dot(vjs[h // GROUP], p_ref[slot, h]) for h in heads]

    scores(0, 0)
    p_ref[1] = jnp.zeros(p_ref.shape[1:], BF16)

    def half_step(jv, js, other, cur, ms, alphas, accs):
        kj = k_ref[pl.ds(pl.multiple_of(js * tk, tk), tk), :]
        start = pl.multiple_of(jv * tk, tk)
        vjs = [vt_ref[kv, :, pl.ds(start, tk)] for kv in range(N_KV_HEADS)]
        ms, alphas, accs = list(ms), list(alphas), list(accs)
        for h in heads:
            accs[h] = alphas[h] * accs[h] + _dot(vjs[h // GROUP], p_ref[other, h])
            s = s_ref[cur, h]
            m_new = jnp.maximum(ms[h], jnp.max(s, axis=0, keepdims=True))
            alphas[h] = jnp.exp2(ms[h] - m_new)
            ms[h] = m_new
            p_ref[cur, h] = jnp.exp2(s - m_new).astype(BF16)
            s_ref[other, h] = _dot(kj, qt_ref[h])
        return ms, alphas, accs

    def body(jj, carry):
        ms, alphas, accs = carry
        j = 2 * jj
        ms, alphas, accs = half_step(jnp.maximum(j - 1, 0), j + 1, 1, 0, ms, alphas, accs)
        return half_step(j, jnp.minimum(j + 2, nt - 1), 0, 1, ms, alphas, accs)

    init = ([jnp.full((1, tq), -jnp.inf, F32)] * N_HEADS, [jnp.ones((1, tq), F32)] * N_HEADS,
            [jnp.zeros((V_ROWS, tq), F32)] * N_HEADS)
    _, alphas, accs = lax.fori_loop(0, nt // 2, body, init)
    accs = values(nt - 1, 1, alphas, accs)
    for h in heads:
        o_ref[h * HEAD_DIM:(h + 1) * HEAD_DIM, :] = accs[h][0:HEAD_DIM] / accs[h][HEAD_DIM:HEAD_DIM + 1]


def _flash(qt, kr, vt, B, L, tq=256, tk=512):
    n = kr.shape[0]
    lt = L // tq
    assert (L // tk) % 2 == 0
    return pl.pallas_call(
        functools.partial(_flash_kernel, tk=tk),
        grid=(B, lt),
        in_specs=[
            pl.BlockSpec((N_HEADS, LANES, tq), lambda b, i: (0, 0, b * lt + i)),
            pl.BlockSpec((L, KV_W), lambda b, i: (b, 0)),
            pl.BlockSpec((N_KV_HEADS, V_ROWS, L), lambda b, i: (0, 0, b)),
        ],
        out_specs=pl.BlockSpec((ATT_W, tq), lambda b, i: (0, b * lt + i)),
        out_shape=jax.ShapeDtypeStruct((ATT_W, n), F32),
        scratch_shapes=[pltpu.VMEM((2, N_HEADS, tk, tq), F32), pltpu.VMEM((2, N_HEADS, tk, tq), BF16)],
        compiler_params=_cparams("parallel", "parallel"),
        name="flash",
    )(qt, kr, vt)


def _hy_pre_kernel(v_ref, vp_ref, vn_ref, a_ref, ap_ref, an_ref, b_ref, bp_ref, bn_ref, w_ref, cb_ref,
                   u_ref, x2_ref, *, lt):
    tm = v_ref.shape[0]
    i = pl.program_id(0)
    first = (i % lt) == 0
    last = (i % lt) == lt - 1
    rows = lax.broadcasted_iota(jnp.int32, v_ref.shape, 0)

    def conv3(m_ref, p_ref, n_ref, part):
        x = m_ref[...]
        prev = jnp.where(first, 0.0, p_ref[7:8, :])
        nxt = jnp.where(last, 0.0, n_ref[0:1, :])
        dn = jnp.where(rows == 0, prev, pltpu.roll(x, 1, 0))
        up = jnp.where(rows == tm - 1, nxt, pltpu.roll(x, tm - 1, 0))
        w = w_ref[part]
        return dn * w[0:1] + x * w[1:2] + up * w[2:3] + cb_ref[part:part + 1, :]

    u_ref[...] = conv3(v_ref, vp_ref, vn_ref, 0) * conv3(a_ref, ap_ref, an_ref, 1)
    x2_ref[...] = conv3(b_ref, bp_ref, bn_ref, 2)


def _hy_pre(z, conv_w, conv_b, L, tm=512, tc=512):
    n = z.shape[0]
    lt = L // tm
    r8 = tm // 8
    nb8 = n // 8
    specs = []
    for part in range(3):
        c0 = (Z_HZ + part * HY_W) // tc
        specs.append(pl.BlockSpec((tm, tc), lambda i, c, c0=c0: (i, c0 + c)))
        specs.append(pl.BlockSpec((8, tc), lambda i, c, c0=c0: (jnp.maximum(i * r8 - 1, 0), c0 + c)))
        specs.append(pl.BlockSpec((8, tc), lambda i, c, c0=c0: (jnp.minimum((i + 1) * r8, nb8 - 1), c0 + c)))
    specs.append(pl.BlockSpec((3, 3, tc), lambda i, c: (0, 0, c)))
    specs.append(pl.BlockSpec((3, tc), lambda i, c: (0, c)))
    return pl.pallas_call(
        functools.partial(_hy_pre_kernel, lt=lt),
        grid=(n // tm, HY_W // tc),
        in_specs=specs,
        out_specs=[pl.BlockSpec((tm, tc), lambda i, c: (i, c))] * 2,
        out_shape=[jax.ShapeDtypeStruct((n, HY_W), F32)] * 2,
        compiler_params=_cparams("parallel", "parallel"),
        name="hy_pre",
    )(*([z] * 9), conv_w, conv_b)


def _hy_filt_kernel(f_ref, w1_ref, b1_ref, f1_ref, w2_ref, b2_ref, f2_ref, w3_ref, dl_ref, kf_ref, ssq_ref):
    feats = f_ref[...]
    half = feats.shape[0] // 2
    halves = (feats[:half], feats[half:])
    h = jnp.sin(f1_ref[...] * (_dot3(jnp.concatenate(halves, axis=1), w1_ref[...]) + b1_ref[...]))
    h = jnp.sin(f2_ref[...] * (_dot3(h, w2_ref[...]) + b2_ref[...]))

    @pl.when(pl.program_id(0) == 0)
    def _():
        ssq_ref[...] = jnp.zeros_like(ssq_ref)

    for i, f in enumerate(halves):
        t = f[:, 0:1]
        valid = f[:, HY_EMB:HY_EMB + 1]
        kf = _dot3(h, w3_ref[i]) * jnp.exp(-t * dl_ref[...]) * valid
        kf_ref[i * half:(i + 1) * half, :] = kf
        ssq_ref[...] += jnp.sum(kf * kf, axis=0, keepdims=True)


def _hy_filt(feats_ext, w1p, b1, f1, w2, b2, f2, w3s, deltas, L, tr=512):
    lt = L // tr
    full = lambda a: pl.BlockSpec(a.shape, lambda i: (0,) * a.ndim)
    return pl.pallas_call(
        _hy_filt_kernel,
        grid=(2 * lt,),
        in_specs=[
            pl.BlockSpec((tr, LANES), lambda i: (i, 0)),
            full(w1p), full(b1), full(f1), full(w2), full(b2), full(f2),
            pl.BlockSpec((None, 2, 2 * HY_FILTER_W, HY_W), lambda i: (i // lt, 0, 0, 0)),
            full(deltas),
        ],
        out_specs=[pl.BlockSpec((tr, HY_W), lambda i: (i, 0)), pl.BlockSpec((1, HY_W), lambda i: (0, 0))],
        out_shape=[jax.ShapeDtypeStruct((2 * L, HY_W), F32), jax.ShapeDtypeStruct((1, HY_W), F32)],
        compiler_params=_cparams("arbitrary"),
        name="hy_filt",
    )(feats_ext, w1p, b1, f1, w2, b2, f2, w3s, deltas)


def _flat_rows(ref):
    return ref.reshape(ref.shape[0] * ref.shape[1] * FFT_SUB, ref.shape[-1])


def _to_slabs(src2d, row0, slab_ref):
    n = slab_ref.shape[1]
    for c in range(slab_ref.shape[0]):
        slab_ref[c] = src2d[row0:row0 + n, c * LANES:(c + 1) * LANES]


def _outer_batch(B, outer_rows):
    return max(1, min(B, 128 // outer_rows))


def _fft1_kernel(x_ref, f_ref, o_ref, xs_ref, os_ref):
    hin, rows = x_ref.shape[1], o_ref.shape[1]
    x2, o2 = _flat_rows(x_ref), _flat_rows(o_ref)
    f = f_ref[...]
    for b in range(x_ref.shape[0]):
        _to_slabs(x2, b * hin * FFT_SUB, xs_ref)
        for c in range(xs_ref.shape[0]):
            for p in range(0, FFT_SUB, 2):
                x = jnp.concatenate([xs_ref[c, pl.ds(p, hin, stride=FFT_SUB), :],
                                     xs_ref[c, pl.ds(p + 1, hin, stride=FFT_SUB), :]], axis=1)
                xh, xl = _split2(x)
                y = _dot(f, jnp.concatenate([xh, xl, xh], axis=0))
                os_ref[c, pl.ds(p, rows, stride=FFT_SUB), :] = y[:, :LANES]
                os_ref[c, pl.ds(p + 1, rows, stride=FFT_SUB), :] = y[:, LANES:]
            o2[b * rows * FFT_SUB:(b + 1) * rows * FFT_SUB, c * LANES:(c + 1) * LANES] = os_ref[c]


def _fft1(x, fcat, B, hin):
    C = x.shape[-1]
    rows = fcat.shape[0]
    nj = FFT_N2 // FFT_SUB
    bt = _outer_batch(B, hin)
    out = pl.pallas_call(
        _fft1_kernel,
        grid=(B // bt, nj),
        in_specs=[
            pl.BlockSpec((bt, hin, None, FFT_SUB, C), lambda b, j: (b, 0, j, 0, 0)),
            pl.BlockSpec(fcat.shape, lambda b, j: (0, 0)),
        ],
        out_specs=pl.BlockSpec((bt, rows, None, FFT_SUB, C), lambda b, j: (b, 0, j, 0, 0)),
        out_shape=jax.ShapeDtypeStruct((B, rows, nj, FFT_SUB, C), F32),
        scratch_shapes=[pltpu.VMEM((C // LANES, hin * FFT_SUB, LANES), F32),
                        pltpu.VMEM((C // LANES, rows * FFT_SUB, LANES), F32)],
        compiler_params=_cparams("parallel", "parallel"),
        name="fft1",
    )(x.reshape(B, hin, nj, FFT_SUB, C), fcat)
    return out.reshape(B, rows, FFT_N2, C)


def _fft2_kernel(ar_ref, ai_ref, tr_ref, ti_ref, m_ref, o_ref):
    m = m_ref[...]
    for r in range(FFT_KB):
        ar, ai = ar_ref[r], ai_ref[r]
        tr, ti = (jnp.tile(t[r], (1, FFT_CT // LANES)) for t in (tr_ref, ti_ref))
        b = jnp.concatenate([ar * tr - ai * ti, ar * ti + ai * tr], axis=0)
        bh, bl = _split2(b)
        o_ref[r] = _dot(m, jnp.concatenate([bh, bl, bh], axis=0))


def _fft2(a, tw_r, tw_i, mcat, kp):
    B = a.shape[0]
    C = a.shape[-1]
    nk = kp // FFT_KB
    return pl.pallas_call(
        _fft2_kernel,
        grid=(B, C // FFT_CT, nk),
        in_specs=[
            pl.BlockSpec((None, FFT_KB, FFT_N2, FFT_CT), lambda b, c, k: (b, k, 0, c)),
            pl.BlockSpec((None, FFT_KB, FFT_N2, FFT_CT), lambda b, c, k: (b, nk + k, 0, c)),
            pl.BlockSpec((FFT_KB, FFT_N2, LANES), lambda b, c, k: (k, 0, 0)),
            pl.BlockSpec((FFT_KB, FFT_N2, LANES), lambda b, c, k: (k, 0, 0)),
            pl.BlockSpec(mcat.shape, lambda b, c, k: (0, 0)),
        ],
        out_specs=pl.BlockSpec((None, FFT_KB, 2 * FFT_N2, FFT_CT), lambda b, c, k: (b, k, 0, c)),
        out_shape=jax.ShapeDtypeStruct((B, kp, 2 * FFT_N2, C), F32),
        compiler_params=_cparams("parallel", "parallel", "parallel"),
        name="fft2",
    )(a, a, tw_r, tw_i, mcat)


def _ifft2_kernel(x_ref, k_ref, ssq_ref, tr_ref, ti_ref, m_ref, o_ref):
    m = m_ref[...]
    scale = lax.rsqrt(ssq_ref[...] + EPS)
    n2 = FFT_N2
    for r in range(FFT_KB):
        x = x_ref[r]
        kf = k_ref[r]
        xr, xi = x[:n2], x[n2:]
        kr, ki = kf[:n2] * scale, kf[n2:] * scale
        p = jnp.concatenate([xr * kr - xi * ki, xr * ki + xi * kr], axis=0)
        ph, plo = _split2(p)
        c = _dot(m, jnp.concatenate([ph, plo, ph], axis=0))
        cr, ci = c[:n2], c[n2:]
        tr, ti = (jnp.tile(t[r], (1, FFT_CT // LANES)) for t in (tr_ref, ti_ref))
        o_ref[0, r] = cr * tr + ci * ti
        o_ref[1, r] = ci * tr - cr * ti


def _ifft2(xs, ks, ssq, tw_r, tw_i, mcat_inv):
    B, kp, _, C = xs.shape
    nk = kp // FFT_KB
    return pl.pallas_call(
        _ifft2_kernel,
        grid=(B, C // FFT_CT, nk),
        in_specs=[
            pl.BlockSpec((None, FFT_KB, 2 * FFT_N2, FFT_CT), lambda b, c, k: (b, k, 0, c)),
            pl.BlockSpec((None, FFT_KB, 2 * FFT_N2, FFT_CT), lambda b, c, k: (0, k, 0, c)),
            pl.BlockSpec((1, FFT_CT), lambda b, c, k: (0, c)),
            pl.BlockSpec((FFT_KB, FFT_N2, LANES), lambda b, c, k: (k, 0, 0)),
            pl.BlockSpec((FFT_KB, FFT_N2, LANES), lambda b, c, k: (k, 0, 0)),
            pl.BlockSpec(mcat_inv.shape, lambda b, c, k: (0, 0)),
        ],
        out_specs=pl.BlockSpec((None, 2, FFT_KB, FFT_N2, FFT_CT), lambda b, c, k: (b, 0, k, 0, c)),
        out_shape=jax.ShapeDtypeStruct((B, 2, kp, FFT_N2, C), F32),
        compiler_params=_cparams("parallel", "parallel", "parallel"),
        name="ifft2",
    )(xs, ks, ssq, tw_r, tw_i, mcat_inv)


def _ifft1_kernel(d_ref, w_ref, u_ref, x2_ref, skip_ref, o_ref, ds_ref, ys_ref):
    rows, h = d_ref.shape[1], o_ref.shape[1]
    d2, u2, x22, o2 = (_flat_rows(r) for r in (d_ref, u_ref, x2_ref, o_ref))
    w = w_ref[...]
    for b in range(d_ref.shape[0]):
        _to_slabs(d2, b * rows * FFT_SUB, ds_ref)
        seq = slice(b * h * FFT_SUB, (b + 1) * h * FFT_SUB)
        for c in range(ds_ref.shape[0]):
            lanes = slice(c * LANES, (c + 1) * LANES)
            for p in range(0, FFT_SUB, 2):
                d = jnp.concatenate([ds_ref[c, pl.ds(p, rows, stride=FFT_SUB), :],
                                     ds_ref[c, pl.ds(p + 1, rows, stride=FFT_SUB), :]], axis=1)
                dh, dl = _split2(d)
                y = _dot(w, jnp.concatenate([dh, dl, dh], axis=0))
                ys_ref[c, pl.ds(p, h, stride=FFT_SUB), :] = y[:, :LANES]
                ys_ref[c, pl.ds(p + 1, h, stride=FFT_SUB), :] = y[:, LANES:]
            o2[seq, lanes] = x22[seq, lanes] * (ys_ref[c] + u2[seq, lanes] * skip_ref[:, lanes])


def _ifft1(d, wcat, u, x2c, skip, B):
    rows, C = d.shape[1], d.shape[-1]
    h = wcat.shape[0]
    nj = FFT_N2 // FFT_SUB
    bt = _outer_batch(B, h)
    seq = pl.BlockSpec((bt, h, None, FFT_SUB, C), lambda b, j: (b, 0, j, 0, 0))
    out = pl.pallas_call(
        _ifft1_kernel,
        grid=(B // bt, nj),
        in_specs=[
            pl.BlockSpec((bt, rows, None, FFT_SUB, C), lambda b, j: (b, 0, j, 0, 0)),
            pl.BlockSpec(wcat.shape, lambda b, j: (0, 0)),
            seq, seq,
            pl.BlockSpec((1, C), lambda b, j: (0, 0)),
        ],
        out_specs=seq,
        out_shape=jax.ShapeDtypeStruct((B, h, nj, FFT_SUB, C), F32),
        scratch_shapes=[pltpu.VMEM((C // LANES, rows * FFT_SUB, LANES), F32),
                        pltpu.VMEM((C // LANES, h * FFT_SUB, LANES), F32)],
        compiler_params=_cparams("parallel", "parallel"),
        name="ifft1",
    )(d.reshape(B, rows, nj, FFT_SUB, C), wcat, u.reshape(B, h, nj, FFT_SUB, C), x2c.reshape(B, h, nj, FFT_SUB, C),
      skip)
    return out.reshape(B * h * FFT_N2, C)


def _hilo_cat(m):
    hi = m.astype(BF16)
    lo = (m - hi.astype(F32)).astype(BF16)
    return jnp.concatenate([hi, hi, lo], axis=1)


def _fft_tables(L):
    n = 2 * L
    n2 = FFT_N2
    n1 = n // n2
    h = n1 // 2
    kp = h + FFT_KB
    k1 = jnp.arange(kp, dtype=jnp.int32)

    def outer(hin):
        nn = jnp.arange(hin, dtype=jnp.int32)
        ang = (2.0 * math.pi / n1) * ((k1[:, None] * nn[None, :]) % n1).astype(F32)
        return _hilo_cat(jnp.concatenate([jnp.cos(ang), -jnp.sin(ang)], axis=0))

    f_data = outer(h)
    f_filt = outer(n1)
    j = jnp.arange(n2, dtype=jnp.int32)
    ang2 = (2.0 * math.pi / n2) * ((j[:, None] * j[None, :]) % n2).astype(F32)
    c2, s2 = jnp.cos(ang2), jnp.sin(ang2)
    m_fwd = _hilo_cat(jnp.block([[c2, s2], [-s2, c2]]))
    m_inv = _hilo_cat(jnp.block([[c2, -s2], [s2, c2]]))
    angt = (2.0 * math.pi / n) * (k1[:, None] * j[None, :]).astype(F32)
    tw_r = jnp.broadcast_to(jnp.cos(angt)[:, :, None], (kp, n2, LANES))
    tw_i = jnp.broadcast_to(-jnp.sin(angt)[:, :, None], (kp, n2, LANES))
    wgt = jnp.where((k1 == 0) | (k1 == h), 1.0, jnp.where(k1 < h, 2.0, 0.0)).astype(F32) / n
    nn = jnp.arange(h, dtype=jnp.int32)
    angi = (2.0 * math.pi / n1) * ((nn[:, None] * k1[None, :]) % n1).astype(F32)
    w_inv = _hilo_cat(jnp.concatenate([jnp.cos(angi) * wgt, -jnp.sin(angi) * wgt], axis=1))
    return dict(n1=n1, h=h, kp=kp, f_data=f_data, f_filt=f_filt, m_fwd=m_fwd, m_inv=m_inv,
                tw_r=tw_r, tw_i=tw_i, w_inv=w_inv)


def _filter_feats(L):
    i = jnp.arange(2 * L, dtype=jnp.int32)
    n = jnp.where(i < L, i, 2 * L - i).astype(F32)
    t = (n / (L - 1))[:, None]
    w = 2.0 * math.pi * n / L
    fb = jnp.linspace(1e-4, HY_BANDS - 1, HY_BANDS, dtype=F32)
    ph = w[:, None] * fb
    valid = jnp.where(i == L, 0.0, 1.0).astype(F32)[:, None]
    ext = jnp.concatenate([t, jnp.cos(ph), -jnp.sin(ph), valid], axis=-1)
    return jnp.pad(ext, ((0, 0), (0, LANES - HY_EMB - 1)))


def _hyena_spectrum(tabs, feats_ext, lp, L):
    kf, ssq = _hy_filt(feats_ext, lp["hy_w1p"], lp["hy_b1"], lp["hy_f1"], lp["hy_w2"], lp["hy_b2"], lp["hy_f2"],
                       lp["hy_w3s"], lp["hy_deltas"], L)
    a = _fft1(kf, tabs["f_filt"], 1, tabs["n1"])
    ks = _fft2(a, tabs["tw_r"], tabs["tw_i"], tabs["m_fwd"], tabs["kp"])
    return ks, ssq


def _hyena(z, tabs, ks, ssq, lp, B, L):
    u, x2c = _hy_pre(z, lp["hy_conv_w"], lp["hy_conv_b"], L)
    kp = tabs["kp"]
    a = _fft1(u, tabs["f_data"], B, tabs["h"])
    xs = _fft2(a, tabs["tw_r"], tabs["tw_i"], tabs["m_fwd"], kp)
    d = _ifft2(xs, ks, ssq, tabs["tw_r"], tabs["tw_i"], tabs["m_inv"])
    return _ifft1(d.reshape(B, 2 * kp, FFT_N2, HY_W), tabs["w_inv"], u, x2c, lp["hy_skip"], B)


def _gla_kernel(qf_ref, kf_ref, vf_ref, lf_ref, qb_ref, kb_ref, vb_ref, lb_ref, gup_ref, gb_ref,
                of_ref, ob_ref, sf_ref, sb_ref, *, nchunk):
    C = GLA_CHUNK
    R = nchunk * C

    @pl.when(pl.program_id(1) == 0)
    def _():
        sf_ref[...] = jnp.zeros_like(sf_ref)
        sb_ref[...] = jnp.zeros_like(sb_ref)

    rr = lax.broadcasted_iota(jnp.int32, (R, R), 0)
    cc = lax.broadcasted_iota(jnp.int32, (R, R), 1)
    same_chunk = (rr // C) == (cc // C)
    lane = lax.broadcasted_iota(jnp.int32, (C, GLA_K), 1) // GLA_DK
    ri4 = lax.broadcasted_iota(jnp.int32, (GLA_HEADS * C, C), 0) % C
    ci4 = lax.broadcasted_iota(jnp.int32, (GLA_HEADS * C, C), 1)

    def stack_heads(x):
        return jnp.concatenate([jnp.where(lane == h, x, 0.0) for h in range(GLA_HEADS)], axis=0).astype(BF16)

    def per_chunk_rows(b, row):
        return jnp.concatenate([jnp.broadcast_to(b[c * C + row:c * C + row + 1], (C, GLA_K)) for c in range(nchunk)],
                               axis=0)

    dirs = []
    for d, (q_ref, k_ref, v_ref, l_ref) in enumerate(((qf_ref, kf_ref, vf_ref, lf_ref),
                                                       (qb_ref, kb_ref, vb_ref, lb_ref))):
        logit = _dot3(l_ref[...], gup_ref[:, d * GLA_K:(d + 1) * GLA_K]) + gb_ref[:, d * GLA_K:(d + 1) * GLA_K]
        la = (jnp.minimum(logit, 0.0) - jnp.log(1.0 + jnp.exp(-jnp.abs(logit)))) * (1.0 / GLA_TAU)
        if d == 0:
            tri = jnp.where(same_chunk & (rr >= cc), 1.0, 0.0).astype(BF16)
            keep = ri4 >= ci4
            mid, last = C // 2 - 1, C - 1
            order = list(range(nchunk))
        else:
            tri = jnp.where(same_chunk & (cc >= rr), 1.0, 0.0).astype(BF16)
            keep = ci4 > ri4
            mid, last = C // 2, 0
            order = list(range(nchunk - 1, -1, -1))
        hi, lo = _split2(la)
        lo2 = (la - hi.astype(F32) - lo.astype(F32)).astype(BF16)
        b = _dot(tri, hi) + _dot(tri, lo) + _dot(tri, lo2)
        b_mid = per_chunk_rows(b, mid)
        b_last = per_chunk_rows(b, last)
        q = q_ref[...] * (GLA_DK ** -0.5)
        k = k_ref[...]
        dirs.append(dict(
            keep=keep, order=order, v=v_ref[...].astype(BF16),
            qt=q * jnp.exp(b - b_mid), kt=(k * jnp.exp(b_mid - b)).astype(BF16),
            kh=k * jnp.exp(b_last - b), qh=q * jnp.exp(b), dec=jnp.exp(b_last)))

    for dd in dirs:
        dd["a"] = []
        for c in range(nchunk):
            rs = slice(c * C, (c + 1) * C)
            a = lax.dot_general(stack_heads(dd["qt"][rs]), dd["kt"][rs], (((1,), (1,)), ((), ())),
                                preferred_element_type=F32)
            dd["a"].append(jnp.where(dd["keep"], a, 0.0).astype(BF16))

    for dd in dirs:
        dd["o"], dd["u"] = [], []
        for c in range(nchunk):
            rs = slice(c * C, (c + 1) * C)
            k_t = jnp.transpose(dd["kh"][rs]).astype(BF16)
            outs, ups = [], []
            for h in range(GLA_HEADS):
                vh = dd["v"][rs, h * GLA_DV:(h + 1) * GLA_DV]
                outs.append(_dot(dd["a"][c][h * C:(h + 1) * C], vh))
                ups.append(_dot(k_t[h * GLA_DK:(h + 1) * GLA_DK], vh))
            dd["o"].append(outs)
            dd["u"].append(jnp.concatenate(ups, axis=0))

    for dd, s_ref in zip(dirs, (sf_ref, sb_ref)):
        s = s_ref[...]
        dd["s_prev"] = {}
        for c in dd["order"]:
            dd["s_prev"][c] = s.astype(BF16)
            decay = jnp.transpose(jnp.broadcast_to(dd["dec"][c * C:c * C + 1], (GLA_DV, GLA_K)))
            s = decay * s + dd["u"][c]
        s_ref[...] = s

    for dd, o_ref in zip(dirs, (of_ref, ob_ref)):
        for c in range(nchunk):
            rs = slice(c * C, (c + 1) * C)
            o_inter = _dot(stack_heads(dd["qh"][rs]), dd["s_prev"][c])
            o_ref[rs, :] = jnp.concatenate(
                [dd["o"][c][h] + o_inter[h * C:(h + 1) * C] for h in range(GLA_HEADS)], axis=1)


def _gla(z, gup, gb, B, L, nchunk=4):
    n = z.shape[0]
    R = nchunk * GLA_CHUNK
    T = L // R
    fwd = lambda b, t: b * T + t
    bwd = lambda b, t: b * T + (T - 1 - t)

    def specs(row):
        return [
            pl.BlockSpec((R, GLA_K), lambda b, t: (row(b, t), Z_GQ // GLA_K)),
            pl.BlockSpec((R, GLA_K), lambda b, t: (row(b, t), Z_GK // GLA_K)),
            pl.BlockSpec((R, GLA_V), lambda b, t: (row(b, t), Z_GV // GLA_V)),
            pl.BlockSpec((R, LANES), lambda b, t: (row(b, t), Z_GLOW // LANES)),
        ]

    return pl.pallas_call(
        functools.partial(_gla_kernel, nchunk=nchunk),
        grid=(B, T),
        in_specs=specs(fwd) + specs(bwd) + [
            pl.BlockSpec(gup.shape, lambda b, t: (0, 0)),
            pl.BlockSpec(gb.shape, lambda b, t: (0, 0)),
        ],
        out_specs=[
            pl.BlockSpec((R, GLA_V), lambda b, t: (fwd(b, t), 0)),
            pl.BlockSpec((R, GLA_V), lambda b, t: (bwd(b, t), 0)),
        ],
        out_shape=[jax.ShapeDtypeStruct((n, GLA_V), F32)] * 2,
        scratch_shapes=[pltpu.VMEM((GLA_K, GLA_DV), F32)] * 2,
        compiler_params=_cparams("parallel", "arbitrary"),
        name="gla",
    )(*([z] * 8), gup, gb)


def _mix_out_kernel(x_ref, ya_ref, yh_ref, of_ref, ob_ref, og_ref, g0_ref, g1_ref, g2_ref,
                    wb_ref, wo_ref, gn_ref, o_ref):
    o = of_ref[...] + ob_ref[...]
    og = og_ref[...]
    gn = gn_ref[...]
    parts = []
    for h in range(GLA_HEADS):
        blk = o[:, h * GLA_DV:(h + 1) * GLA_DV]
        ms = jnp.mean(blk * blk, axis=-1, keepdims=True)
        gate = og[:, h * GLA_DV:(h + 1) * GLA_DV]
        parts.append(blk * lax.rsqrt(ms + EPS) * gn * (gate * jax.nn.sigmoid(gate)))
    y_gla = jnp.concatenate(parts, axis=1)
    proj_att = lax.dot_general(ya_ref[...].astype(BF16), wb_ref[0], (((0,), (0,)), ((), ())),
                               preferred_element_type=F32)
    merged = jax.nn.sigmoid(g0_ref[...].astype(F32)) * proj_att
    merged += jax.nn.sigmoid(g1_ref[...].astype(F32)) * _dot(yh_ref[...].astype(BF16), wb_ref[1])
    merged += jax.nn.sigmoid(g2_ref[...].astype(F32)) * _dot(y_gla.astype(BF16), wb_ref[2])
    o_ref[...] = x_ref[...] + _dot(merged.astype(BF16), wo_ref[...])


def _mix_out(x, y_att, y_hy, o_f, o_b, z, zg, wb, wo, gn, tm=256):
    n = x.shape[0]
    row = lambda w: pl.BlockSpec((tm, w), lambda i: (i, 0))
    return pl.pallas_call(
        _mix_out_kernel,
        grid=(n // tm,),
        in_specs=[
            row(D_MODEL), pl.BlockSpec((MIX_W, tm), lambda i: (0, i)), row(MIX_W), row(MIX_W), row(MIX_W),
            pl.BlockSpec((tm, GLA_V), lambda i: (i, Z_GOG // GLA_V)),
            pl.BlockSpec((tm, D_MODEL), lambda i: (i, 0)),
            pl.BlockSpec((tm, D_MODEL), lambda i: (i, 1)),
            pl.BlockSpec((tm, D_MODEL), lambda i: (i, 2)),
            pl.BlockSpec(wb.shape, lambda i: (0, 0, 0)),
            pl.BlockSpec(wo.shape, lambda i: (0, 0)),
            pl.BlockSpec((1, GLA_DV), lambda i: (0, 0)),
        ],
        out_specs=row(D_MODEL),
        out_shape=jax.ShapeDtypeStruct((n, D_MODEL), F32),
        compiler_params=_cparams("parallel"),
        name="mix_out",
    )(x, y_att, y_hy, o_f, o_b, z, zg, zg, zg, wb, wo, gn)


def _ffn_kernel(x_ref, g_ref, wg_ref, wu_ref, wd_ref, o_ref, *, tf):
    x = x_ref[...]
    ms = jnp.mean(x * x, axis=-1, keepdims=True)
    h = (x * lax.rsqrt(ms + EPS) * g_ref[...]).astype(BF16)
    acc = x
    for c in range(0, D_FF, tf):
        a = _dot(h, wg_ref[:, c:c + tf])
        act = (a * jax.nn.sigmoid(a)) * _dot(h, wu_ref[:, c:c + tf])
        acc = acc + _dot(act.astype(BF16), wd_ref[c:c + tf, :])
    o_ref[...] = acc


def _ffn(x, g, wg, wu, wd, tm=512, tf=1408):
    n = x.shape[0]
    resident = lambda w: pl.BlockSpec(w.shape, lambda i: (0, 0), pipeline_mode=pl.Buffered(1))
    return pl.pallas_call(
        functools.partial(_ffn_kernel, tf=tf),
        grid=(n // tm,),
        in_specs=[
            pl.BlockSpec((tm, D_MODEL), lambda i: (i, 0)),
            pl.BlockSpec((1, D_MODEL), lambda i: (0, 0)),
            resident(wg), resident(wu), resident(wd),
        ],
        out_specs=pl.BlockSpec((tm, D_MODEL), lambda i: (i, 0)),
        out_shape=jax.ShapeDtypeStruct((n, D_MODEL), F32),
        compiler_params=_cparams("parallel"),
        name="ffn",
    )(x, g, wg, wu, wd)


def _final_norm_kernel(x_ref, g_ref, o_ref):
    x = x_ref[...]
    ms = jnp.mean(x * x, axis=-1, keepdims=True)
    o_ref[...] = x * lax.rsqrt(ms + EPS) * g_ref[...]


def _final_norm(x, g, tm=512):
    n = x.shape[0]
    return pl.pallas_call(
        _final_norm_kernel,
        grid=(n // tm,),
        in_specs=[pl.BlockSpec((tm, D_MODEL), lambda i: (i, 0)), pl.BlockSpec((1, D_MODEL), lambda i: (0, 0))],
        out_specs=pl.BlockSpec((tm, D_MODEL), lambda i: (i, 0)),
        out_shape=jax.ShapeDtypeStruct((n, D_MODEL), F32),
        compiler_params=_cparams("parallel"),
        name="final_norm",
    )(x, g)


def _rope_tables(L):
    rows = L // GRID_W
    r = jnp.repeat(jnp.arange(rows, dtype=F32), GRID_W)
    c = jnp.tile(jnp.arange(GRID_W, dtype=F32), rows)
    inv = ROPE_THETA ** (-jnp.arange(0, ROPE_AXIS_DIM, 2, dtype=F32) / ROPE_AXIS_DIM)
    ang = jnp.concatenate([r[:, None] * inv, c[:, None] * inv], axis=-1)
    cos, sin = jnp.cos(ang), jnp.sin(ang)
    cos_t = jnp.tile(cos, (1, 2 * LANES // HEAD_DIM))
    sin_t = jnp.tile(jnp.concatenate([-sin, sin], axis=-1), (1, LANES // HEAD_DIM))
    return cos_t, sin_t


def _layer_params(l, norm_mix_g, w_in, q_norm_g, k_norm_g, hy_conv_w, hy_conv_b, hy_w1, hy_b1, hy_f1, hy_w2,
                  hy_b2, hy_f2, hy_w3, hy_skip, gla_gate_up, gla_gate_b, gla_norm_g, w_branch, w_out, norm_ffn_g,
                  w_ffn_gate, w_ffn_up, w_ffn_down):
    w = w_in[l]
    offs = {}
    off = 0
    for name, size in (("aq", ATT_W), ("ak", KV_W), ("av", KV_W), ("hz", 3 * HY_W), ("gq", GLA_K), ("gk", GLA_K),
                       ("gv", GLA_V), ("gog", GLA_V), ("glow", 2 * GLA_RANK), ("gates", N_BRANCH * D_MODEL)):
        offs[name] = w[:, off:off + size]
        off += size
    w_cat = jnp.concatenate(
        [offs["aq"], offs["gv"], offs["gog"], offs["hz"], offs["gq"], offs["gk"], offs["ak"],
         offs["av"], offs["glow"], jnp.zeros((D_MODEL, Z_COLS - Z_GLOW - 2 * GLA_RANK), F32)], axis=1).astype(BF16)
    gup = jnp.zeros((LANES, 2 * GLA_K), F32)
    gup = gup.at[0:GLA_RANK, 0:GLA_K].set(gla_gate_up[l, 0])
    gup = gup.at[GLA_RANK:2 * GLA_RANK, GLA_K:2 * GLA_K].set(gla_gate_up[l, 1])
    deltas = jnp.linspace(abs(math.log(HY_TARGET) / HY_SLOW), abs(math.log(HY_TARGET) / HY_FAST), HY_W, dtype=F32)
    return dict(
        norm_mix_g=norm_mix_g[l][None, :],
        w_cat=w_cat, w_gates=offs["gates"].astype(BF16),
        gq=jnp.tile(q_norm_g[l], LANES // HEAD_DIM)[None, :],
        gk=jnp.tile(k_norm_g[l], LANES // HEAD_DIM)[None, :],
        hy_conv_w=jnp.transpose(hy_conv_w[l].reshape(3, 3, HY_W), (1, 0, 2)),
        hy_conv_b=hy_conv_b[l].reshape(3, HY_W),
        hy_w1p=jnp.kron(jnp.eye(2, dtype=F32), jnp.pad(hy_w1[l], ((0, LANES - HY_EMB), (0, 0)))),
        hy_b1=jnp.tile(hy_b1[l], 2)[None, :], hy_f1=jnp.tile(hy_f1[l], 2)[None, :],
        hy_w2=jnp.kron(jnp.eye(2, dtype=F32), hy_w2[l]),
        hy_b2=jnp.tile(hy_b2[l], 2)[None, :], hy_f2=jnp.tile(hy_f2[l], 2)[None, :],
        hy_w3s=jnp.einsum("gh,kfc->fghkc", jnp.eye(2, dtype=F32), hy_w3[l].reshape(HY_FILTER_W, 2, HY_W)
                          ).reshape(2, 2, 2 * HY_FILTER_W, HY_W),
        hy_deltas=deltas[None, :],
        hy_skip=hy_skip[l][None, :],
        gup=gup, gb=gla_gate_b[l].reshape(1, 2 * GLA_K),
        gn=gla_norm_g[l][None, :],
        wb=w_branch[l].astype(BF16), wo=w_out[l].astype(BF16),
        norm_ffn_g=norm_ffn_g[l][None, :],
        wg=w_ffn_gate[l].astype(BF16), wu=w_ffn_up[l].astype(BF16), wd=w_ffn_down[l].astype(BF16),
    )


def _encoder_layer(x, lp, B, L, rope, tabs, ks, ssq):
    z, zg = _in_proj(x, lp["norm_mix_g"], lp["w_cat"], lp["w_gates"])
    qt, kr, vt = _attn_prep(z, rope[0], rope[1], lp["gq"], lp["gk"], L)
    y_att = _flash(qt, kr, vt, B, L)
    y_hy = _hyena(z, tabs, ks, ssq, lp, B, L)
    o_f, o_b = _gla(z, lp["gup"], lp["gb"], B, L)
    x = _mix_out(x, y_att, y_hy, o_f, o_b, z, zg, lp["wb"], lp["wo"], lp["gn"])
    return _ffn(x, lp["norm_ffn_g"], lp["wg"], lp["wu"], lp["wd"])


def kernel(x_prompt, x_sample, norm_mix_g, w_in, q_norm_g, k_norm_g, hy_conv_w, hy_conv_b, hy_w1, hy_b1, hy_f1, hy_w2, hy_b2, hy_f2, hy_w3, hy_skip, gla_gate_up, gla_gate_b, gla_norm_g, w_branch, w_out, norm_ffn_g, w_ffn_gate, w_ffn_up, w_ffn_down, final_norm_g):
    streams = []
    for xin in (x_prompt, x_sample):
        B, L, _ = xin.shape
        streams.append(dict(x=xin.reshape(B * L, D_MODEL), B=B, L=L, rope=_rope_tables(L), tabs=_fft_tables(L),
                            feats=_filter_feats(L)))
    for l in range(DEPTH):
        lp = _layer_params(l, norm_mix_g, w_in, q_norm_g, k_norm_g, hy_conv_w, hy_conv_b, hy_w1, hy_b1, hy_f1,
                           hy_w2, hy_b2, hy_f2, hy_w3, hy_skip, gla_gate_up, gla_gate_b, gla_norm_g, w_branch,
                           w_out, norm_ffn_g, w_ffn_gate, w_ffn_up, w_ffn_down)
        for s in streams:
            ks, ssq = _hyena_spectrum(s["tabs"], s["feats"], lp, s["L"])
            s["x"] = _encoder_layer(s["x"], lp, s["B"], s["L"], s["rope"], s["tabs"], ks, ssq)
    outs = []
    for s, xin in zip(streams, (x_prompt, x_sample)):
        outs.append(_final_norm(s["x"], final_norm_g[None, :]).reshape(xin.shape))
    return tuple(outs)
```

```python
import functools
import math

import jax
import jax.numpy as jnp
from jax import lax
from jax.experimental import pallas as pl
from jax.experimental.pallas import tpu as pltpu

F32 = jnp.float32
BF16 = jnp.bfloat16

D_MODEL = 1024
DEPTH = 4
GRID_W = 64
N_HEADS = 8
N_KV_HEADS = 2
HEAD_DIM = 64
GROUP = N_HEADS // N_KV_HEADS
ROPE_THETA = 10000.0
ROPE_AXIS_DIM = HEAD_DIM // 2
ATT_W = N_HEADS * HEAD_DIM
KV_W = N_KV_HEADS * HEAD_DIM
HY_W = D_MODEL // 2
HY_EMB = 33
HY_BANDS = (HY_EMB - 1) // 2
HY_FILTER_W = 64
HY_TARGET = 1e-2
HY_FAST = 0.3
HY_SLOW = 1.5
GLA_HEADS = 4
GLA_DK = 64
GLA_DV = 128
GLA_K = GLA_HEADS * GLA_DK
GLA_V = GLA_HEADS * GLA_DV
GLA_RANK = 16
GLA_TAU = 16.0
GLA_CHUNK = 64
N_BRANCH = 3
MIX_W = 512
D_FF = 2816
EPS = 1e-6

LANES = 128
VMEM_LIMIT = 56 * 1024 * 1024

Z_AQ = 0
Z_GV = 512
Z_GOG = 1024
Z_HZ = 1536
Z_GQ = 3072
Z_GK = 3328
Z_AK = 3584
Z_AV = 3712
Z_GLOW = 3840
Z_COLS = 4096

FFT_N2 = 128
FFT_KB = 8
FFT_SUB = 8
FFT_CT = 512


def _cparams(*sem):
    return pltpu.CompilerParams(dimension_semantics=sem, vmem_limit_bytes=VMEM_LIMIT)


def _split2(x):
    hi = x.astype(BF16)
    lo = (x - hi.astype(F32)).astype(BF16)
    return hi, lo


def _dot(a, b):
    return jnp.dot(a, b, preferred_element_type=F32)


def _dot3(a, b):
    ah, al = _split2(a)
    bh, bl = _split2(b)
    return _dot(ah, bh) + _dot(al, bh) + _dot(ah, bl)


def _in_proj_kernel(x_ref, g_ref, w_ref, wg_ref, z_ref, zg_ref, *, tn):
    x = x_ref[...]
    ms = jnp.mean(x * x, axis=-1, keepdims=True)
    h = (x * lax.rsqrt(ms + EPS) * g_ref[...]).astype(BF16)
    for c in range(0, z_ref.shape[1], tn):
        z_ref[:, c:c + tn] = _dot(h, w_ref[:, c:c + tn])
    for c in range(0, zg_ref.shape[1], tn):
        zg_ref[:, c:c + tn] = _dot(h, wg_ref[:, c:c + tn]).astype(zg_ref.dtype)


def _in_proj(x, g, w, w_gates, tm=512, tn=1024):
    n = x.shape[0]
    resident = lambda a: pl.BlockSpec(a.shape, lambda i: (0, 0), pipeline_mode=pl.Buffered(1))
    row = lambda a: pl.BlockSpec((tm, a.shape[1]), lambda i: (i, 0))
    return pl.pallas_call(
        functools.partial(_in_proj_kernel, tn=tn),
        grid=(n // tm,),
        in_specs=[row(x), pl.BlockSpec((1, D_MODEL), lambda i: (0, 0)), resident(w), resident(w_gates)],
        out_specs=[row(w), row(w_gates)],
        out_shape=[jax.ShapeDtypeStruct((n, w.shape[1]), F32), jax.ShapeDtypeStruct((n, w_gates.shape[1]), BF16)],
        compiler_params=_cparams("parallel"),
        name="in_proj",
    )(x, g, w, w_gates)


V_ROWS = HEAD_DIM + 16
Q_SCALE = (HEAD_DIM ** -0.5) * math.log2(math.e)


def _attn_prep_kernel(q_ref, k_ref, v_ref, cos_ref, sin_ref, gq_ref, gk_ref, qt_ref, kr_ref, vt_ref):
    tm = q_ref.shape[0]
    cos = cos_ref[...]
    sin = sin_ref[...]
    lane = lax.broadcasted_iota(jnp.int32, (tm, LANES), 1)
    first_half = (lane % HEAD_DIM) < (HEAD_DIM // 2)
    left = lane < HEAD_DIM
    r = lax.broadcasted_iota(jnp.int32, (LANES, LANES), 0) // HEAD_DIM
    c = lax.broadcasted_iota(jnp.int32, (LANES, LANES), 1) // HEAD_DIM
    head_ones = jnp.where(r == c, 1.0, 0.0).astype(BF16)

    def norm_rope(x, g):
        xsq = x * x
        hi, lo = _split2(xsq)
        lo2 = (xsq - hi.astype(F32) - lo.astype(F32)).astype(BF16)
        ssq = _dot(hi, head_ones) + _dot(lo, head_ones) + _dot(lo2, head_ones)
        y = x * lax.rsqrt(ssq * (1.0 / HEAD_DIM) + EPS) * g
        other = jnp.where(first_half, pltpu.roll(y, LANES - HEAD_DIM // 2, 1), pltpu.roll(y, HEAD_DIM // 2, 1))
        return y * cos + other * sin

    gq = gq_ref[...]
    for jb in range(ATT_W // LANES):
        y = norm_rope(q_ref[:, jb * LANES:(jb + 1) * LANES], gq) * Q_SCALE
        ys = pltpu.roll(y, HEAD_DIM, 1)
        if (2 * jb) // GROUP == 0:
            e0 = jnp.where(left, y, 0.0)
            e1 = jnp.where(left, ys, 0.0)
        else:
            e0 = jnp.where(left, 0.0, ys)
            e1 = jnp.where(left, 0.0, y)
        qt_ref[2 * jb] = jnp.transpose(e0).astype(BF16)
        qt_ref[2 * jb + 1] = jnp.transpose(e1).astype(BF16)
    kr_ref[...] = norm_rope(k_ref[...], gk_ref[...]).astype(BF16)
    vt = jnp.transpose(v_ref[...])
    ones = jnp.ones((V_ROWS - HEAD_DIM, tm), BF16)
    for kv in range(N_KV_HEADS):
        vt_ref[kv, 0:HEAD_DIM, :] = vt[kv * HEAD_DIM:(kv + 1) * HEAD_DIM].astype(BF16)
        vt_ref[kv, HEAD_DIM:V_ROWS, :] = ones


def _attn_prep(z, cos_t, sin_t, gq, gk, L, tm=512):
    n = z.shape[0]
    lt = L // tm
    return pl.pallas_call(
        _attn_prep_kernel,
        grid=(n // tm,),
        in_specs=[
            pl.BlockSpec((tm, ATT_W), lambda i: (i, Z_AQ // ATT_W)),
            pl.BlockSpec((tm, KV_W), lambda i: (i, Z_AK // KV_W)),
            pl.BlockSpec((tm, KV_W), lambda i: (i, Z_AV // KV_W)),
            pl.BlockSpec((tm, LANES), lambda i: (i % lt, 0)),
            pl.BlockSpec((tm, LANES), lambda i: (i % lt, 0)),
            pl.BlockSpec((1, LANES), lambda i: (0, 0)),
            pl.BlockSpec((1, LANES), lambda i: (0, 0)),
        ],
        out_specs=[
            pl.BlockSpec((N_HEADS, LANES, tm), lambda i: (0, 0, i)),
            pl.BlockSpec((tm, KV_W), lambda i: (i, 0)),
            pl.BlockSpec((N_KV_HEADS, V_ROWS, tm), lambda i: (0, 0, i)),
        ],
        out_shape=[
            jax.ShapeDtypeStruct((N_HEADS, LANES, n), BF16),
            jax.ShapeDtypeStruct((n, KV_W), BF16),
            jax.ShapeDtypeStruct((N_KV_HEADS, V_ROWS, n), BF16),
        ],
        compiler_params=_cparams("parallel"),
        name="attn_prep",
    )(z, z, z, cos_t, sin_t, gq, gk)


def _flash_kernel(qt_ref, k_ref, vt_ref, o_ref, s_ref, p_ref, *, tk):
    tq = qt_ref.shape[2]
    L = k_ref.shape[0]
    nt = L // tk
    heads = range(N_HEADS)

    def scores(j, slot):
        kj = k_ref[pl.ds(pl.multiple_of(j * tk, tk), tk), :]
        for h in heads:
            s_ref[slot, h] = _dot(kj, qt_ref[h])

    def values(j, slot, alphas, accs):
        start = pl.multiple_of(j * tk, tk)
        vjs = [vt_ref[kv, :, pl.ds(start, tk)] for kv in range(N_KV_HEADS)]
        return [alphas[h] * accs[h] + _dot(vjs[h // GROUP], p_ref[slot, h]) for h in heads]

    scores(0, 0)
    p_ref[1] = jnp.zeros(p_ref.shape[1:], BF16)

    def half_step(jv, js, other, cur, ms, alphas, accs):
        kj = k_ref[pl.ds(pl.multiple_of(js * tk, tk), tk), :]
        start = pl.multiple_of(jv * tk, tk)
        vjs = [vt_ref[kv, :, pl.ds(start, tk)] for kv in range(N_KV_HEADS)]
        ms, alphas, accs = list(ms), list(alphas), list(accs)
        for h in heads:
            accs[h] = alphas[h] * accs[h] + _dot(vjs[h // GROUP], p_ref[other, h])
            s = s_ref[cur, h]
            m_new = jnp.maximum(ms[h], jnp.max(s, axis=0, keepdims=True))
            alphas[h] = jnp.exp2(ms[h] - m_new)
            ms[h] = m_new
            p_ref[cur, h] = jnp.exp2(s - m_new).astype(BF16)
            s_ref[other, h] = _dot(kj, qt_ref[h])
        return ms, alphas, accs

    def body(jj, carry):
        ms, alphas, accs = carry
        j = 2 * jj
        ms, alphas, accs = half_step(jnp.maximum(j - 1, 0), j + 1, 1, 0, ms, alphas, accs)
        return half_step(j, jnp.minimum(j + 2, nt - 1), 0, 1, ms, alphas, accs)

    init = ([jnp.full((1, tq), -jnp.inf, F32)] * N_HEADS, [jnp.ones((1, tq), F32)] * N_HEADS,
            [jnp.zeros((V_ROWS, tq), F32)] * N_HEADS)
    _, alphas, accs = lax.fori_loop(0, nt // 2, body, init)
    accs = values(nt - 1, 1, alphas, accs)
    for h in heads:
        o_ref[h * HEAD_DIM:(h + 1) * HEAD_DIM, :] = (accs[h][0:HEAD_DIM] / accs[h][HEAD_DIM:HEAD_DIM + 1]
                                                     ).astype(o_ref.dtype)


def _flash(qt, kr, vt, B, L, tq=256, tk=512):
    n = kr.shape[0]
    lt = L // tq
    assert (L // tk) % 2 == 0
    return pl.pallas_call(
        functools.partial(_flash_kernel, tk=tk),
        grid=(B, lt),
        in_specs=[
            pl.BlockSpec((N_HEADS, LANES, tq), lambda b, i: (0, 0, b * lt + i)),
            pl.BlockSpec((L, KV_W), lambda b, i: (b, 0)),
            pl.BlockSpec((N_KV_HEADS, V_ROWS, L), lambda b, i: (0, 0, b)),
        ],
        out_specs=pl.BlockSpec((ATT_W, tq), lambda b, i: (0, b * lt + i)),
        out_shape=jax.ShapeDtypeStruct((ATT_W, n), BF16),
        scratch_shapes=[pltpu.VMEM((2, N_HEADS, tk, tq), F32), pltpu.VMEM((2, N_HEADS, tk, tq), BF16)],
        compiler_params=_cparams("parallel", "parallel"),
        name="flash",
    )(qt, kr, vt)


def _hy_pre_kernel(v_ref, vp_ref, vn_ref, a_ref, ap_ref, an_ref, b_ref, bp_ref, bn_ref, w_ref, cb_ref,
                   u_ref, x2_ref, *, lt):
    tm = v_ref.shape[0]
    i = pl.program_id(0)
    first = (i % lt) == 0
    last = (i % lt) == lt - 1
    rows = lax.broadcasted_iota(jnp.int32, v_ref.shape, 0)

    def conv3(m_ref, p_ref, n_ref, part):
        x = m_ref[...]
        prev = jnp.where(first, 0.0, p_ref[7:8, :])
        nxt = jnp.where(last, 0.0, n_ref[0:1, :])
        dn = jnp.where(rows == 0, prev, pltpu.roll(x, 1, 0))
        up = jnp.where(rows == tm - 1, nxt, pltpu.roll(x, tm - 1, 0))
        w = w_ref[part]
        return dn * w[0:1] + x * w[1:2] + up * w[2:3] + cb_ref[part:part + 1, :]

    u_ref[...] = conv3(v_ref, vp_ref, vn_ref, 0) * conv3(a_ref, ap_ref, an_ref, 1)
    x2_ref[...] = conv3(b_ref, bp_ref, bn_ref, 2)


def _hy_pre(z, conv_w, conv_b, L, tm=512, tc=512):
    n = z.shape[0]
    lt = L // tm
    r8 = tm // 8
    nb8 = n // 8
    specs = []
    for part in range(3):
        c0 = (Z_HZ + part * HY_W) // tc
        specs.append(pl.BlockSpec((tm, tc), lambda i, c, c0=c0: (i, c0 + c)))
        specs.append(pl.BlockSpec((8, tc), lambda i, c, c0=c0: (jnp.maximum(i * r8 - 1, 0), c0 + c)))
        specs.append(pl.BlockSpec((8, tc), lambda i, c, c0=c0: (jnp.minimum((i + 1) * r8, nb8 - 1), c0 + c)))
    specs.append(pl.BlockSpec((3, 3, tc), lambda i, c: (0, 0, c)))
    specs.append(pl.BlockSpec((3, tc), lambda i, c: (0, c)))
    return pl.pallas_call(
        functools.partial(_hy_pre_kernel, lt=lt),
        grid=(n // tm, HY_W // tc),
        in_specs=specs,
        out_specs=[pl.BlockSpec((tm, tc), lambda i, c: (i, c))] * 2,
        out_shape=[jax.ShapeDtypeStruct((n, HY_W), F32)] * 2,
        compiler_params=_cparams("parallel", "parallel"),
        name="hy_pre",
    )(*([z] * 9), conv_w, conv_b)


def _hy_filt_kernel(f_ref, w1_ref, b1_ref, f1_ref, w2_ref, b2_ref, f2_ref, w3_ref, dl_ref, kf_ref, ssq_ref):
    feats = f_ref[...]
    half = feats.shape[0] // 2
    halves = (feats[:half], feats[half:])
    h = jnp.sin(f1_ref[...] * (_dot3(jnp.concatenate(halves, axis=1), w1_ref[...]) + b1_ref[...]))
    h = jnp.sin(f2_ref[...] * (_dot3(h, w2_ref[...]) + b2_ref[...]))

    @pl.when(pl.program_id(0) == 0)
    def _():
        ssq_ref[...] = jnp.zeros_like(ssq_ref)

    for i, f in enumerate(halves):
        t = f[:, 0:1]
        valid = f[:, HY_EMB:HY_EMB + 1]
        kf = _dot3(h, w3_ref[i]) * jnp.exp(-t * dl_ref[...]) * valid
        kf_ref[i * half:(i + 1) * half, :] = kf
        ssq_ref[...] += jnp.sum(kf * kf, axis=0, keepdims=True)


def _hy_filt(feats_ext, w1p, b1, f1, w2, b2, f2, w3s, deltas, L, tr=512):
    lt = L // tr
    full = lambda a: pl.BlockSpec(a.shape, lambda i: (0,) * a.ndim)
    return pl.pallas_call(
        _hy_filt_kernel,
        grid=(2 * lt,),
        in_specs=[
            pl.BlockSpec((tr, LANES), lambda i: (i, 0)),
            full(w1p), full(b1), full(f1), full(w2), full(b2), full(f2),
            pl.BlockSpec((None, 2, 2 * HY_FILTER_W, HY_W), lambda i: (i // lt, 0, 0, 0)),
            full(deltas),
        ],
        out_specs=[pl.BlockSpec((tr, HY_W), lambda i: (i, 0)), pl.BlockSpec((1, HY_W), lambda i: (0, 0))],
        out_shape=[jax.ShapeDtypeStruct((2 * L, HY_W), F32), jax.ShapeDtypeStruct((1, HY_W), F32)],
        compiler_params=_cparams("arbitrary"),
        name="hy_filt",
    )(feats_ext, w1p, b1, f1, w2, b2, f2, w3s, deltas)


def _flat_rows(ref):
    return ref.reshape(ref.shape[0] * ref.shape[1] * FFT_SUB, ref.shape[-1])


def _to_slabs(src2d, row0, slab_ref):
    n = slab_ref.shape[1]
    for c in range(slab_ref.shape[0]):
        slab_ref[c] = src2d[row0:row0 + n, c * LANES:(c + 1) * LANES]


def _outer_batch(B, outer_rows):
    return max(1, min(B, 128 // outer_rows))


def _fft1_kernel(x_ref, f_ref, o_ref, xs_ref, os_ref):
    hin, rows = x_ref.shape[1], o_ref.shape[1]
    x2, o2 = _flat_rows(x_ref), _flat_rows(o_ref)
    f = f_ref[...]
    for b in range(x_ref.shape[0]):
        _to_slabs(x2, b * hin * FFT_SUB, xs_ref)
        for c in range(xs_ref.shape[0]):
            for p in range(0, FFT_SUB, 2):
                x = jnp.concatenate([xs_ref[c, pl.ds(p, hin, stride=FFT_SUB), :],
                                     xs_ref[c, pl.ds(p + 1, hin, stride=FFT_SUB), :]], axis=1)
                xh, xl = _split2(x)
                y = _dot(f, jnp.concatenate([xh, xl, xh], axis=0))
                os_ref[c, pl.ds(p, rows, stride=FFT_SUB), :] = y[:, :LANES]
                os_ref[c, pl.ds(p + 1, rows, stride=FFT_SUB), :] = y[:, LANES:]
            o2[b * rows * FFT_SUB:(b + 1) * rows * FFT_SUB, c * LANES:(c + 1) * LANES] = os_ref[c]


def _fft1(x, fcat, B, hin):
    C = x.shape[-1]
    rows = fcat.shape[0]
    nj = FFT_N2 // FFT_SUB
    bt = _outer_batch(B, hin)
    out = pl.pallas_call(
        _fft1_kernel,
        grid=(B // bt, nj),
        in_specs=[
            pl.BlockSpec((bt, hin, None, FFT_SUB, C), lambda b, j: (b, 0, j, 0, 0)),
            pl.BlockSpec(fcat.shape, lambda b, j: (0, 0)),
        ],
        out_specs=pl.BlockSpec((bt, rows, None, FFT_SUB, C), lambda b, j: (b, 0, j, 0, 0)),
        out_shape=jax.ShapeDtypeStruct((B, rows, nj, FFT_SUB, C), F32),
        scratch_shapes=[pltpu.VMEM((C // LANES, hin * FFT_SUB, LANES), F32),
                        pltpu.VMEM((C // LANES, rows * FFT_SUB, LANES), F32)],
        compiler_params=_cparams("parallel", "parallel"),
        name="fft1",
    )(x.reshape(B, hin, nj, FFT_SUB, C), fcat)
    return out.reshape(B, rows, FFT_N2, C)


def _fft2_kernel(ar_ref, ai_ref, tr_ref, ti_ref, m_ref, o_ref):
    m = m_ref[...]
    for r in range(FFT_KB):
        ar, ai = ar_ref[r], ai_ref[r]
        tr, ti = (jnp.tile(t[r], (1, FFT_CT // LANES)) for t in (tr_ref, ti_ref))
        b = jnp.concatenate([ar * tr - ai * ti, ar * ti + ai * tr], axis=0)
        bh, bl = _split2(b)
        o_ref[r] = _dot(m, jnp.concatenate([bh, bl, bh], axis=0))


def _fft2(a, tw_r, tw_i, mcat, kp):
    B = a.shape[0]
    C = a.shape[-1]
    nk = kp // FFT_KB
    return pl.pallas_call(
        _fft2_kernel,
        grid=(B, C // FFT_CT, nk),
        in_specs=[
            pl.BlockSpec((None, FFT_KB, FFT_N2, FFT_CT), lambda b, c, k: (b, k, 0, c)),
            pl.BlockSpec((None, FFT_KB, FFT_N2, FFT_CT), lambda b, c, k: (b, nk + k, 0, c)),
            pl.BlockSpec((FFT_KB, FFT_N2, LANES), lambda b, c, k: (k, 0, 0)),
            pl.BlockSpec((FFT_KB, FFT_N2, LANES), lambda b, c, k: (k, 0, 0)),
            pl.BlockSpec(mcat.shape, lambda b, c, k: (0, 0)),
        ],
        out_specs=pl.BlockSpec((None, FFT_KB, 2 * FFT_N2, FFT_CT), lambda b, c, k: (b, k, 0, c)),
        out_shape=jax.ShapeDtypeStruct((B, kp, 2 * FFT_N2, C), F32),
        compiler_params=_cparams("parallel", "parallel", "parallel"),
        name="fft2",
    )(a, a, tw_r, tw_i, mcat)


def _ifft2_kernel(x_ref, k_ref, ssq_ref, tr_ref, ti_ref, m_ref, o_ref):
    m = m_ref[...]
    scale = lax.rsqrt(ssq_ref[...] + EPS)
    n2 = FFT_N2
    for r in range(FFT_KB):
        x = x_ref[r]
        kf = k_ref[r]
        xr, xi = x[:n2], x[n2:]
        kr, ki = kf[:n2] * scale, kf[n2:] * scale
        p = jnp.concatenate([xr * kr - xi * ki, xr * ki + xi * kr], axis=0)
        ph, plo = _split2(p)
        c = _dot(m, jnp.concatenate([ph, plo, ph], axis=0))
        cr, ci = c[:n2], c[n2:]
        tr, ti = (jnp.tile(t[r], (1, FFT_CT // LANES)) for t in (tr_ref, ti_ref))
        o_ref[0, r] = cr * tr + ci * ti
        o_ref[1, r] = ci * tr - cr * ti


def _ifft2(xs, ks, ssq, tw_r, tw_i, mcat_inv):
    B, kp, _, C = xs.shape
    nk = kp // FFT_KB
    return pl.pallas_call(
        _ifft2_kernel,
        grid=(B, C // FFT_CT, nk),
        in_specs=[
            pl.BlockSpec((None, FFT_KB, 2 * FFT_N2, FFT_CT), lambda b, c, k: (b, k, 0, c)),
            pl.BlockSpec((None, FFT_KB, 2 * FFT_N2, FFT_CT), lambda b, c, k: (0, k, 0, c)),
            pl.BlockSpec((1, FFT_CT), lambda b, c, k: (0, c)),
            pl.BlockSpec((FFT_KB, FFT_N2, LANES), lambda b, c, k: (k, 0, 0)),
            pl.BlockSpec((FFT_KB, FFT_N2, LANES), lambda b, c, k: (k, 0, 0)),
            pl.BlockSpec(mcat_inv.shape, lambda b, c, k: (0, 0)),
        ],
        out_specs=pl.BlockSpec((None, 2, FFT_KB, FFT_N2, FFT_CT), lambda b, c, k: (b, 0, k, 0, c)),
        out_shape=jax.ShapeDtypeStruct((B, 2, kp, FFT_N2, C), F32),
        compiler_params=_cparams("parallel", "parallel", "parallel"),
        name="ifft2",
    )(xs, ks, ssq, tw_r, tw_i, mcat_inv)


def _ifft1_kernel(d_ref, w_ref, u_ref, x2_ref, skip_ref, o_ref, ds_ref, ys_ref):
    rows, h = d_ref.shape[1], o_ref.shape[1]
    d2, u2, x22, o2 = (_flat_rows(r) for r in (d_ref, u_ref, x2_ref, o_ref))
    w = w_ref[...]
    for b in range(d_ref.shape[0]):
        _to_slabs(d2, b * rows * FFT_SUB, ds_ref)
        seq = slice(b * h * FFT_SUB, (b + 1) * h * FFT_SUB)
        for c in range(ds_ref.shape[0]):
            lanes = slice(c * LANES, (c + 1) * LANES)
            for p in range(0, FFT_SUB, 2):
                d = jnp.concatenate([ds_ref[c, pl.ds(p, rows, stride=FFT_SUB), :],
                                     ds_ref[c, pl.ds(p + 1, rows, stride=FFT_SUB), :]], axis=1)
                dh, dl = _split2(d)
                y = _dot(w, jnp.concatenate([dh, dl, dh], axis=0))
                ys_ref[c, pl.ds(p, h, stride=FFT_SUB), :] = y[:, :LANES]
                ys_ref[c, pl.ds(p + 1, h, stride=FFT_SUB), :] = y[:, LANES:]
            o2[seq, lanes] = x22[seq, lanes] * (ys_ref[c] + u2[seq, lanes] * skip_ref[:, lanes])


def _ifft1(d, wcat, u, x2c, skip, B):
    rows, C = d.shape[1], d.shape[-1]
    h = wcat.shape[0]
    nj = FFT_N2 // FFT_SUB
    bt = _outer_batch(B, h)
    seq = pl.BlockSpec((bt, h, None, FFT_SUB, C), lambda b, j: (b, 0, j, 0, 0))
    out = pl.pallas_call(
        _ifft1_kernel,
        grid=(B // bt, nj),
        in_specs=[
            pl.BlockSpec((bt, rows, None, FFT_SUB, C), lambda b, j: (b, 0, j, 0, 0)),
            pl.BlockSpec(wcat.shape, lambda b, j: (0, 0)),
            seq, seq,
            pl.BlockSpec((1, C), lambda b, j: (0, 0)),
        ],
        out_specs=seq,
        out_shape=jax.ShapeDtypeStruct((B, h, nj, FFT_SUB, C), F32),
        scratch_shapes=[pltpu.VMEM((C // LANES, rows * FFT_SUB, LANES), F32),
                        pltpu.VMEM((C // LANES, h * FFT_SUB, LANES), F32)],
        compiler_params=_cparams("parallel", "parallel"),
        name="ifft1",
    )(d.reshape(B, rows, nj, FFT_SUB, C), wcat, u.reshape(B, h, nj, FFT_SUB, C), x2c.reshape(B, h, nj, FFT_SUB, C),
      skip)
    return out.reshape(B * h * FFT_N2, C)


def _hilo_cat(m):
    hi = m.astype(BF16)
    lo = (m - hi.astype(F32)).astype(BF16)
    return jnp.concatenate([hi, hi, lo], axis=1)


def _fft_tables(L):
    n = 2 * L
    n2 = FFT_N2
    n1 = n // n2
    h = n1 // 2
    kp = h + FFT_KB
    k1 = jnp.arange(kp, dtype=jnp.int32)

    def outer(hin):
        nn = jnp.arange(hin, dtype=jnp.int32)
        ang = (2.0 * math.pi / n1) * ((k1[:, None] * nn[None, :]) % n1).astype(F32)
        return _hilo_cat(jnp.concatenate([jnp.cos(ang), -jnp.sin(ang)], axis=0))

    f_data = outer(h)
    f_filt = outer(n1)
    j = jnp.arange(n2, dtype=jnp.int32)
    ang2 = (2.0 * math.pi / n2) * ((j[:, None] * j[None, :]) % n2).astype(F32)
    c2, s2 = jnp.cos(ang2), jnp.sin(ang2)
    m_fwd = _hilo_cat(jnp.block([[c2, s2], [-s2, c2]]))
    m_inv = _hilo_cat(jnp.block([[c2, -s2], [s2, c2]]))
    angt = (2.0 * math.pi / n) * (k1[:, None] * j[None, :]).astype(F32)
    tw_r = jnp.broadcast_to(jnp.cos(angt)[:, :, None], (kp, n2, LANES))
    tw_i = jnp.broadcast_to(-jnp.sin(angt)[:, :, None], (kp, n2, LANES))
    wgt = jnp.where((k1 == 0) | (k1 == h), 1.0, jnp.where(k1 < h, 2.0, 0.0)).astype(F32) / n
    nn = jnp.arange(h, dtype=jnp.int32)
    angi = (2.0 * math.pi / n1) * ((nn[:, None] * k1[None, :]) % n1).astype(F32)
    w_inv = _hilo_cat(jnp.concatenate([jnp.cos(angi) * wgt, -jnp.sin(angi) * wgt], axis=1))
    return dict(n1=n1, h=h, kp=kp, f_data=f_data, f_filt=f_filt, m_fwd=m_fwd, m_inv=m_inv,
                tw_r=tw_r, tw_i=tw_i, w_inv=w_inv)


def _filter_feats(L):
    i = jnp.arange(2 * L, dtype=jnp.int32)
    n = jnp.where(i < L, i, 2 * L - i).astype(F32)
    t = (n / (L - 1))[:, None]
    w = 2.0 * math.pi * n / L
    fb = jnp.linspace(1e-4, HY_BANDS - 1, HY_BANDS, dtype=F32)
    ph = w[:, None] * fb
    valid = jnp.where(i == L, 0.0, 1.0).astype(F32)[:, None]
    ext = jnp.concatenate([t, jnp.cos(ph), -jnp.sin(ph), valid], axis=-1)
    return jnp.pad(ext, ((0, 0), (0, LANES - HY_EMB - 1)))


def _hyena_spectrum(tabs, feats_ext, lp, L):
    kf, ssq = _hy_filt(feats_ext, lp["hy_w1p"], lp["hy_b1"], lp["hy_f1"], lp["hy_w2"], lp["hy_b2"], lp["hy_f2"],
                       lp["hy_w3s"], lp["hy_deltas"], L)
    a = _fft1(kf, tabs["f_filt"], 1, tabs["n1"])
    ks = _fft2(a, tabs["tw_r"], tabs["tw_i"], tabs["m_fwd"], tabs["kp"])
    return ks, ssq


def _hyena(z, tabs, ks, ssq, lp, B, L):
    u, x2c = _hy_pre(z, lp["hy_conv_w"], lp["hy_conv_b"], L)
    kp = tabs["kp"]
    a = _fft1(u, tabs["f_data"], B, tabs["h"])
    xs = _fft2(a, tabs["tw_r"], tabs["tw_i"], tabs["m_fwd"], kp)
    d = _ifft2(xs, ks, ssq, tabs["tw_r"], tabs["tw_i"], tabs["m_inv"])
    return _ifft1(d.reshape(B, 2 * kp, FFT_N2, HY_W), tabs["w_inv"], u, x2c, lp["hy_skip"], B)


def _gla_kernel(qf_ref, kf_ref, vf_ref, lf_ref, qb_ref, kb_ref, vb_ref, lb_ref, gup_ref, gb_ref,
                of_ref, ob_ref, sf_ref, sb_ref, *, nchunk):
    C = GLA_CHUNK
    R = nchunk * C

    @pl.when(pl.program_id(1) == 0)
    def _():
        sf_ref[...] = jnp.zeros_like(sf_ref)
        sb_ref[...] = jnp.zeros_like(sb_ref)

    rr = lax.broadcasted_iota(jnp.int32, (R, R), 0)
    cc = lax.broadcasted_iota(jnp.int32, (R, R), 1)
    same_chunk = (rr // C) == (cc // C)
    lane = lax.broadcasted_iota(jnp.int32, (C, GLA_K), 1) // GLA_DK
    ri4 = lax.broadcasted_iota(jnp.int32, (GLA_HEADS * C, C), 0) % C
    ci4 = lax.broadcasted_iota(jnp.int32, (GLA_HEADS * C, C), 1)

    def stack_heads(x):
        return jnp.concatenate([jnp.where(lane == h, x, 0.0) for h in range(GLA_HEADS)], axis=0).astype(BF16)

    def per_chunk_rows(b, row):
        return jnp.concatenate([jnp.broadcast_to(b[c * C + row:c * C + row + 1], (C, GLA_K)) for c in range(nchunk)],
                               axis=0)

    dirs = []
    for d, (q_ref, k_ref, v_ref, l_ref) in enumerate(((qf_ref, kf_ref, vf_ref, lf_ref),
                                                       (qb_ref, kb_ref, vb_ref, lb_ref))):
        logit = _dot3(l_ref[...], gup_ref[:, d * GLA_K:(d + 1) * GLA_K]) + gb_ref[:, d * GLA_K:(d + 1) * GLA_K]
        la = (jnp.minimum(logit, 0.0) - jnp.log(1.0 + jnp.exp(-jnp.abs(logit)))) * (1.0 / GLA_TAU)
        if d == 0:
            tri = jnp.where(same_chunk & (rr >= cc), 1.0, 0.0).astype(BF16)
            keep = ri4 >= ci4
            mid, last = C // 2 - 1, C - 1
            order = list(range(nchunk))
        else:
            tri = jnp.where(same_chunk & (cc >= rr), 1.0, 0.0).astype(BF16)
            keep = ci4 > ri4
            mid, last = C // 2, 0
            order = list(range(nchunk - 1, -1, -1))
        hi, lo = _split2(la)
        lo2 = (la - hi.astype(F32) - lo.astype(F32)).astype(BF16)
        b = _dot(tri, hi) + _dot(tri, lo) + _dot(tri, lo2)
        b_mid = per_chunk_rows(b, mid)
        b_last = per_chunk_rows(b, last)
        q = q_ref[...] * (GLA_DK ** -0.5)
        k = k_ref[...]
        dirs.append(dict(
            keep=keep, order=order, v=v_ref[...].astype(BF16),
            qt=q * jnp.exp(b - b_mid), kt=(k * jnp.exp(b_mid - b)).astype(BF16),
            kh=k * jnp.exp(b_last - b), qh=q * jnp.exp(b), dec=jnp.exp(b_last)))

    for dd in dirs:
        dd["a"] = []
        for c in range(nchunk):
            rs = slice(c * C, (c + 1) * C)
            a = lax.dot_general(stack_heads(dd["qt"][rs]), dd["kt"][rs], (((1,), (1,)), ((), ())),
                                preferred_element_type=F32)
            dd["a"].append(jnp.where(dd["keep"], a, 0.0).astype(BF16))

    for dd in dirs:
        dd["o"], dd["u"] = [], []
        for c in range(nchunk):
            rs = slice(c * C, (c + 1) * C)
            k_t = jnp.transpose(dd["kh"][rs]).astype(BF16)
            outs, ups = [], []
            for h in range(GLA_HEADS):
                vh = dd["v"][rs, h * GLA_DV:(h + 1) * GLA_DV]
                outs.append(_dot(dd["a"][c][h * C:(h + 1) * C], vh))
                ups.append(_dot(k_t[h * GLA_DK:(h + 1) * GLA_DK], vh))
            dd["o"].append(outs)
            dd["u"].append(jnp.concatenate(ups, axis=0))

    for dd, s_ref in zip(dirs, (sf_ref, sb_ref)):
        s = s_ref[...]
        dd["s_prev"] = {}
        for c in dd["order"]:
            dd["s_prev"][c] = s.astype(BF16)
            decay = jnp.transpose(jnp.broadcast_to(dd["dec"][c * C:c * C + 1], (GLA_DV, GLA_K)))
            s = decay * s + dd["u"][c]
        s_ref[...] = s

    for dd, o_ref in zip(dirs, (of_ref, ob_ref)):
        for c in range(nchunk):
            rs = slice(c * C, (c + 1) * C)
            o_inter = _dot(stack_heads(dd["qh"][rs]), dd["s_prev"][c])
            o_ref[rs, :] = jnp.concatenate(
                [dd["o"][c][h] + o_inter[h * C:(h + 1) * C] for h in range(GLA_HEADS)], axis=1)


def _gla(z, gup, gb, B, L, nchunk=4):
    n = z.shape[0]
    R = nchunk * GLA_CHUNK
    T = L // R
    fwd = lambda b, t: b * T + t
    bwd = lambda b, t: b * T + (T - 1 - t)

    def specs(row):
        return [
            pl.BlockSpec((R, GLA_K), lambda b, t: (row(b, t), Z_GQ // GLA_K)),
            pl.BlockSpec((R, GLA_K), lambda b, t: (row(b, t), Z_GK // GLA_K)),
            pl.BlockSpec((R, GLA_V), lambda b, t: (row(b, t), Z_GV // GLA_V)),
            pl.BlockSpec((R, LANES), lambda b, t: (row(b, t), Z_GLOW // LANES)),
        ]

    return pl.pallas_call(
        functools.partial(_gla_kernel, nchunk=nchunk),
        grid=(B, T),
        in_specs=specs(fwd) + specs(bwd) + [
            pl.BlockSpec(gup.shape, lambda b, t: (0, 0)),
            pl.BlockSpec(gb.shape, lambda b, t: (0, 0)),
        ],
        out_specs=[
            pl.BlockSpec((R, GLA_V), lambda b, t: (fwd(b, t), 0)),
            pl.BlockSpec((R, GLA_V), lambda b, t: (bwd(b, t), 0)),
        ],
        out_shape=[jax.ShapeDtypeStruct((n, GLA_V), F32)] * 2,
        scratch_shapes=[pltpu.VMEM((GLA_K, GLA_DV), F32)] * 2,
        compiler_params=_cparams("parallel", "arbitrary"),
        name="gla",
    )(*([z] * 8), gup, gb)


def _mix_out_kernel(x_ref, ya_ref, yh_ref, of_ref, ob_ref, og_ref, g0_ref, g1_ref, g2_ref,
                    wb_ref, wo_ref, gn_ref, o_ref):
    o = of_ref[...] + ob_ref[...]
    og = og_ref[...]
    gn = gn_ref[...]
    parts = []
    for h in range(GLA_HEADS):
        blk = o[:, h * GLA_DV:(h + 1) * GLA_DV]
        ms = jnp.mean(blk * blk, axis=-1, keepdims=True)
        gate = og[:, h * GLA_DV:(h + 1) * GLA_DV]
        parts.append(blk * lax.rsqrt(ms + EPS) * gn * (gate * jax.nn.sigmoid(gate)))
    y_gla = jnp.concatenate(parts, axis=1)
    proj_att = lax.dot_general(ya_ref[...], wb_ref[0], (((0,), (0,)), ((), ())),
                               preferred_element_type=F32)
    merged = jax.nn.sigmoid(g0_ref[...].astype(F32)) * proj_att
    merged += jax.nn.sigmoid(g1_ref[...].astype(F32)) * _dot(yh_ref[...].astype(BF16), wb_ref[1])
    merged += jax.nn.sigmoid(g2_ref[...].astype(F32)) * _dot(y_gla.astype(BF16), wb_ref[2])
    o_ref[...] = x_ref[...] + _dot(merged.astype(BF16), wo_ref[...])


def _mix_out(x, y_att, y_hy, o_f, o_b, z, zg, wb, wo, gn, tm=256):
    n = x.shape[0]
    row = lambda w: pl.BlockSpec((tm, w), lambda i: (i, 0))
    return pl.pallas_call(
        _mix_out_kernel,
        grid=(n // tm,),
        in_specs=[
            row(D_MODEL), pl.BlockSpec((MIX_W, tm), lambda i: (0, i)), row(MIX_W), row(MIX_W), row(MIX_W),
            pl.BlockSpec((tm, GLA_V), lambda i: (i, Z_GOG // GLA_V)),
            pl.BlockSpec((tm, D_MODEL), lambda i: (i, 0)),
            pl.BlockSpec((tm, D_MODEL), lambda i: (i, 1)),
            pl.BlockSpec((tm, D_MODEL), lambda i: (i, 2)),
            pl.BlockSpec(wb.shape, lambda i: (0, 0, 0)),
            pl.BlockSpec(wo.shape, lambda i: (0, 0)),
            pl.BlockSpec((1, GLA_DV), lambda i: (0, 0)),
        ],
        out_specs=row(D_MODEL),
        out_shape=jax.ShapeDtypeStruct((n, D_MODEL), F32),
        compiler_params=_cparams("parallel"),
        name="mix_out",
    )(x, y_att, y_hy, o_f, o_b, z, zg, zg, zg, wb, wo, gn)


def _ffn_kernel(x_ref, g_ref, wg_ref, wu_ref, wd_ref, o_ref, *, tf):
    x = x_ref[...]
    ms = jnp.mean(x * x, axis=-1, keepdims=True)
    h = (x * lax.rsqrt(ms + EPS) * g_ref[...]).astype(BF16)
    acc = x
    for c in range(0, D_FF, tf):
        a = _dot(h, wg_ref[:, c:c + tf])
        act = (a * jax.nn.sigmoid(a)) * _dot(h, wu_ref[:, c:c + tf])
        acc = acc + _dot(act.astype(BF16), wd_ref[c:c + tf, :])
    o_ref[...] = acc


def _ffn(x, g, wg, wu, wd, tm=512, tf=1408):
    n = x.shape[0]
    resident = lambda w: pl.BlockSpec(w.shape, lambda i: (0, 0), pipeline_mode=pl.Buffered(1))
    return pl.pallas_call(
        functools.partial(_ffn_kernel, tf=tf),
        grid=(n // tm,),
        in_specs=[
            pl.BlockSpec((tm, D_MODEL), lambda i: (i, 0)),
            pl.BlockSpec((1, D_MODEL), lambda i: (0, 0)),
            resident(wg), resident(wu), resident(wd),
        ],
        out_specs=pl.BlockSpec((tm, D_MODEL), lambda i: (i, 0)),
        out_shape=jax.ShapeDtypeStruct((n, D_MODEL), F32),
        compiler_params=_cparams("parallel"),
        name="ffn",
    )(x, g, wg, wu, wd)


def _final_norm_kernel(x_ref, g_ref, o_ref):
    x = x_ref[...]
    ms = jnp.mean(x * x, axis=-1, keepdims=True)
    o_ref[...] = x * lax.rsqrt(ms + EPS) * g_ref[...]


def _final_norm(x, g, tm=512):
    n = x.shape[0]
    return pl.pallas_call(
        _final_norm_kernel,
        grid=(n // tm,),
        in_specs=[pl.BlockSpec((tm, D_MODEL), lambda i: (i, 0)), pl.BlockSpec((1, D_MODEL), lambda i: (0, 0))],
        out_specs=pl.BlockSpec((tm, D_MODEL), lambda i: (i, 0)),
        out_shape=jax.ShapeDtypeStruct((n, D_MODEL), F32),
        compiler_params=_cparams("parallel"),
        name="final_norm",
    )(x, g)


def _rope_tables(L):
    rows = L // GRID_W
    r = jnp.repeat(jnp.arange(rows, dtype=F32), GRID_W)
    c = jnp.tile(jnp.arange(GRID_W, dtype=F32), rows)
    inv = ROPE_THETA ** (-jnp.arange(0, ROPE_AXIS_DIM, 2, dtype=F32) / ROPE_AXIS_DIM)
    ang = jnp.concatenate([r[:, None] * inv, c[:, None] * inv], axis=-1)
    cos, sin = jnp.cos(ang), jnp.sin(ang)
    cos_t = jnp.tile(cos, (1, 2 * LANES // HEAD_DIM))
    sin_t = jnp.tile(jnp.concatenate([-sin, sin], axis=-1), (1, LANES // HEAD_DIM))
    return cos_t, sin_t


def _layer_params(l, norm_mix_g, w_in, q_norm_g, k_norm_g, hy_conv_w, hy_conv_b, hy_w1, hy_b1, hy_f1, hy_w2,
                  hy_b2, hy_f2, hy_w3, hy_skip, gla_gate_up, gla_gate_b, gla_norm_g, w_branch, w_out, norm_ffn_g,
                  w_ffn_gate, w_ffn_up, w_ffn_down):
    w = w_in[l]
    offs = {}
    off = 0
    for name, size in (("aq", ATT_W), ("ak", KV_W), ("av", KV_W), ("hz", 3 * HY_W), ("gq", GLA_K), ("gk", GLA_K),
                       ("gv", GLA_V), ("gog", GLA_V), ("glow", 2 * GLA_RANK), ("gates", N_BRANCH * D_MODEL)):
        offs[name] = w[:, off:off + size]
        off += size
    w_cat = jnp.concatenate(
        [offs["aq"], offs["gv"], offs["gog"], offs["hz"], offs["gq"], offs["gk"], offs["ak"],
         offs["av"], offs["glow"], jnp.zeros((D_MODEL, Z_COLS - Z_GLOW - 2 * GLA_RANK), w.dtype)], axis=1).astype(BF16)
    gup = jnp.zeros((LANES, 2 * GLA_K), F32)
    gup = gup.at[0:GLA_RANK, 0:GLA_K].set(gla_gate_up[l, 0])
    gup = gup.at[GLA_RANK:2 * GLA_RANK, GLA_K:2 * GLA_K].set(gla_gate_up[l, 1])
    deltas = jnp.linspace(abs(math.log(HY_TARGET) / HY_SLOW), abs(math.log(HY_TARGET) / HY_FAST), HY_W, dtype=F32)
    return dict(
        norm_mix_g=norm_mix_g[l][None, :],
        w_cat=w_cat, w_gates=offs["gates"].astype(BF16),
        gq=jnp.tile(q_norm_g[l], LANES // HEAD_DIM)[None, :],
        gk=jnp.tile(k_norm_g[l], LANES // HEAD_DIM)[None, :],
        hy_conv_w=jnp.transpose(hy_conv_w[l].reshape(3, 3, HY_W), (1, 0, 2)),
        hy_conv_b=hy_conv_b[l].reshape(3, HY_W),
        hy_w1p=jnp.kron(jnp.eye(2, dtype=F32), jnp.pad(hy_w1[l], ((0, LANES - HY_EMB), (0, 0)))),
        hy_b1=jnp.tile(hy_b1[l], 2)[None, :], hy_f1=jnp.tile(hy_f1[l], 2)[None, :],
        hy_w2=jnp.kron(jnp.eye(2, dtype=F32), hy_w2[l]),
        hy_b2=jnp.tile(hy_b2[l], 2)[None, :], hy_f2=jnp.tile(hy_f2[l], 2)[None, :],
        hy_w3s=jnp.einsum("gh,kfc->fghkc", jnp.eye(2, dtype=F32), hy_w3[l].reshape(HY_FILTER_W, 2, HY_W)
                          ).reshape(2, 2, 2 * HY_FILTER_W, HY_W),
        hy_deltas=deltas[None, :],
        hy_skip=hy_skip[l][None, :],
        gup=gup, gb=gla_gate_b[l].reshape(1, 2 * GLA_K),
        gn=gla_norm_g[l][None, :],
        wb=w_branch[l].astype(BF16), wo=w_out[l].astype(BF16),
        norm_ffn_g=norm_ffn_g[l][None, :],
        wg=w_ffn_gate[l].astype(BF16), wu=w_ffn_up[l].astype(BF16), wd=w_ffn_down[l].astype(BF16),
    )


def _encoder_layer(x, lp, B, L, rope, tabs, ks, ssq):
    z, zg = _in_proj(x, lp["norm_mix_g"], lp["w_cat"], lp["w_gates"])
    qt, kr, vt = _attn_prep(z, rope[0], rope[1], lp["gq"], lp["gk"], L)
    y_att = _flash(qt, kr, vt, B, L)
    y_hy = _hyena(z, tabs, ks, ssq, lp, B, L)
    o_f, o_b = _gla(z, lp["gup"], lp["gb"], B, L)
    x = _mix_out(x, y_att, y_hy, o_f, o_b, z, zg, lp["wb"], lp["wo"], lp["gn"])
    return _ffn(x, lp["norm_ffn_g"], lp["wg"], lp["wu"], lp["wd"])


def kernel(x_prompt, x_sample, norm_mix_g, w_in, q_norm_g, k_norm_g, hy_conv_w, hy_conv_b, hy_w1, hy_b1, hy_f1, hy_w2, hy_b2, hy_f2, hy_w3, hy_skip, gla_gate_up, gla_gate_b, gla_norm_g, w_branch, w_out, norm_ffn_g, w_ffn_gate, w_ffn_up, w_ffn_down, final_norm_g):
    streams = []
    for xin in (x_prompt, x_sample):
        B, L, _ = xin.shape
        streams.append(dict(x=xin.reshape(B * L, D_MODEL), B=B, L=L, rope=_rope_tables(L), tabs=_fft_tables(L),
                            feats=_filter_feats(L)))
    w_in, w_branch, w_out, w_ffn_gate, w_ffn_up, w_ffn_down = (
        w.astype(BF16) for w in (w_in, w_branch, w_out, w_ffn_gate, w_ffn_up, w_ffn_down))
    for l in range(DEPTH):
        lp = _layer_params(l, norm_mix_g, w_in, q_norm_g, k_norm_g, hy_conv_w, hy_conv_b, hy_w1, hy_b1, hy_f1,
                           hy_w2, hy_b2, hy_f2, hy_w3, hy_skip, gla_gate_up, gla_gate_b, gla_norm_g, w_branch,
                           w_out, norm_ffn_g, w_ffn_gate, w_ffn_up, w_ffn_down)
        for s in streams:
            ks, ssq = _hyena_spectrum(s["tabs"], s["feats"], lp, s["L"])
            s["x"] = _encoder_layer(s["x"], lp, s["B"], s["L"], s["rope"], s["tabs"], ks, ssq)
    outs = []
    for s, xin in zip(streams, (x_prompt, x_sample)):
        outs.append(_final_norm(s["x"], final_norm_g[None, :]).reshape(xin.shape))
    return tuple(outs)
```

```python
import functools
import math

import jax
import jax.numpy as jnp
from jax import lax
from jax.experimental import pallas as pl
from jax.experimental.pallas import tpu as pltpu

F32 = jnp.float32
BF16 = jnp.bfloat16

D_MODEL = 1024
DEPTH = 4
GRID_W = 64
N_HEADS = 8
N_KV_HEADS = 2
HEAD_DIM = 64
GROUP = N_HEADS // N_KV_HEADS
ROPE_THETA = 10000.0
ROPE_AXIS_DIM = HEAD_DIM // 2
ATT_W = N_HEADS * HEAD_DIM
KV_W = N_KV_HEADS * HEAD_DIM
HY_W = D_MODEL // 2
HY_EMB = 33
HY_BANDS = (HY_EMB - 1) // 2
HY_FILTER_W = 64
HY_TARGET = 1e-2
HY_FAST = 0.3
HY_SLOW = 1.5
GLA_HEADS = 4
GLA_DK = 64
GLA_DV = 128
GLA_K = GLA_HEADS * GLA_DK
GLA_V = GLA_HEADS * GLA_DV
GLA_RANK = 16
GLA_TAU = 16.0
GLA_CHUNK = 64
N_BRANCH = 3
MIX_W = 512
D_FF = 2816
EPS = 1e-6

LANES = 128
VMEM_LIMIT = 56 * 1024 * 1024

Z_AQ = 0
Z_GV = 512
Z_GOG = 1024
Z_HZ = 1536
Z_GQ = 3072
Z_GK = 3328
Z_AK = 3584
Z_AV = 3712
Z_GLOW = 3840
Z_COLS = 4096

FFT_N2 = 128
FFT_KB = 8
FFT_SUB = 8
FFT_CT = 512


def _cparams(*sem):
    return pltpu.CompilerParams(dimension_semantics=sem, vmem_limit_bytes=VMEM_LIMIT)


def _split2(x):
    hi = x.astype(BF16)
    lo = (x - hi.astype(F32)).astype(BF16)
    return hi, lo


def _dot(a, b):
    return jnp.dot(a, b, preferred_element_type=F32)


def _dot3(a, b):
    ah, al = _split2(a)
    bh, bl = _split2(b)
    return _dot(ah, bh) + _dot(al, bh) + _dot(ah, bl)


def _in_proj_kernel(x_ref, g_ref, w_ref, wg_ref, z_ref, zg_ref, *, tn):
    x = x_ref[...]
    ms = jnp.mean(x * x, axis=-1, keepdims=True)
    h = (x * lax.rsqrt(ms + EPS) * g_ref[...]).astype(BF16)
    for c in range(0, z_ref.shape[1], tn):
        z_ref[:, c:c + tn] = _dot(h, w_ref[:, c:c + tn])
    for c in range(0, zg_ref.shape[1], tn):
        zg_ref[:, c:c + tn] = _dot(h, wg_ref[:, c:c + tn]).astype(zg_ref.dtype)


def _in_proj(x, g, w, w_gates, tm=512, tn=1024):
    n = x.shape[0]
    resident = lambda a: pl.BlockSpec(a.shape, lambda i: (0, 0), pipeline_mode=pl.Buffered(1))
    row = lambda a: pl.BlockSpec((tm, a.shape[1]), lambda i: (i, 0))
    return pl.pallas_call(
        functools.partial(_in_proj_kernel, tn=tn),
        grid=(n // tm,),
        in_specs=[row(x), pl.BlockSpec((1, D_MODEL), lambda i: (0, 0)), resident(w), resident(w_gates)],
        out_specs=[row(w), row(w_gates)],
        out_shape=[jax.ShapeDtypeStruct((n, w.shape[1]), F32), jax.ShapeDtypeStruct((n, w_gates.shape[1]), BF16)],
        compiler_params=_cparams("parallel"),
        name="in_proj",
    )(x, g, w, w_gates)


V_ROWS = HEAD_DIM + 16
Q_SCALE = (HEAD_DIM ** -0.5) * math.log2(math.e)


def _attn_prep_kernel(q_ref, k_ref, v_ref, cos_ref, sin_ref, gq_ref, gk_ref, qt_ref, kr_ref, vt_ref):
    tm = q_ref.shape[0]
    cos = cos_ref[...]
    sin = sin_ref[...]
    lane = lax.broadcasted_iota(jnp.int32, (tm, LANES), 1)
    first_half = (lane % HEAD_DIM) < (HEAD_DIM // 2)
    left = lane < HEAD_DIM
    r = lax.broadcasted_iota(jnp.int32, (LANES, LANES), 0) // HEAD_DIM
    c = lax.broadcasted_iota(jnp.int32, (LANES, LANES), 1) // HEAD_DIM
    head_ones = jnp.where(r == c, 1.0, 0.0).astype(BF16)

    def norm_rope(x, g):
        xsq = x * x
        hi, lo = _split2(xsq)
        lo2 = (xsq - hi.astype(F32) - lo.astype(F32)).astype(BF16)
        ssq = _dot(hi, head_ones) + _dot(lo, head_ones) + _dot(lo2, head_ones)
        y = x * lax.rsqrt(ssq * (1.0 / HEAD_DIM) + EPS) * g
        other = jnp.where(first_half, pltpu.roll(y, LANES - HEAD_DIM // 2, 1), pltpu.roll(y, HEAD_DIM // 2, 1))
        return y * cos + other * sin

    gq = gq_ref[...]
    for jb in range(ATT_W // LANES):
        y = norm_rope(q_ref[:, jb * LANES:(jb + 1) * LANES], gq) * Q_SCALE
        ys = pltpu.roll(y, HEAD_DIM, 1)
        if (2 * jb) // GROUP == 0:
            e0 = jnp.where(left, y, 0.0)
            e1 = jnp.where(left, ys, 0.0)
        else:
            e0 = jnp.where(left, 0.0, ys)
            e1 = jnp.where(left, 0.0, y)
        qt_ref[2 * jb] = jnp.transpose(e0).astype(BF16)
        qt_ref[2 * jb + 1] = jnp.transpose(e1).astype(BF16)
    kr_ref[...] = norm_rope(k_ref[...], gk_ref[...]).astype(BF16)
    vt = jnp.transpose(v_ref[...])
    ones = jnp.ones((V_ROWS - HEAD_DIM, tm), BF16)
    for kv in range(N_KV_HEADS):
        vt_ref[kv, 0:HEAD_DIM, :] = vt[kv * HEAD_DIM:(kv + 1) * HEAD_DIM].astype(BF16)
        vt_ref[kv, HEAD_DIM:V_ROWS, :] = ones


def _attn_prep(z, cos_t, sin_t, gq, gk, L, tm=512):
    n = z.shape[0]
    lt = L // tm
    return pl.pallas_call(
        _attn_prep_kernel,
        grid=(n // tm,),
        in_specs=[
            pl.BlockSpec((tm, ATT_W), lambda i: (i, Z_AQ // ATT_W)),
            pl.BlockSpec((tm, KV_W), lambda i: (i, Z_AK // KV_W)),
            pl.BlockSpec((tm, KV_W), lambda i: (i, Z_AV // KV_W)),
            pl.BlockSpec((tm, LANES), lambda i: (i % lt, 0)),
            pl.BlockSpec((tm, LANES), lambda i: (i % lt, 0)),
            pl.BlockSpec((1, LANES), lambda i: (0, 0)),
            pl.BlockSpec((1, LANES), lambda i: (0, 0)),
        ],
        out_specs=[
            pl.BlockSpec((N_HEADS, LANES, tm), lambda i: (0, 0, i)),
            pl.BlockSpec((tm, KV_W), lambda i: (i, 0)),
            pl.BlockSpec((N_KV_HEADS, V_ROWS, tm), lambda i: (0, 0, i)),
        ],
        out_shape=[
            jax.ShapeDtypeStruct((N_HEADS, LANES, n), BF16),
            jax.ShapeDtypeStruct((n, KV_W), BF16),
            jax.ShapeDtypeStruct((N_KV_HEADS, V_ROWS, n), BF16),
        ],
        compiler_params=_cparams("parallel"),
        name="attn_prep",
    )(z, z, z, cos_t, sin_t, gq, gk)


def _flash_kernel(qt_ref, k_ref, vt_ref, o_ref, s_ref, p_ref, *, tk):
    tq = qt_ref.shape[2]
    L = k_ref.shape[0]
    nt = L // tk
    heads = range(N_HEADS)

    def scores(j, slot):
        kj = k_ref[pl.ds(pl.multiple_of(j * tk, tk), tk), :]
        for h in heads:
            s_ref[slot, h] = _dot(kj, qt_ref[h])

    def values(j, slot, alphas, accs):
        start = pl.multiple_of(j * tk, tk)
        vjs = [vt_ref[kv, :, pl.ds(start, tk)] for kv in range(N_KV_HEADS)]
        return [alphas[h] * accs[h] + _dot(vjs[h // GROUP], p_ref[slot, h]) for h in heads]

    scores(0, 0)
    p_ref[1] = jnp.zeros(p_ref.shape[1:], BF16)

    def half_step(jv, js, other, cur, ms, alphas, accs):
        kj = k_ref[pl.ds(pl.multiple_of(js * tk, tk), tk), :]
        start = pl.multiple_of(jv * tk, tk)
        vjs = [vt_ref[kv, :, pl.ds(start, tk)] for kv in range(N_KV_HEADS)]
        ms, alphas, accs = list(ms), list(alphas), list(accs)
        for h in heads:
            accs[h] = alphas[h] * accs[h] + _dot(vjs[h // GROUP], p_ref[other, h])
            s = s_ref[cur, h]
            m_new = jnp.maximum(ms[h], jnp.max(s, axis=0, keepdims=True))
            alphas[h] = jnp.exp2(ms[h] - m_new)
            ms[h] = m_new
            p_ref[cur, h] = jnp.exp2(s - m_new).astype(BF16)
            s_ref[other, h] = _dot(kj, qt_ref[h])
        return ms, alphas, accs

    def body(jj, carry):
        ms, alphas, accs = carry
        j = 2 * jj
        ms, alphas, accs = half_step(jnp.maximum(j - 1, 0), j + 1, 1, 0, ms, alphas, accs)
        return half_step(j, jnp.minimum(j + 2, nt - 1), 0, 1, ms, alphas, accs)

    init = ([jnp.full((1, tq), -jnp.inf, F32)] * N_HEADS, [jnp.ones((1, tq), F32)] * N_HEADS,
            [jnp.zeros((V_ROWS, tq), F32)] * N_HEADS)
    _, alphas, accs = lax.fori_loop(0, nt // 2, body, init)
    accs = values(nt - 1, 1, alphas, accs)
    for h in heads:
        o_ref[h * HEAD_DIM:(h + 1) * HEAD_DIM, :] = (accs[h][0:HEAD_DIM] / accs[h][HEAD_DIM:HEAD_DIM + 1]
                                                     ).astype(o_ref.dtype)


def _flash(qt, kr, vt, B, L, tq=256, tk=512):
    n = kr.shape[0]
    lt = L // tq
    assert (L // tk) % 2 == 0
    return pl.pallas_call(
        functools.partial(_flash_kernel, tk=tk),
        grid=(B, lt),
        in_specs=[
            pl.BlockSpec((N_HEADS, LANES, tq), lambda b, i: (0, 0, b * lt + i)),
            pl.BlockSpec((L, KV_W), lambda b, i: (b, 0)),
            pl.BlockSpec((N_KV_HEADS, V_ROWS, L), lambda b, i: (0, 0, b)),
        ],
        out_specs=pl.BlockSpec((ATT_W, tq), lambda b, i: (0, b * lt + i)),
        out_shape=jax.ShapeDtypeStruct((ATT_W, n), BF16),
        scratch_shapes=[pltpu.VMEM((2, N_HEADS, tk, tq), F32), pltpu.VMEM((2, N_HEADS, tk, tq), BF16)],
        compiler_params=_cparams("parallel", "parallel"),
        name="flash",
    )(qt, kr, vt)


def _hy_pre_kernel(v_ref, vp_ref, vn_ref, a_ref, ap_ref, an_ref, b_ref, bp_ref, bn_ref, w_ref, cb_ref,
                   u_ref, x2_ref, *, lt):
    tm = v_ref.shape[0]
    i = pl.program_id(0)
    first = (i % lt) == 0
    last = (i % lt) == lt - 1
    rows = lax.broadcasted_iota(jnp.int32, v_ref.shape, 0)

    def conv3(m_ref, p_ref, n_ref, part):
        x = m_ref[...]
        prev = jnp.where(first, 0.0, p_ref[7:8, :])
        nxt = jnp.where(last, 0.0, n_ref[0:1, :])
        dn = jnp.where(rows == 0, prev, pltpu.roll(x, 1, 0))
        up = jnp.where(rows == tm - 1, nxt, pltpu.roll(x, tm - 1, 0))
        w = w_ref[part]
        return dn * w[0:1] + x * w[1:2] + up * w[2:3] + cb_ref[part:part + 1, :]

    u_ref[...] = conv3(v_ref, vp_ref, vn_ref, 0) * conv3(a_ref, ap_ref, an_ref, 1)
    x2_ref[...] = conv3(b_ref, bp_ref, bn_ref, 2)


def _hy_pre(z, conv_w, conv_b, L, tm=1024, tc=512):
    n = z.shape[0]
    lt = L // tm
    r8 = tm // 8
    nb8 = n // 8
    specs = []
    for part in range(3):
        c0 = (Z_HZ + part * HY_W) // tc
        specs.append(pl.BlockSpec((tm, tc), lambda i, c, c0=c0: (i, c0 + c)))
        specs.append(pl.BlockSpec((8, tc), lambda i, c, c0=c0: (jnp.maximum(i * r8 - 1, 0), c0 + c)))
        specs.append(pl.BlockSpec((8, tc), lambda i, c, c0=c0: (jnp.minimum((i + 1) * r8, nb8 - 1), c0 + c)))
    specs.append(pl.BlockSpec((3, 3, tc), lambda i, c: (0, 0, c)))
    specs.append(pl.BlockSpec((3, tc), lambda i, c: (0, c)))
    return pl.pallas_call(
        functools.partial(_hy_pre_kernel, lt=lt),
        grid=(n // tm, HY_W // tc),
        in_specs=specs,
        out_specs=[pl.BlockSpec((tm, tc), lambda i, c: (i, c))] * 2,
        out_shape=[jax.ShapeDtypeStruct((n, HY_W), F32)] * 2,
        compiler_params=_cparams("parallel", "parallel"),
        name="hy_pre",
    )(*([z] * 9), conv_w, conv_b)


def _hy_filt_kernel(f_ref, w1_ref, b1_ref, f1_ref, w2_ref, b2_ref, f2_ref, w3_ref, dl_ref, kf_ref, ssq_ref):
    feats = f_ref[...]
    half = feats.shape[0] // 2
    halves = (feats[:half], feats[half:])
    h = jnp.sin(f1_ref[...] * (_dot3(jnp.concatenate(halves, axis=1), w1_ref[...]) + b1_ref[...]))
    h = jnp.sin(f2_ref[...] * (_dot3(h, w2_ref[...]) + b2_ref[...]))

    @pl.when(pl.program_id(0) == 0)
    def _():
        ssq_ref[...] = jnp.zeros_like(ssq_ref)

    for i, f in enumerate(halves):
        t = f[:, 0:1]
        valid = f[:, HY_EMB:HY_EMB + 1]
        kf = _dot3(h, w3_ref[i]) * jnp.exp(-t * dl_ref[...]) * valid
        kf_ref[i * half:(i + 1) * half, :] = kf
        ssq_ref[...] += jnp.sum(kf * kf, axis=0, keepdims=True)


def _hy_filt(feats_ext, w1p, b1, f1, w2, b2, f2, w3s, deltas, L, tr=512):
    lt = L // tr
    full = lambda a: pl.BlockSpec(a.shape, lambda i: (0,) * a.ndim)
    return pl.pallas_call(
        _hy_filt_kernel,
        grid=(2 * lt,),
        in_specs=[
            pl.BlockSpec((tr, LANES), lambda i: (i, 0)),
            full(w1p), full(b1), full(f1), full(w2), full(b2), full(f2),
            pl.BlockSpec((None, 2, 2 * HY_FILTER_W, HY_W), lambda i: (i // lt, 0, 0, 0)),
            full(deltas),
        ],
        out_specs=[pl.BlockSpec((tr, HY_W), lambda i: (i, 0)), pl.BlockSpec((1, HY_W), lambda i: (0, 0))],
        out_shape=[jax.ShapeDtypeStruct((2 * L, HY_W), F32), jax.ShapeDtypeStruct((1, HY_W), F32)],
        compiler_params=_cparams("arbitrary"),
        name="hy_filt",
    )(feats_ext, w1p, b1, f1, w2, b2, f2, w3s, deltas)


def _flat_rows(ref):
    return ref.reshape(ref.shape[0] * ref.shape[1] * FFT_SUB, ref.shape[-1])


def _to_slabs(src2d, row0, slab_ref):
    n = slab_ref.shape[1]
    for c in range(slab_ref.shape[0]):
        slab_ref[c] = src2d[row0:row0 + n, c * LANES:(c + 1) * LANES]


def _outer_batch(B, outer_rows):
    return max(1, min(B, 128 // outer_rows))


def _fft1_kernel(x_ref, f_ref, o_ref, xs_ref, os_ref):
    hin, rows = x_ref.shape[1], o_ref.shape[1]
    x2, o2 = _flat_rows(x_ref), _flat_rows(o_ref)
    f = f_ref[...]
    for b in range(x_ref.shape[0]):
        _to_slabs(x2, b * hin * FFT_SUB, xs_ref)
        for c in range(xs_ref.shape[0]):
            for p in range(0, FFT_SUB, 2):
                x = jnp.concatenate([xs_ref[c, pl.ds(p, hin, stride=FFT_SUB), :],
                                     xs_ref[c, pl.ds(p + 1, hin, stride=FFT_SUB), :]], axis=1)
                xh, xl = _split2(x)
                y = _dot(f, jnp.concatenate([xh, xl, xh], axis=0))
                os_ref[c, pl.ds(p, rows, stride=FFT_SUB), :] = y[:, :LANES]
                os_ref[c, pl.ds(p + 1, rows, stride=FFT_SUB), :] = y[:, LANES:]
            o2[b * rows * FFT_SUB:(b + 1) * rows * FFT_SUB, c * LANES:(c + 1) * LANES] = os_ref[c]


def _fft1(x, fcat, B, hin):
    C = x.shape[-1]
    rows = fcat.shape[0]
    nj = FFT_N2 // FFT_SUB
    bt = _outer_batch(B, hin)
    out = pl.pallas_call(
        _fft1_kernel,
        grid=(B // bt, nj),
        in_specs=[
            pl.BlockSpec((bt, hin, None, FFT_SUB, C), lambda b, j: (b, 0, j, 0, 0)),
            pl.BlockSpec(fcat.shape, lambda b, j: (0, 0)),
        ],
        out_specs=pl.BlockSpec((bt, rows, None, FFT_SUB, C), lambda b, j: (b, 0, j, 0, 0)),
        out_shape=jax.ShapeDtypeStruct((B, rows, nj, FFT_SUB, C), F32),
        scratch_shapes=[pltpu.VMEM((C // LANES, hin * FFT_SUB, LANES), F32),
                        pltpu.VMEM((C // LANES, rows * FFT_SUB, LANES), F32)],
        compiler_params=_cparams("parallel", "parallel"),
        name="fft1",
    )(x.reshape(B, hin, nj, FFT_SUB, C), fcat)
    return out.reshape(B, rows, FFT_N2, C)


def _fft2_kernel(ar_ref, ai_ref, tr_ref, ti_ref, m_ref, o_ref):
    m = m_ref[...]
    for r in range(FFT_KB):
        ar, ai = ar_ref[r], ai_ref[r]
        tr, ti = (jnp.tile(t[r], (1, FFT_CT // LANES)) for t in (tr_ref, ti_ref))
        b = jnp.concatenate([ar * tr - ai * ti, ar * ti + ai * tr], axis=0)
        bh, bl = _split2(b)
        o_ref[r] = _dot(m, jnp.concatenate([bh, bl, bh], axis=0))


def _fft2(a, tw_r, tw_i, mcat, kp):
    B = a.shape[0]
    C = a.shape[-1]
    nk = kp // FFT_KB
    return pl.pallas_call(
        _fft2_kernel,
        grid=(B, C // FFT_CT, nk),
        in_specs=[
            pl.BlockSpec((None, FFT_KB, FFT_N2, FFT_CT), lambda b, c, k: (b, k, 0, c)),
            pl.BlockSpec((None, FFT_KB, FFT_N2, FFT_CT), lambda b, c, k: (b, nk + k, 0, c)),
            pl.BlockSpec((FFT_KB, FFT_N2, LANES), lambda b, c, k: (k, 0, 0)),
            pl.BlockSpec((FFT_KB, FFT_N2, LANES), lambda b, c, k: (k, 0, 0)),
            pl.BlockSpec(mcat.shape, lambda b, c, k: (0, 0)),
        ],
        out_specs=pl.BlockSpec((None, FFT_KB, 2 * FFT_N2, FFT_CT), lambda b, c, k: (b, k, 0, c)),
        out_shape=jax.ShapeDtypeStruct((B, kp, 2 * FFT_N2, C), F32),
        compiler_params=_cparams("parallel", "parallel", "parallel"),
        name="fft2",
    )(a, a, tw_r, tw_i, mcat)


def _ifft2_kernel(x_ref, k_ref, ssq_ref, tr_ref, ti_ref, m_ref, o_ref):
    m = m_ref[...]
    scale = lax.rsqrt(ssq_ref[...] + EPS)
    n2 = FFT_N2
    for r in range(FFT_KB):
        x = x_ref[r]
        kf = k_ref[r]
        xr, xi = x[:n2], x[n2:]
        kr, ki = kf[:n2] * scale, kf[n2:] * scale
        p = jnp.concatenate([xr * kr - xi * ki, xr * ki + xi * kr], axis=0)
        ph, plo = _split2(p)
        c = _dot(m, jnp.concatenate([ph, plo, ph], axis=0))
        cr, ci = c[:n2], c[n2:]
        tr, ti = (jnp.tile(t[r], (1, FFT_CT // LANES)) for t in (tr_ref, ti_ref))
        o_ref[0, r] = cr * tr + ci * ti
        o_ref[1, r] = ci * tr - cr * ti


def _ifft2(xs, ks, ssq, tw_r, tw_i, mcat_inv):
    B, kp, _, C = xs.shape
    nk = kp // FFT_KB
    return pl.pallas_call(
        _ifft2_kernel,
        grid=(B, C // FFT_CT, nk),
        in_specs=[
            pl.BlockSpec((None, FFT_KB, 2 * FFT_N2, FFT_CT), lambda b, c, k: (b, k, 0, c)),
            pl.BlockSpec((None, FFT_KB, 2 * FFT_N2, FFT_CT), lambda b, c, k: (0, k, 0, c)),
            pl.BlockSpec((1, FFT_CT), lambda b, c, k: (0, c)),
            pl.BlockSpec((FFT_KB, FFT_N2, LANES), lambda b, c, k: (k, 0, 0)),
            pl.BlockSpec((FFT_KB, FFT_N2, LANES), lambda b, c, k: (k, 0, 0)),
            pl.BlockSpec(mcat_inv.shape, lambda b, c, k: (0, 0)),
        ],
        out_specs=pl.BlockSpec((None, 2, FFT_KB, FFT_N2, FFT_CT), lambda b, c, k: (b, 0, k, 0, c)),
        out_shape=jax.ShapeDtypeStruct((B, 2, kp, FFT_N2, C), F32),
        compiler_params=_cparams("parallel", "parallel", "parallel"),
        name="ifft2",
    )(xs, ks, ssq, tw_r, tw_i, mcat_inv)


def _ifft1_kernel(d_ref, w_ref, u_ref, x2_ref, skip_ref, o_ref, ds_ref, ys_ref):
    rows, h = d_ref.shape[1], o_ref.shape[1]
    d2, u2, x22, o2 = (_flat_rows(r) for r in (d_ref, u_ref, x2_ref, o_ref))
    w = w_ref[...]
    for b in range(d_ref.shape[0]):
        _to_slabs(d2, b * rows * FFT_SUB, ds_ref)
        seq = slice(b * h * FFT_SUB, (b + 1) * h * FFT_SUB)
        for c in range(ds_ref.shape[0]):
            lanes = slice(c * LANES, (c + 1) * LANES)
            for p in range(0, FFT_SUB, 2):
                d = jnp.concatenate([ds_ref[c, pl.ds(p, rows, stride=FFT_SUB), :],
                                     ds_ref[c, pl.ds(p + 1, rows, stride=FFT_SUB), :]], axis=1)
                dh, dl = _split2(d)
                y = _dot(w, jnp.concatenate([dh, dl, dh], axis=0))
                ys_ref[c, pl.ds(p, h, stride=FFT_SUB), :] = y[:, :LANES]
                ys_ref[c, pl.ds(p + 1, h, stride=FFT_SUB), :] = y[:, LANES:]
            o2[seq, lanes] = x22[seq, lanes] * (ys_ref[c] + u2[seq, lanes] * skip_ref[:, lanes])


def _ifft1(d, wcat, u, x2c, skip, B):
    rows, C = d.shape[1], d.shape[-1]
    h = wcat.shape[0]
    nj = FFT_N2 // FFT_SUB
    bt = _outer_batch(B, h)
    seq = pl.BlockSpec((bt, h, None, FFT_SUB, C), lambda b, j: (b, 0, j, 0, 0))
    out = pl.pallas_call(
        _ifft1_kernel,
        grid=(B // bt, nj),
        in_specs=[
            pl.BlockSpec((bt, rows, None, FFT_SUB, C), lambda b, j: (b, 0, j, 0, 0)),
            pl.BlockSpec(wcat.shape, lambda b, j: (0, 0)),
            seq, seq,
            pl.BlockSpec((1, C), lambda b, j: (0, 0)),
        ],
        out_specs=seq,
        out_shape=jax.ShapeDtypeStruct((B, h, nj, FFT_SUB, C), F32),
        scratch_shapes=[pltpu.VMEM((C // LANES, rows * FFT_SUB, LANES), F32),
                        pltpu.VMEM((C // LANES, h * FFT_SUB, LANES), F32)],
        compiler_params=_cparams("parallel", "parallel"),
        name="ifft1",
    )(d.reshape(B, rows, nj, FFT_SUB, C), wcat, u.reshape(B, h, nj, FFT_SUB, C), x2c.reshape(B, h, nj, FFT_SUB, C),
      skip)
    return out.reshape(B * h * FFT_N2, C)


def _hilo_cat(m):
    hi = m.astype(BF16)
    lo = (m - hi.astype(F32)).astype(BF16)
    return jnp.concatenate([hi, hi, lo], axis=1)


def _fft_tables(L):
    n = 2 * L
    n2 = FFT_N2
    n1 = n // n2
    h = n1 // 2
    kp = h + FFT_KB
    k1 = jnp.arange(kp, dtype=jnp.int32)

    def outer(hin):
        nn = jnp.arange(hin, dtype=jnp.int32)
        ang = (2.0 * math.pi / n1) * ((k1[:, None] * nn[None, :]) % n1).astype(F32)
        return _hilo_cat(jnp.concatenate([jnp.cos(ang), -jnp.sin(ang)], axis=0))

    f_data = outer(h)
    f_filt = outer(n1)
    j = jnp.arange(n2, dtype=jnp.int32)
    ang2 = (2.0 * math.pi / n2) * ((j[:, None] * j[None, :]) % n2).astype(F32)
    c2, s2 = jnp.cos(ang2), jnp.sin(ang2)
    m_fwd = _hilo_cat(jnp.block([[c2, s2], [-s2, c2]]))
    m_inv = _hilo_cat(jnp.block([[c2, -s2], [s2, c2]]))
    angt = (2.0 * math.pi / n) * (k1[:, None] * j[None, :]).astype(F32)
    tw_r = jnp.broadcast_to(jnp.cos(angt)[:, :, None], (kp, n2, LANES))
    tw_i = jnp.broadcast_to(-jnp.sin(angt)[:, :, None], (kp, n2, LANES))
    wgt = jnp.where((k1 == 0) | (k1 == h), 1.0, jnp.where(k1 < h, 2.0, 0.0)).astype(F32) / n
    nn = jnp.arange(h, dtype=jnp.int32)
    angi = (2.0 * math.pi / n1) * ((nn[:, None] * k1[None, :]) % n1).astype(F32)
    w_inv = _hilo_cat(jnp.concatenate([jnp.cos(angi) * wgt, -jnp.sin(angi) * wgt], axis=1))
    return dict(n1=n1, h=h, kp=kp, f_data=f_data, f_filt=f_filt, m_fwd=m_fwd, m_inv=m_inv,
                tw_r=tw_r, tw_i=tw_i, w_inv=w_inv)


def _filter_feats(L):
    i = jnp.arange(2 * L, dtype=jnp.int32)
    n = jnp.where(i < L, i, 2 * L - i).astype(F32)
    t = (n / (L - 1))[:, None]
    w = 2.0 * math.pi * n / L
    fb = jnp.linspace(1e-4, HY_BANDS - 1, HY_BANDS, dtype=F32)
    ph = w[:, None] * fb
    valid = jnp.where(i == L, 0.0, 1.0).astype(F32)[:, None]
    ext = jnp.concatenate([t, jnp.cos(ph), -jnp.sin(ph), valid], axis=-1)
    return jnp.pad(ext, ((0, 0), (0, LANES - HY_EMB - 1)))


def _hyena_spectrum(tabs, feats_ext, lp, L):
    kf, ssq = _hy_filt(feats_ext, lp["hy_w1p"], lp["hy_b1"], lp["hy_f1"], lp["hy_w2"], lp["hy_b2"], lp["hy_f2"],
                       lp["hy_w3s"], lp["hy_deltas"], L)
    a = _fft1(kf, tabs["f_filt"], 1, tabs["n1"])
    ks = _fft2(a, tabs["tw_r"], tabs["tw_i"], tabs["m_fwd"], tabs["kp"])
    return ks, ssq


def _hyena(z, tabs, ks, ssq, lp, B, L):
    u, x2c = _hy_pre(z, lp["hy_conv_w"], lp["hy_conv_b"], L)
    kp = tabs["kp"]
    a = _fft1(u, tabs["f_data"], B, tabs["h"])
    xs = _fft2(a, tabs["tw_r"], tabs["tw_i"], tabs["m_fwd"], kp)
    d = _ifft2(xs, ks, ssq, tabs["tw_r"], tabs["tw_i"], tabs["m_inv"])
    return _ifft1(d.reshape(B, 2 * kp, FFT_N2, HY_W), tabs["w_inv"], u, x2c, lp["hy_skip"], B)


def _gla_kernel(qf_ref, kf_ref, vf_ref, lf_ref, qb_ref, kb_ref, vb_ref, lb_ref, gup_ref, gb_ref,
                of_ref, ob_ref, sf_ref, sb_ref, *, nchunk):
    C = GLA_CHUNK
    R = nchunk * C

    @pl.when(pl.program_id(1) == 0)
    def _():
        sf_ref[...] = jnp.zeros_like(sf_ref)
        sb_ref[...] = jnp.zeros_like(sb_ref)

    rr = lax.broadcasted_iota(jnp.int32, (R, R), 0)
    cc = lax.broadcasted_iota(jnp.int32, (R, R), 1)
    same_chunk = (rr // C) == (cc // C)
    lane = lax.broadcasted_iota(jnp.int32, (C, GLA_K), 1) // GLA_DK
    ri4 = lax.broadcasted_iota(jnp.int32, (GLA_HEADS * C, C), 0) % C
    ci4 = lax.broadcasted_iota(jnp.int32, (GLA_HEADS * C, C), 1)

    def stack_heads(x):
        return jnp.concatenate([jnp.where(lane == h, x, 0.0) for h in range(GLA_HEADS)], axis=0).astype(BF16)

    def per_chunk_rows(b, row):
        return jnp.concatenate([jnp.broadcast_to(b[c * C + row:c * C + row + 1], (C, GLA_K)) for c in range(nchunk)],
                               axis=0)

    dirs = []
    for d, (q_ref, k_ref, v_ref, l_ref) in enumerate(((qf_ref, kf_ref, vf_ref, lf_ref),
                                                       (qb_ref, kb_ref, vb_ref, lb_ref))):
        logit = _dot3(l_ref[...], gup_ref[:, d * GLA_K:(d + 1) * GLA_K]) + gb_ref[:, d * GLA_K:(d + 1) * GLA_K]
        la = (jnp.minimum(logit, 0.0) - jnp.log(1.0 + jnp.exp(-jnp.abs(logit)))) * (1.0 / GLA_TAU)
        if d == 0:
            tri = jnp.where(same_chunk & (rr >= cc), 1.0, 0.0).astype(BF16)
            keep = ri4 >= ci4
            mid, last = C // 2 - 1, C - 1
            order = list(range(nchunk))
        else:
            tri = jnp.where(same_chunk & (cc >= rr), 1.0, 0.0).astype(BF16)
            keep = ci4 > ri4
            mid, last = C // 2, 0
            order = list(range(nchunk - 1, -1, -1))
        hi, lo = _split2(la)
        lo2 = (la - hi.astype(F32) - lo.astype(F32)).astype(BF16)
        b = _dot(tri, hi) + _dot(tri, lo) + _dot(tri, lo2)
        b_mid = per_chunk_rows(b, mid)
        b_last = per_chunk_rows(b, last)
        q = q_ref[...] * (GLA_DK ** -0.5)
        k = k_ref[...]
        dirs.append(dict(
            keep=keep, order=order, v=v_ref[...].astype(BF16),
            qt=q * jnp.exp(b - b_mid), kt=(k * jnp.exp(b_mid - b)).astype(BF16),
            kh=k * jnp.exp(b_last - b), qh=q * jnp.exp(b), dec=jnp.exp(b_last)))

    for dd in dirs:
        dd["a"] = []
        for c in range(nchunk):
            rs = slice(c * C, (c + 1) * C)
            a = lax.dot_general(stack_heads(dd["qt"][rs]), dd["kt"][rs], (((1,), (1,)), ((), ())),
                                preferred_element_type=F32)
            dd["a"].append(jnp.where(dd["keep"], a, 0.0).astype(BF16))

    for dd in dirs:
        dd["o"], dd["u"] = [], []
        for c in range(nchunk):
            rs = slice(c * C, (c + 1) * C)
            k_t = jnp.transpose(dd["kh"][rs]).astype(BF16)
            outs, ups = [], []
            for h in range(GLA_HEADS):
                vh = dd["v"][rs, h * GLA_DV:(h + 1) * GLA_DV]
                outs.append(_dot(dd["a"][c][h * C:(h + 1) * C], vh))
                ups.append(_dot(k_t[h * GLA_DK:(h + 1) * GLA_DK], vh))
            dd["o"].append(outs)
            dd["u"].append(jnp.concatenate(ups, axis=0))

    for dd, s_ref in zip(dirs, (sf_ref, sb_ref)):
        s = s_ref[...]
        dd["s_prev"] = {}
        for c in dd["order"]:
            dd["s_prev"][c] = s.astype(BF16)
            decay = jnp.transpose(jnp.broadcast_to(dd["dec"][c * C:c * C + 1], (GLA_DV, GLA_K)))
            s = decay * s + dd["u"][c]
        s_ref[...] = s

    for dd, o_ref in zip(dirs, (of_ref, ob_ref)):
        for c in range(nchunk):
            rs = slice(c * C, (c + 1) * C)
            o_inter = _dot(stack_heads(dd["qh"][rs]), dd["s_prev"][c])
            o_ref[rs, :] = jnp.concatenate(
                [dd["o"][c][h] + o_inter[h * C:(h + 1) * C] for h in range(GLA_HEADS)], axis=1)


def _gla(z, gup, gb, B, L, nchunk=4):
    n = z.shape[0]
    R = nchunk * GLA_CHUNK
    T = L // R
    fwd = lambda b, t: b * T + t
    bwd = lambda b, t: b * T + (T - 1 - t)

    def specs(row):
        return [
            pl.BlockSpec((R, GLA_K), lambda b, t: (row(b, t), Z_GQ // GLA_K)),
            pl.BlockSpec((R, GLA_K), lambda b, t: (row(b, t), Z_GK // GLA_K)),
            pl.BlockSpec((R, GLA_V), lambda b, t: (row(b, t), Z_GV // GLA_V)),
            pl.BlockSpec((R, LANES), lambda b, t: (row(b, t), Z_GLOW // LANES)),
        ]

    return pl.pallas_call(
        functools.partial(_gla_kernel, nchunk=nchunk),
        grid=(B, T),
        in_specs=specs(fwd) + specs(bwd) + [
            pl.BlockSpec(gup.shape, lambda b, t: (0, 0)),
            pl.BlockSpec(gb.shape, lambda b, t: (0, 0)),
        ],
        out_specs=[
            pl.BlockSpec((R, GLA_V), lambda b, t: (fwd(b, t), 0)),
            pl.BlockSpec((R, GLA_V), lambda b, t: (bwd(b, t), 0)),
        ],
        out_shape=[jax.ShapeDtypeStruct((n, GLA_V), F32)] * 2,
        scratch_shapes=[pltpu.VMEM((GLA_K, GLA_DV), F32)] * 2,
        compiler_params=_cparams("parallel", "arbitrary"),
        name="gla",
    )(*([z] * 8), gup, gb)


def _mix_out_kernel(x_ref, ya_ref, yh_ref, of_ref, ob_ref, og_ref, g0_ref, g1_ref, g2_ref,
                    wb_ref, wo_ref, gn_ref, o_ref):
    o = of_ref[...] + ob_ref[...]
    og = og_ref[...]
    gn = gn_ref[...]
    parts = []
    for h in range(GLA_HEADS):
        blk = o[:, h * GLA_DV:(h + 1) * GLA_DV]
        ms = jnp.mean(blk * blk, axis=-1, keepdims=True)
        gate = og[:, h * GLA_DV:(h + 1) * GLA_DV]
        parts.append(blk * lax.rsqrt(ms + EPS) * gn * (gate * jax.nn.sigmoid(gate)))
    y_gla = jnp.concatenate(parts, axis=1)
    proj_att = lax.dot_general(ya_ref[...], wb_ref[0], (((0,), (0,)), ((), ())),
                               preferred_element_type=F32)
    merged = jax.nn.sigmoid(g0_ref[...].astype(F32)) * proj_att
    merged += jax.nn.sigmoid(g1_ref[...].astype(F32)) * _dot(yh_ref[...].astype(BF16), wb_ref[1])
    merged += jax.nn.sigmoid(g2_ref[...].astype(F32)) * _dot(y_gla.astype(BF16), wb_ref[2])
    o_ref[...] = x_ref[...] + _dot(merged.astype(BF16), wo_ref[...])


def _mix_out(x, y_att, y_hy, o_f, o_b, z, zg, wb, wo, gn, tm=512):
    n = x.shape[0]
    row = lambda w: pl.BlockSpec((tm, w), lambda i: (i, 0))
    return pl.pallas_call(
        _mix_out_kernel,
        grid=(n // tm,),
        in_specs=[
            row(D_MODEL), pl.BlockSpec((MIX_W, tm), lambda i: (0, i)), row(MIX_W), row(MIX_W), row(MIX_W),
            pl.BlockSpec((tm, GLA_V), lambda i: (i, Z_GOG // GLA_V)),
            pl.BlockSpec((tm, D_MODEL), lambda i: (i, 0)),
            pl.BlockSpec((tm, D_MODEL), lambda i: (i, 1)),
            pl.BlockSpec((tm, D_MODEL), lambda i: (i, 2)),
            pl.BlockSpec(wb.shape, lambda i: (0, 0, 0)),
            pl.BlockSpec(wo.shape, lambda i: (0, 0)),
            pl.BlockSpec((1, GLA_DV), lambda i: (0, 0)),
        ],
        out_specs=row(D_MODEL),
        out_shape=jax.ShapeDtypeStruct((n, D_MODEL), F32),
        compiler_params=_cparams("parallel"),
        name="mix_out",
    )(x, y_att, y_hy, o_f, o_b, z, zg, zg, zg, wb, wo, gn)


def _ffn_kernel(x_ref, g_ref, wg_ref, wu_ref, wd_ref, o_ref, *, tf):
    x = x_ref[...]
    ms = jnp.mean(x * x, axis=-1, keepdims=True)
    h = (x * lax.rsqrt(ms + EPS) * g_ref[...]).astype(BF16)
    acc = x
    for c in range(0, D_FF, tf):
        a = _dot(h, wg_ref[:, c:c + tf])
        act = (a * jax.nn.sigmoid(a)) * _dot(h, wu_ref[:, c:c + tf])
        acc = acc + _dot(act.astype(BF16), wd_ref[c:c + tf, :])
    o_ref[...] = acc


def _ffn(x, g, wg, wu, wd, tm=512, tf=1408):
    n = x.shape[0]
    resident = lambda w: pl.BlockSpec(w.shape, lambda i: (0, 0), pipeline_mode=pl.Buffered(1))
    return pl.pallas_call(
        functools.partial(_ffn_kernel, tf=tf),
        grid=(n // tm,),
        in_specs=[
            pl.BlockSpec((tm, D_MODEL), lambda i: (i, 0)),
            pl.BlockSpec((1, D_MODEL), lambda i: (0, 0)),
            resident(wg), resident(wu), resident(wd),
        ],
        out_specs=pl.BlockSpec((tm, D_MODEL), lambda i: (i, 0)),
        out_shape=jax.ShapeDtypeStruct((n, D_MODEL), F32),
        compiler_params=_cparams("parallel"),
        name="ffn",
    )(x, g, wg, wu, wd)


def _final_norm_kernel(x_ref, g_ref, o_ref):
    x = x_ref[...]
    ms = jnp.mean(x * x, axis=-1, keepdims=True)
    o_ref[...] = x * lax.rsqrt(ms + EPS) * g_ref[...]


def _final_norm(x, g, tm=512):
    n = x.shape[0]
    return pl.pallas_call(
        _final_norm_kernel,
        grid=(n // tm,),
        in_specs=[pl.BlockSpec((tm, D_MODEL), lambda i: (i, 0)), pl.BlockSpec((1, D_MODEL), lambda i: (0, 0))],
        out_specs=pl.BlockSpec((tm, D_MODEL), lambda i: (i, 0)),
        out_shape=jax.ShapeDtypeStruct((n, D_MODEL), F32),
        compiler_params=_cparams("parallel"),
        name="final_norm",
    )(x, g)


def _rope_tables(L):
    rows = L // GRID_W
    r = jnp.repeat(jnp.arange(rows, dtype=F32), GRID_W)
    c = jnp.tile(jnp.arange(GRID_W, dtype=F32), rows)
    inv = ROPE_THETA ** (-jnp.arange(0, ROPE_AXIS_DIM, 2, dtype=F32) / ROPE_AXIS_DIM)
    ang = jnp.concatenate([r[:, None] * inv, c[:, None] * inv], axis=-1)
    cos, sin = jnp.cos(ang), jnp.sin(ang)
    cos_t = jnp.tile(cos, (1, 2 * LANES // HEAD_DIM))
    sin_t = jnp.tile(jnp.concatenate([-sin, sin], axis=-1), (1, LANES // HEAD_DIM))
    return cos_t, sin_t


def _layer_params(l, norm_mix_g, w_in, q_norm_g, k_norm_g, hy_conv_w, hy_conv_b, hy_w1, hy_b1, hy_f1, hy_w2,
                  hy_b2, hy_f2, hy_w3, hy_skip, gla_gate_up, gla_gate_b, gla_norm_g, w_branch, w_out, norm_ffn_g,
                  w_ffn_gate, w_ffn_up, w_ffn_down):
    w = w_in[l]
    offs = {}
    off = 0
    for name, size in (("aq", ATT_W), ("ak", KV_W), ("av", KV_W), ("hz", 3 * HY_W), ("gq", GLA_K), ("gk", GLA_K),
                       ("gv", GLA_V), ("gog", GLA_V), ("glow", 2 * GLA_RANK), ("gates", N_BRANCH * D_MODEL)):
        offs[name] = w[:, off:off + size]
        off += size
    w_cat = jnp.concatenate(
        [offs["aq"], offs["gv"], offs["gog"], offs["hz"], offs["gq"], offs["gk"], offs["ak"],
         offs["av"], offs["glow"], jnp.zeros((D_MODEL, Z_COLS - Z_GLOW - 2 * GLA_RANK), w.dtype)], axis=1).astype(BF16)
    gup = jnp.zeros((LANES, 2 * GLA_K), F32)
    gup = gup.at[0:GLA_RANK, 0:GLA_K].set(gla_gate_up[l, 0])
    gup = gup.at[GLA_RANK:2 * GLA_RANK, GLA_K:2 * GLA_K].set(gla_gate_up[l, 1])
    deltas = jnp.linspace(abs(math.log(HY_TARGET) / HY_SLOW), abs(math.log(HY_TARGET) / HY_FAST), HY_W, dtype=F32)
    return dict(
        norm_mix_g=norm_mix_g[l][None, :],
        w_cat=w_cat, w_gates=offs["gates"].astype(BF16),
        gq=jnp.tile(q_norm_g[l], LANES // HEAD_DIM)[None, :],
        gk=jnp.tile(k_norm_g[l], LANES // HEAD_DIM)[None, :],
        hy_conv_w=jnp.transpose(hy_conv_w[l].reshape(3, 3, HY_W), (1, 0, 2)),
        hy_conv_b=hy_conv_b[l].reshape(3, HY_W),
        hy_w1p=jnp.kron(jnp.eye(2, dtype=F32), jnp.pad(hy_w1[l], ((0, LANES - HY_EMB), (0, 0)))),
        hy_b1=jnp.tile(hy_b1[l], 2)[None, :], hy_f1=jnp.tile(hy_f1[l], 2)[None, :],
        hy_w2=jnp.kron(jnp.eye(2, dtype=F32), hy_w2[l]),
        hy_b2=jnp.tile(hy_b2[l], 2)[None, :], hy_f2=jnp.tile(hy_f2[l], 2)[None, :],
        hy_w3s=jnp.einsum("gh,kfc->fghkc", jnp.eye(2, dtype=F32), hy_w3[l].reshape(HY_FILTER_W, 2, HY_W)
                          ).reshape(2, 2, 2 * HY_FILTER_W, HY_W),
        hy_deltas=deltas[None, :],
        hy_skip=hy_skip[l][None, :],
        gup=gup, gb=gla_gate_b[l].reshape(1, 2 * GLA_K),
        gn=gla_norm_g[l][None, :],
        wb=w_branch[l].astype(BF16), wo=w_out[l].astype(BF16),
        norm_ffn_g=norm_ffn_g[l][None, :],
        wg=w_ffn_gate[l].astype(BF16), wu=w_ffn_up[l].astype(BF16), wd=w_ffn_down[l].astype(BF16),
    )


def _encoder_layer(x, lp, B, L, rope, tabs, ks, ssq):
    z, zg = _in_proj(x, lp["norm_mix_g"], lp["w_cat"], lp["w_gates"])
    qt, kr, vt = _attn_prep(z, rope[0], rope[1], lp["gq"], lp["gk"], L)
    y_att = _flash(qt, kr, vt, B, L)
    y_hy = _hyena(z, tabs, ks, ssq, lp, B, L)
    o_f, o_b = _gla(z, lp["gup"], lp["gb"], B, L)
    x = _mix_out(x, y_att, y_hy, o_f, o_b, z, zg, lp["wb"], lp["wo"], lp["gn"])
    return _ffn(x, lp["norm_ffn_g"], lp["wg"], lp["wu"], lp["wd"])


def kernel(x_prompt, x_sample, norm_mix_g, w_in, q_norm_g, k_norm_g, hy_conv_w, hy_conv_b, hy_w1, hy_b1, hy_f1, hy_w2, hy_b2, hy_f2, hy_w3, hy_skip, gla_gate_up, gla_gate_b, gla_norm_g, w_branch, w_out, norm_ffn_g, w_ffn_gate, w_ffn_up, w_ffn_down, final_norm_g):
    streams = []
    for xin in (x_prompt, x_sample):
        B, L, _ = xin.shape
        streams.append(dict(x=xin.reshape(B * L, D_MODEL), B=B, L=L, rope=_rope_tables(L), tabs=_fft_tables(L),
                            feats=_filter_feats(L)))
    w_in, w_branch, w_out, w_ffn_gate, w_ffn_up, w_ffn_down = (
        w.astype(BF16) for w in (w_in, w_branch, w_out, w_ffn_gate, w_ffn_up, w_ffn_down))
    for l in range(DEPTH):
        lp = _layer_params(l, norm_mix_g, w_in, q_norm_g, k_norm_g, hy_conv_w, hy_conv_b, hy_w1, hy_b1, hy_f1,
                           hy_w2, hy_b2, hy_f2, hy_w3, hy_skip, gla_gate_up, gla_gate_b, gla_norm_g, w_branch,
                           w_out, norm_ffn_g, w_ffn_gate, w_ffn_up, w_ffn_down)
        for s in streams:
            ks, ssq = _hyena_spectrum(s["tabs"], s["feats"], lp, s["L"])
            s["x"] = _encoder_layer(s["x"], lp, s["B"], s["L"], s["rope"], s["tabs"], ks, ssq)
    outs = []
    for s, xin in zip(streams, (x_prompt, x_sample)):
        outs.append(_final_norm(s["x"], final_norm_g[None, :]).reshape(xin.shape))
    return tuple(outs)
```
